```python
import jax, jax.numpy as jnp
from jax import lax
import numpy as np

D_MODEL = 1024
BATCH = 2
SEQ = 8192
DEPTH = 1

MLA_HEADS = 8
MLA_Q_RANK = 384
MLA_KV_RANK = 256
MLA_NOPE = 64
MLA_ROPE = 32
MLA_V = 64
ATTN_QBLOCK = 128
RET_HEADS = 4
RET_QK = 128
RET_V = 256
RET_CHUNK = 128
ROPE_THETA = 10000.0
N_EXPERTS = 64
TOP_K = 8
N_GROUPS = 8
TOPK_GROUPS = 4
D_EXPERT = 256
D_SHARED = 256
ROUTED_SCALE = 2.5
MOE_BLOCK = 128
RMS_EPS = 1e-6
GN_EPS = 1e-5

IN_SPLITS = [MLA_Q_RANK, MLA_KV_RANK, MLA_ROPE,
             RET_HEADS * RET_QK, RET_HEADS * RET_QK, RET_HEADS * RET_V, RET_HEADS * RET_V,
             D_MODEL, D_MODEL]
D_IN = int(sum(IN_SPLITS))
IN_SPLIT_IDX = [int(v) for v in np.cumsum(IN_SPLITS)[:-1]]

kernel_name = "hybrid_mla_retention_moe_adaln"


def rmsnorm(x, g):
    xf = x.astype(jnp.float32)
    y = xf * lax.rsqrt(jnp.mean(xf * xf, axis=-1, keepdims=True) + RMS_EPS)
    return y.astype(x.dtype) * g


def rope_tables(positions, dim, dtype):
    inv = 1.0 / (ROPE_THETA ** (jnp.arange(0, dim, 2, dtype=jnp.float32) / dim))
    ang = positions.astype(jnp.float32)[..., None] * inv
    return jnp.cos(ang).astype(dtype), jnp.sin(ang).astype(dtype)


def rope(x, cos, sin):
    half = x.shape[-1] // 2
    x1, x2 = x[..., :half], x[..., half:]
    return jnp.concatenate([x1 * cos - x2 * sin, x2 * cos + x1 * sin], axis=-1)


def mla_branch(cq, ckv, krope, positions, g_cq, w_uq, g_ckv, w_ukv):
    B, S, _ = cq.shape
    H = MLA_HEADS
    cos, sin = rope_tables(positions, MLA_ROPE, cq.dtype)
    q = (rmsnorm(cq, g_cq) @ w_uq).reshape(B, S, H, MLA_NOPE + MLA_ROPE)
    q_nope, q_pe = q[..., :MLA_NOPE], q[..., MLA_NOPE:]
    q_pe = rope(q_pe, cos[:, :, None, :], sin[:, :, None, :])
    kv = (rmsnorm(ckv, g_ckv) @ w_ukv).reshape(B, S, H, MLA_NOPE + MLA_V)
    k_nope, v = kv[..., :MLA_NOPE], kv[..., MLA_NOPE:]
    k_pe = rope(krope, cos, sin)[:, :, None, :]
    q = jnp.concatenate([q_nope, q_pe], axis=-1)
    k = jnp.concatenate([k_nope, jnp.broadcast_to(k_pe, (B, S, H, MLA_ROPE))], axis=-1)
    scale = (MLA_NOPE + MLA_ROPE) ** -0.5
    n_blocks = S // ATTN_QBLOCK
    key_idx = jnp.arange(S)

    def attend_block(i):
        qb = lax.dynamic_slice_in_dim(q, i * ATTN_QBLOCK, ATTN_QBLOCK, axis=1)
        s = jnp.einsum('bqhd,bkhd->bhqk', qb, k).astype(jnp.float32) * scale
        q_idx = i * ATTN_QBLOCK + jnp.arange(ATTN_QBLOCK)
        causal = q_idx[:, None] >= key_idx[None, :]
        s = jnp.where(causal[None, None], s, -jnp.inf)
        p = jax.nn.softmax(s, axis=-1).astype(v.dtype)
        return jnp.einsum('bhqk,bkhd->bqhd', p, v)

    o = lax.map(attend_block, jnp.arange(n_blocks))
    o = jnp.moveaxis(o, 0, 1).reshape(B, S, H * MLA_V)
    return o


def retention_branch(rq, rk, rv, rg, positions, g_ret):
    B, S, _ = rq.shape
    H, C = RET_HEADS, RET_CHUNK
    N = S // C
    dtype = rq.dtype
    cos, sin = rope_tables(positions, RET_QK, dtype)
    q = rope(rq.reshape(B, S, H, RET_QK), cos[:, :, None, :], sin[:, :, None, :])
    k = rope(rk.reshape(B, S, H, RET_QK), cos[:, :, None, :], sin[:, :, None, :]) * (RET_QK ** -0.5)
    v = rv.reshape(B, S, H, RET_V)

    def to_chunks(t):
        return t.astype(jnp.float32).reshape(B, N, C, H, t.shape[-1]).transpose(0, 3, 1, 2, 4)

    q, k, v = to_chunks(q), to_chunks(k), to_chunks(v)
    gamma = 1.0 - jnp.exp2(-5.0 - jnp.arange(H, dtype=jnp.float32))
    log_g = jnp.log(gamma)
    idx = jnp.arange(C, dtype=jnp.float32)
    diff = idx[:, None] - idx[None, :]
    decay = jnp.where(diff[None] >= 0, jnp.exp(jnp.maximum(diff, 0.0)[None] * log_g[:, None, None]), 0.0)
    scores = jnp.einsum('bhncd,bhnmd->bhncm', q, k) * decay[None, :, None]
    inner = jnp.einsum('bhncm,bhnme->bhnce', scores, v)
    zeta = jnp.exp((C - 1 - idx)[None, :] * log_g[:, None])
    kv = jnp.einsum('bhnmd,bhnme->bhnde', k * zeta[None, :, None, :, None], v)
    g_chunk = jnp.exp(C * log_g)[None, :, None, None]

    def step(state, kv_n):
        return g_chunk * state + kv_n, state

    _, s_prev = lax.scan(step, jnp.zeros((B, H, RET_QK, RET_V), jnp.float32), jnp.moveaxis(kv, 2, 0))
    s_prev = jnp.moveaxis(s_prev, 0, 2)
    xi = jnp.exp((idx + 1.0)[None, :] * log_g[:, None])
    cross = jnp.einsum('bhncd,bhnde->bhnce', q, s_prev) * xi[None, :, None, :, None]
    o = (inner + cross).transpose(0, 2, 3, 1, 4).reshape(B, S, H, RET_V)
    mu = jnp.mean(o, axis=-1, keepdims=True)
    var = jnp.mean(jnp.square(o - mu), axis=-1, keepdims=True)
    o = ((o - mu) * lax.rsqrt(var + GN_EPS)).reshape(B, S, H * RET_V).astype(dtype) * g_ret
    return jax.nn.silu(rg) * o


def moe_ffn(h, w_router, b_router, w_exp_gate, w_exp_up, w_exp_down, w_sh_gate, w_sh_up, w_sh_down):
    T, D = h.shape
    E, G = N_EXPERTS, N_GROUPS
    s = jax.nn.sigmoid((h @ w_router).astype(jnp.float32))
    biased = s + b_router.astype(jnp.float32)
    grp_score = lax.top_k(biased.reshape(T, G, E // G), 2)[0].sum(-1)
    _, grp_idx = lax.top_k(grp_score, TOPK_GROUPS)
    grp_mask = jax.nn.one_hot(grp_idx, G, dtype=jnp.float32).sum(1)
    exp_mask = jnp.repeat(grp_mask, E // G, axis=1) > 0
    _, top_idx = lax.top_k(jnp.where(exp_mask, biased, -jnp.inf), TOP_K)
    w = jnp.take_along_axis(s, top_idx, axis=1)
    w = w / jnp.sum(w, axis=-1, keepdims=True) * ROUTED_SCALE
    combine = jnp.sum(jax.nn.one_hot(top_idx, E, dtype=jnp.float32) * w[..., None], axis=1).astype(h.dtype)
    nb = T // MOE_BLOCK

    def expert_block(args):
        hb, cb = args
        g = jnp.einsum('td,edf->tef', hb, w_exp_gate)
        u = jnp.einsum('td,edf->tef', hb, w_exp_up)
        a = jax.nn.silu(g) * u * cb[:, :, None]
        return jnp.einsum('tef,efd->td', a, w_exp_down)

    routed = lax.map(expert_block, (h.reshape(nb, MOE_BLOCK, D), combine.reshape(nb, MOE_BLOCK, E))).reshape(T, D)
    shared = (jax.nn.silu(h @ w_sh_gate) * (h @ w_sh_up)) @ w_sh_down
    return routed + shared


def setup_inputs(seed: int = 0) -> dict:
    key = jax.random.key(seed)
    ks = jax.random.split(key, 32)
    D = D_MODEL
    f32 = jnp.float32

    def nrm(k, shape, fan_in, mult=1.0):
        return jax.random.normal(k, shape, f32) * (fan_in ** -0.5) * mult

    def gain(k, n):
        return 1.0 + 0.05 * jax.random.normal(k, (n,), f32)

    offsets = jax.random.randint(ks[2], (BATCH, 1), 0, 4096, dtype=jnp.int32)
    positions = offsets + jnp.arange(SEQ, dtype=jnp.int32)[None, :]
    return {
        "x": jax.random.normal(ks[0], (BATCH, SEQ, D), f32),
        "c": jax.random.normal(ks[1], (BATCH, D), f32),
        "positions": positions,
        "w_ada": nrm(ks[3], (D, 6 * D), D, 0.5),
        "b_ada": 0.02 * jax.random.normal(ks[4], (6 * D,), f32),
        "g_norm1": gain(ks[5], D),
        "w_in": nrm(ks[6], (D, D_IN), D),
        "g_cq": gain(ks[7], MLA_Q_RANK),
        "w_uq": nrm(ks[8], (MLA_Q_RANK, MLA_HEADS * (MLA_NOPE + MLA_ROPE)), MLA_Q_RANK),
        "g_ckv": gain(ks[9], MLA_KV_RANK),
        "w_ukv": nrm(ks[10], (MLA_KV_RANK, MLA_HEADS * (MLA_NOPE + MLA_V)), MLA_KV_RANK),
        "g_ret": gain(ks[11], RET_HEADS * RET_V),
        "w_o_mla": nrm(ks[12], (MLA_HEADS * MLA_V, D), MLA_HEADS * MLA_V),
        "w_o_ret": nrm(ks[13], (RET_HEADS * RET_V, D), RET_HEADS * RET_V),
        "w_out": nrm(ks[14], (D, D), D),
        "g_norm2": gain(ks[15], D),
        "w_router": nrm(ks[16], (D, N_EXPERTS), D),
        "b_router": 0.01 * jax.random.normal(ks[17], (N_EXPERTS,), f32),
        "w_exp_gate": nrm(ks[18], (N_EXPERTS, D, D_EXPERT), D),
        "w_exp_up": nrm(ks[19], (N_EXPERTS, D, D_EXPERT), D),
        "w_exp_down": nrm(ks[20], (N_EXPERTS, D_EXPERT, D), D_EXPERT),
        "w_sh_gate": nrm(ks[21], (D, D_SHARED), D),
        "w_sh_up": nrm(ks[22], (D, D_SHARED), D),
        "w_sh_down": nrm(ks[23], (D_SHARED, D), D_SHARED),
        "g_final": gain(ks[24], D),
    }


def reference(x, c, positions, w_ada, b_ada, g_norm1, w_in, g_cq, w_uq, g_ckv, w_ukv, g_ret,
              w_o_mla, w_o_ret, w_out, g_norm2, w_router, b_router, w_exp_gate, w_exp_up,
              w_exp_down, w_sh_gate, w_sh_up, w_sh_down, g_final):
    B, S, D = x.shape
    for _ in range(DEPTH):
        mod = jax.nn.silu(c) @ w_ada + b_ada
        sh1, sc1, gt1, sh2, sc2, gt2 = [m[:, None, :] for m in jnp.split(mod, 6, axis=-1)]
        h = rmsnorm(x, g_norm1) * (1.0 + sc1) + sh1
        z = h @ w_in
        cq, ckv, krope, rq, rk, rv, rg, ga, gb = jnp.split(z, IN_SPLIT_IDX, axis=-1)
        branch_a = mla_branch(cq, ckv, krope, positions, g_cq, w_uq, g_ckv, w_ukv) @ w_o_mla
        branch_b = retention_branch(rq, rk, rv, rg, positions, g_ret) @ w_o_ret
        merged = jax.nn.sigmoid(ga) * branch_a + jax.nn.sigmoid(gb) * branch_b
        x = x + gt1 * (merged @ w_out)
        h2 = rmsnorm(x, g_norm2) * (1.0 + sc2) + sh2
        ffn = moe_ffn(h2.reshape(B * S, D), w_router, b_router, w_exp_gate, w_exp_up, w_exp_down,
                      w_sh_gate, w_sh_up, w_sh_down).reshape(B, S, D)
        x = x + gt2 * ffn
    return rmsnorm(x, g_final)
```

```python
import functools
import math

import numpy as np
import jax
import jax.numpy as jnp
from jax import lax
from jax.experimental import pallas as pl
from jax.experimental.pallas import tpu as pltpu

F32 = jnp.float32
BF16 = jnp.bfloat16
I32 = jnp.int32

MLA_HEADS = 8
MLA_Q_RANK = 384
MLA_KV_RANK = 256
MLA_NOPE = 64
MLA_ROPE = 32
MLA_V = 64
RET_HEADS = 4
RET_QK = 128
RET_V = 256
RET_CHUNK = 128
ROPE_THETA = 10000.0
N_EXPERTS = 64
TOP_K = 8
N_GROUPS = 8
TOPK_GROUPS = 4
D_EXPERT = 256
ROUTED_SCALE = 2.5
RMS_EPS = 1e-6
GN_EPS = 1e-5

LANES = 128
SUBLANES = 8
VMEM_LIMIT = 56 * 1024 * 1024

Z_RV, Z_RG, Z_GA, Z_GB = 0, 1024, 2048, 3072
Z_RQ, Z_RK = 4096, 4608
Z_CQKV = 5120
Z_KR1, Z_KR2 = 5760, 5888
Z_COLS = 6144

LOG2E = 1.4426950408889634

MOE_TB = 1024
MOE_CH = 128
MOE_ROWS = TOP_K * MOE_TB + N_EXPERTS * SUBLANES + MOE_CH


def _cparams(sem):
    return pltpu.CompilerParams(dimension_semantics=sem, vmem_limit_bytes=VMEM_LIMIT)


def _rms(x):
    return x * lax.rsqrt(jnp.mean(x * x, axis=-1, keepdims=True) + RMS_EPS)


def _silu(x):
    return x * jax.nn.sigmoid(x)


def _ada_kernel(c_ref, w_ref, b_ref, o_ref):
    c = c_ref[...]
    o_ref[...] = jnp.dot(_silu(c).astype(BF16), w_ref[...].astype(BF16),
                         preferred_element_type=F32) + b_ref[...]


def _ada(c, w_ada, b_ada):
    B, D = c.shape
    n_out = w_ada.shape[1]
    cp = jnp.zeros((SUBLANES, D), F32).at[:B].set(c)
    tn = D
    out = pl.pallas_call(
        _ada_kernel,
        grid=(n_out // tn,),
        in_specs=[pl.BlockSpec((SUBLANES, D), lambda j: (0, 0)),
                  pl.BlockSpec((D, tn), lambda j: (0, j)),
                  pl.BlockSpec((1, tn), lambda j: (0, j))],
        out_specs=pl.BlockSpec((SUBLANES, tn), lambda j: (0, j)),
        out_shape=jax.ShapeDtypeStruct((SUBLANES, n_out), F32),
        compiler_params=_cparams(("arbitrary",)),
        name="ada_mod",
    )(cp, w_ada, b_ada.reshape(1, n_out))
    return out[:B]


def _rope_kernel(pos_ref, inv_ref, cr_ref, sr_ref, cm_ref, sm_ref):
    ang = pos_ref[...].astype(F32) * inv_ref[...]
    c = jnp.cos(ang)
    s = jnp.sin(ang)
    lane = lax.broadcasted_iota(I32, c.shape, 1)
    half = RET_QK // 2
    cr_ref[...] = jnp.where(lane < half, c, pltpu.roll(c, half, 1))
    sr_ref[...] = jnp.where(lane < half, -s, pltpu.roll(s, half, 1))
    hm = MLA_ROPE // 2
    cm_ref[...] = jnp.where(lane < MLA_NOPE, 1.0,
                            jnp.where(lane < MLA_NOPE + hm, c,
                                      jnp.where(lane < MLA_NOPE + 2 * hm, pltpu.roll(c, hm, 1), 0.0)))
    sm_ref[...] = jnp.where(lane < MLA_NOPE, 0.0,
                            jnp.where(lane < MLA_NOPE + hm, -s,
                                      jnp.where(lane < MLA_NOPE + 2 * hm, pltpu.roll(s, hm, 1), 0.0)))


def _rope_tables(positions):
    T = positions.size
    tm = min(T, 1024)
    inv_r = 1.0 / (ROPE_THETA ** (jnp.arange(0, RET_QK, 2, dtype=F32) / RET_QK))
    inv_m = 1.0 / (ROPE_THETA ** (jnp.arange(0, MLA_ROPE, 2, dtype=F32) / MLA_ROPE))
    inv = jnp.zeros((1, LANES), F32).at[0, :RET_QK // 2].set(inv_r)
    inv = inv.at[0, MLA_NOPE:MLA_NOPE + MLA_ROPE // 2].set(inv_m)
    tab = jax.ShapeDtypeStruct((T, LANES), F32)
    spec = pl.BlockSpec((tm, LANES), lambda i: (i, 0))
    return pl.pallas_call(
        _rope_kernel,
        grid=(T // tm,),
        in_specs=[pl.BlockSpec((tm, 1), lambda i: (i, 0)),
                  pl.BlockSpec((1, LANES), lambda i: (0, 0))],
        out_specs=[spec, spec, spec, spec],
        out_shape=[tab, tab, tab, tab],
        compiler_params=_cparams(("arbitrary",)),
        name="rope_tables",
    )(positions.reshape(T, 1), inv)


def _inproj_kernel(x_ref, sc_ref, sh_ref, g_ref, w_ref, z_ref, h_scr):
    @pl.when(pl.program_id(1) == 0)
    def _():
        h = _rms(x_ref[...]) * g_ref[...] * (1.0 + sc_ref[...]) + sh_ref[...]
        h_scr[...] = h.astype(BF16)

    z_ref[...] = jnp.dot(h_scr[...], w_ref[...], preferred_element_type=F32).astype(z_ref.dtype)


def _inproj(x2, mod3, g_norm1, w_pack, S):
    T, D = x2.shape
    N = w_pack.shape[1]
    tm, tn = 512, 2048
    per_b = S // tm
    return pl.pallas_call(
        _inproj_kernel,
        grid=(T // tm, N // tn),
        in_specs=[pl.BlockSpec((tm, D), lambda i, j: (i, 0)),
                  pl.BlockSpec((None, 1, D), lambda i, j: ((i // per_b) * 6 + 1, 0, 0)),
                  pl.BlockSpec((None, 1, D), lambda i, j: ((i // per_b) * 6 + 0, 0, 0)),
                  pl.BlockSpec((1, D), lambda i, j: (0, 0)),
                  pl.BlockSpec((D, tn), lambda i, j: (0, j))],
        out_specs=pl.BlockSpec((tm, tn), lambda i, j: (i, j)),
        out_shape=jax.ShapeDtypeStruct((T, N), BF16),
        scratch_shapes=[pltpu.VMEM((tm, D), BF16)],
        compiler_params=_cparams(("arbitrary", "arbitrary")),
        name="in_proj",
    )(x2, mod3, mod3, g_norm1.reshape(1, D), w_pack)


def _mla_up_kernel(zc_ref, kr1_ref, kr2_ref, cm_ref, sm_ref, gq_ref, gkv_ref,
                   wq1_ref, wq2_ref, wk_ref, wv_ref, q_ref, k_ref, v_ref):
    zc = zc_ref[...].astype(F32)
    cqn = (_rms(zc[:, :MLA_Q_RANK]) * gq_ref[...]).astype(BF16)
    ckvn = (_rms(zc[:, MLA_Q_RANK:]) * gkv_ref[...]).astype(BF16)
    cm = cm_ref[...]
    sm = sm_ref[...]
    q1 = jnp.dot(cqn, wq1_ref[...], preferred_element_type=F32)
    q2 = jnp.dot(cqn, wq2_ref[...], preferred_element_type=F32)
    kn = jnp.dot(ckvn, wk_ref[...], preferred_element_type=F32)
    kpe = kr1_ref[...].astype(F32) * cm + kr2_ref[...].astype(F32) * sm
    qscale = (MLA_NOPE + MLA_ROPE) ** -0.5 * LOG2E
    for h in range(MLA_HEADS):
        sl = slice(h * LANES, (h + 1) * LANES)
        q_ref[:, sl] = ((q1[:, sl] * cm + q2[:, sl] * sm) * qscale).astype(BF16)
        k_ref[:, sl] = (kn[:, sl] + kpe).astype(BF16)
    v_ref[...] = jnp.dot(ckvn, wv_ref[...], preferred_element_type=F32).astype(BF16)


def _mla_up(z, cm, sm, g_cq, g_ckv, wq1, wq2, wk, wv):
    T = z.shape[0]
    tm = 512
    HW = MLA_HEADS * LANES
    wc = MLA_Q_RANK + MLA_KV_RANK
    full = lambda shape: pl.BlockSpec(shape, lambda i: (0, 0))
    return pl.pallas_call(
        _mla_up_kernel,
        grid=(T // tm,),
        in_specs=[pl.BlockSpec((tm, wc), lambda i: (i, Z_CQKV // wc)),
                  pl.BlockSpec((tm, LANES), lambda i: (i, Z_KR1 // LANES)),
                  pl.BlockSpec((tm, LANES), lambda i: (i, Z_KR2 // LANES)),
                  pl.BlockSpec((tm, LANES), lambda i: (i, 0)),
                  pl.BlockSpec((tm, LANES), lambda i: (i, 0)),
                  full((1, MLA_Q_RANK)), full((1, MLA_KV_RANK)),
                  full(wq1.shape), full(wq2.shape), full(wk.shape), full(wv.shape)],
        out_specs=[pl.BlockSpec((tm, HW), lambda i: (i, 0)),
                   pl.BlockSpec((tm, HW), lambda i: (i, 0)),
                   pl.BlockSpec((tm, MLA_HEADS * MLA_V), lambda i: (i, 0))],
        out_shape=[jax.ShapeDtypeStruct((T, HW), BF16),
                   jax.ShapeDtypeStruct((T, HW), BF16),
                   jax.ShapeDtypeStruct((T, MLA_HEADS * MLA_V), BF16)],
        compiler_params=_cparams(("arbitrary",)),
        name="mla_up",
    )(z, z, z, cm, sm, g_cq.reshape(1, -1), g_ckv.reshape(1, -1), wq1, wq2, wk, wv)


def _attn_kernel(q_ref, k_ref, v_ref, o_ref, m_scr, l_scr, acc_scr, *, bq, bk):
    qi = pl.program_id(2)
    row = lax.broadcasted_iota(I32, (bq, bk), 0)
    col = lax.broadcasted_iota(I32, (bq, bk), 1)
    lane = lax.broadcasted_iota(I32, (bq, LANES), 1)
    nsub = bq // bk
    outs = []
    for h in range(2):
        q = q_ref[:, h * LANES:(h + 1) * LANES]
        m_scr[...] = jnp.full(m_scr.shape, -jnp.inf, F32)
        l_scr[...] = jnp.zeros(l_scr.shape, F32)
        acc_scr[...] = jnp.zeros(acc_scr.shape, F32)

        def block(kb, diag_off):
            k0 = pl.multiple_of(kb * bk, bk)
            kblk = k_ref[pl.ds(k0, bk), h * LANES:(h + 1) * LANES]
            vblk = v_ref[pl.ds(k0, bk), :]
            s = lax.dot_general(q, kblk, (((1,), (1,)), ((), ())), preferred_element_type=F32)
            if diag_off is not None:
                s = jnp.where(row >= col + diag_off, s, -jnp.inf)
            m_prev = m_scr[...]
            m_new = jnp.maximum(m_prev, jnp.max(s, axis=1, keepdims=True))
            alpha = jnp.exp2(m_prev - m_new)
            p = jnp.exp2(s - m_new[:, :1])
            l_scr[...] = alpha * l_scr[...] + jnp.sum(p, axis=1, keepdims=True)
            acc_scr[...] = alpha * acc_scr[...] + jnp.dot(p.astype(BF16), vblk,
                                                         preferred_element_type=F32)
            m_scr[...] = m_new

        def body(kb, carry):
            block(kb, None)
            return carry

        lax.fori_loop(0, qi * nsub, body, 0)
        for d in range(nsub):
            block(qi * nsub + d, d * bk)
        outs.append(acc_scr[...] / l_scr[...])
    o_ref[...] = jnp.where(lane < MLA_V, outs[0], outs[1]).astype(o_ref.dtype)


def _attention(q, k, v, B, S):
    T = q.shape[0]
    bq = min(512, S)
    bk = min(512, S)
    nq = S // bq
    hp = MLA_HEADS // 2
    kern = functools.partial(_attn_kernel, bq=bq, bk=bk)
    return pl.pallas_call(
        kern,
        grid=(B, hp, nq),
        in_specs=[pl.BlockSpec((bq, 2 * LANES), lambda b, p, i: (b * nq + i, p)),
                  pl.BlockSpec((S, 2 * LANES), lambda b, p, i: (b, p)),
                  pl.BlockSpec((S, 2 * MLA_V), lambda b, p, i: (b, p))],
        out_specs=pl.BlockSpec((bq, 2 * MLA_V), lambda b, p, i: (b * nq + i, p)),
        out_shape=jax.ShapeDtypeStruct((T, MLA_HEADS * MLA_V), BF16),
        scratch_shapes=[pltpu.VMEM((bq, LANES), F32),
                        pltpu.VMEM((bq, LANES), F32),
                        pltpu.VMEM((bq, LANES), F32)],
        compiler_params=_cparams(("arbitrary", "arbitrary", "arbitrary")),
        name="mla_attention",
    )(q, k, v)


def _ret_kernel(rq_ref, rk_ref, rv_ref, rg_ref, cr_ref, sr_ref, dec_ref, xi_ref, zeta_ref,
                g_ref, o_ref, state_scr):
    @pl.when(pl.program_id(2) == 0)
    def _():
        state_scr[...] = jnp.zeros(state_scr.shape, F32)

    cr = cr_ref[...]
    sr = sr_ref[...]
    half = RET_QK // 2
    rq = rq_ref[...].astype(F32)
    rk = rk_ref[...].astype(F32)
    q = rq * cr + pltpu.roll(rq, half, 1) * sr
    k = (rk * cr + pltpu.roll(rk, half, 1) * sr) * (RET_QK ** -0.5)
    v = rv_ref[...]
    xi = xi_ref[...]
    state = state_scr[...]
    qb = q.astype(BF16)
    s = lax.dot_general(qb, k.astype(BF16), (((1,), (1,)), ((), ())),
                        preferred_element_type=F32) * dec_ref[...]
    inner = jnp.dot(s.astype(BF16), v, preferred_element_type=F32)
    cross = jnp.dot(qb, state.astype(BF16), preferred_element_type=F32) * xi
    kz = (k * zeta_ref[...]).astype(BF16)
    kv = lax.dot_general(kz, v, (((0,), (0,)), ((), ())), preferred_element_type=F32)
    state_scr[...] = xi[RET_CHUNK - 1:RET_CHUNK, :] * state + kv
    o = inner + cross
    mu = jnp.mean(o, axis=-1, keepdims=True)
    d = o - mu
    var = jnp.mean(d * d, axis=-1, keepdims=True)
    on = d * lax.rsqrt(var + GN_EPS) * g_ref[...]
    o_ref[...] = (_silu(rg_ref[...].astype(F32)) * on).astype(o_ref.dtype)


def _retention_tables():
    C = RET_CHUNK
    h = np.arange(RET_HEADS, dtype=np.float64)
    log_g = np.log(1.0 - np.exp2(-5.0 - h))
    idx = np.arange(C, dtype=np.float64)
    diff = idx[:, None] - idx[None, :]
    decay = np.where(diff[None] >= 0, np.exp(np.maximum(diff, 0.0)[None] * log_g[:, None, None]), 0.0)
    zeta = np.exp((C - 1 - idx)[None, :] * log_g[:, None])
    xi = np.exp((idx + 1.0)[None, :] * log_g[:, None])
    zeta_rep = np.broadcast_to(zeta[:, :, None], (RET_HEADS, C, RET_QK))
    xi_rep = np.broadcast_to(xi[:, :, None], (RET_HEADS, C, RET_V))
    return (jnp.asarray(decay, F32), jnp.asarray(xi_rep, F32), jnp.asarray(zeta_rep, F32))


def _retention(z, cr, sr, g_ret, B, S):
    T = z.shape[0]
    C = RET_CHUNK
    N = S // C
    H = RET_HEADS
    dec, xi, zeta = _retention_tables()
    row = lambda b, h, n: b * N + n
    return pl.pallas_call(
        _ret_kernel,
        grid=(B, H, N),
        in_specs=[pl.BlockSpec((C, RET_QK), lambda b, h, n: (row(b, h, n), Z_RQ // RET_QK + h)),
                  pl.BlockSpec((C, RET_QK), lambda b, h, n: (row(b, h, n), Z_RK // RET_QK + h)),
                  pl.BlockSpec((C, RET_V), lambda b, h, n: (row(b, h, n), Z_RV // RET_V + h)),
                  pl.BlockSpec((C, RET_V), lambda b, h, n: (row(b, h, n), Z_RG // RET_V + h)),
                  pl.BlockSpec((C, LANES), lambda b, h, n: (row(b, h, n), 0)),
                  pl.BlockSpec((C, LANES), lambda b, h, n: (row(b, h, n), 0)),
                  pl.BlockSpec((None, C, C), lambda b, h, n: (h, 0, 0)),
                  pl.BlockSpec((None, C, RET_V), lambda b, h, n: (h, 0, 0)),
                  pl.BlockSpec((None, C, RET_QK), lambda b, h, n: (h, 0, 0)),
                  pl.BlockSpec((1, RET_V), lambda b, h, n: (0, h))],
        out_specs=pl.BlockSpec((C, RET_V), lambda b, h, n: (row(b, h, n), h)),
        out_shape=jax.ShapeDtypeStruct((T, H * RET_V), BF16),
        scratch_shapes=[pltpu.VMEM((RET_QK, RET_V), F32)],
        compiler_params=_cparams(("arbitrary", "arbitrary", "arbitrary")),
        name="retention",
    )(z, z, z, z, cr, sr, dec, xi, zeta, g_ret.reshape(1, -1))


def _merge_kernel(x_ref, oa_ref, ob_ref, ga_ref, gb_ref, gt_ref, wa_ref, wb_ref, wo_ref, x1_ref):
    a = jnp.dot(oa_ref[...], wa_ref[...], preferred_element_type=F32)
    b = jnp.dot(ob_ref[...], wb_ref[...], preferred_element_type=F32)
    merged = (jax.nn.sigmoid(ga_ref[...].astype(F32)) * a
              + jax.nn.sigmoid(gb_ref[...].astype(F32)) * b)
    y = jnp.dot(merged.astype(BF16), wo_ref[...], preferred_element_type=F32)
    x1_ref[...] = x_ref[...] + gt_ref[...] * y


def _merge(x2, o_mla, o_ret, z, mod3, wa, wb, wo, S):
    T, D = x2.shape
    tm = 512
    per_b = S // tm
    full = lambda shape: pl.BlockSpec(shape, lambda i: (0, 0))
    return pl.pallas_call(
        _merge_kernel,
        grid=(T // tm,),
        in_specs=[pl.BlockSpec((tm, D), lambda i: (i, 0)),
                  pl.BlockSpec((tm, o_mla.shape[1]), lambda i: (i, 0)),
                  pl.BlockSpec((tm, D), lambda i: (i, 0)),
                  pl.BlockSpec((tm, D), lambda i: (i, Z_GA // D)),
                  pl.BlockSpec((tm, D), lambda i: (i, Z_GB // D)),
                  pl.BlockSpec((None, 1, D), lambda i: ((i // per_b) * 6 + 2, 0, 0)),
                  full(wa.shape), full(wb.shape), full(wo.shape)],
        out_specs=pl.BlockSpec((tm, D), lambda i: (i, 0)),
        out_shape=jax.ShapeDtypeStruct((T, D), F32),
        compiler_params=_cparams(("arbitrary",)),
        name="merge_out",
    )(x2, o_mla, o_ret, z, z, mod3, wa, wb, wo)


def _router_kernel(x1_ref, sc_ref, sh_ref, g_ref, wr_ref, br_ref,
                   h2_ref, wts_ref, pp_ref, off_ref, pc_ref):
    TB = x1_ref.shape[0]
    D = x1_ref.shape[1]
    E, G = N_EXPERTS, N_GROUPS
    per = E // G
    h2 = _rms(x1_ref[...]) * g_ref[...] * (1.0 + sc_ref[...]) + sh_ref[...]
    h2_ref[...] = h2
    logits = lax.dot_general(wr_ref[...], h2, (((1,), (1,)), ((), ())),
                             precision=lax.Precision.HIGHEST,
                             preferred_element_type=F32)
    s = jax.nn.sigmoid(logits)
    biased = s + br_ref[...]
    sub = lax.broadcasted_iota(I32, (per, TB), 0)
    neg = -jnp.inf

    def first_argmax(vals, m, idx, sentinel):
        return jnp.min(jnp.where(vals == m, idx, sentinel), axis=0, keepdims=True)

    bg = [biased[g * per:(g + 1) * per, :] for g in range(G)]
    sg = [s[g * per:(g + 1) * per, :] for g in range(G)]
    gscore = []
    for g in range(G):
        m1 = jnp.max(bg[g], axis=0, keepdims=True)
        i1 = first_argmax(bg[g], m1, sub, per)
        m2 = jnp.max(jnp.where(sub == i1, neg, bg[g]), axis=0, keepdims=True)
        gscore.append(m1 + m2)
    gs = jnp.concatenate(gscore, axis=0)
    gidx = lax.broadcasted_iota(I32, (G, TB), 0)
    gsel = jnp.zeros((G, TB), F32)
    for _ in range(TOPK_GROUPS):
        m = jnp.max(gs, axis=0, keepdims=True)
        i = first_argmax(gs, m, gidx, G)
        hit = gidx == i
        gsel = jnp.where(hit, 1.0, gsel)
        gs = jnp.where(hit, neg, gs)
    cand = [jnp.where(gsel[g:g + 1, :] > 0.0, bg[g], neg) for g in range(G)]
    eidx = [sub + g * per for g in range(G)]
    sel = [jnp.zeros((per, TB), F32) for _ in range(G)]
    top_i, top_w = [], []
    for _ in range(TOP_K):
        m = functools.reduce(jnp.maximum, [jnp.max(c, axis=0, keepdims=True) for c in cand])
        i = functools.reduce(jnp.minimum,
                             [first_argmax(cand[g], m, eidx[g], E) for g in range(G)])
        w = jnp.zeros((1, TB), F32)
        for g in range(G):
            hit = eidx[g] == i
            w = w + jnp.sum(jnp.where(hit, sg[g], 0.0), axis=0, keepdims=True)
            sel[g] = jnp.where(hit, 1.0, sel[g])
            cand[g] = jnp.where(hit, neg, cand[g])
        top_i.append(i)
        top_w.append(w)
    wsum = functools.reduce(lambda a, b: a + b, top_w)
    wts_ref[...] = jnp.concatenate([w / wsum * ROUTED_SCALE for w in top_w], axis=0)

    mask = jnp.concatenate(sel, axis=0)
    t_row = lax.broadcasted_iota(I32, (TB, TB), 0)
    t_col = lax.broadcasted_iota(I32, (TB, TB), 1)
    before = jnp.where(t_row < t_col, 1.0, 0.0).astype(BF16)
    rank = jnp.dot(mask.astype(BF16), before, preferred_element_type=F32)
    cnt = jnp.sum(mask, axis=1, keepdims=True)
    pc = jnp.floor((cnt + (SUBLANES - 1)) * (1.0 / SUBLANES)) * SUBLANES
    pc_rep = jnp.broadcast_to(pc, (E, LANES))
    e_row = lax.broadcasted_iota(I32, (E, E), 0)
    e_col = lax.broadcasted_iota(I32, (E, E), 1)
    lower = jnp.where(e_col < e_row, 1.0, 0.0).astype(BF16)
    off_rep = jnp.dot(lower, pc_rep.astype(BF16), preferred_element_type=F32)
    off_ref[...] = off_rep.astype(I32)
    pc_ref[...] = pc_rep.astype(I32)
    posfull = off_rep[:, :1] + rank
    pos = []
    for kk in range(TOP_K):
        p = jnp.zeros((1, TB), F32)
        for g in range(G):
            p = p + jnp.sum(jnp.where(eidx[g] == top_i[kk], posfull[g * per:(g + 1) * per, :], 0.0),
                            axis=0, keepdims=True)
        pos.append(p.astype(I32))
    pp_ref[...] = jnp.concatenate(
        [pos[2 * m] | (pos[2 * m + 1] << 16) for m in range(TOP_K // 2)], axis=0)


def _router(x1, mod3, g_norm2, w_router, b_router, S):
    T, D = x1.shape
    TB = min(MOE_TB, T)
    nb = T // TB
    per_b = S // TB
    E = N_EXPERTS
    return pl.pallas_call(
        _router_kernel,
        grid=(nb,),
        in_specs=[pl.BlockSpec((TB, D), lambda i: (i, 0)),
                  pl.BlockSpec((None, 1, D), lambda i: ((i // per_b) * 6 + 4, 0, 0)),
                  pl.BlockSpec((None, 1, D), lambda i: ((i // per_b) * 6 + 3, 0, 0)),
                  pl.BlockSpec((1, D), lambda i: (0, 0)),
                  pl.BlockSpec((E, D), lambda i: (0, 0)),
                  pl.BlockSpec((E, 1), lambda i: (0, 0))],
        out_specs=[pl.BlockSpec((TB, D), lambda i: (i, 0)),
                   pl.BlockSpec((TOP_K, TB), lambda i: (0, i)),
                   pl.BlockSpec((TOP_K // 2, TB), lambda i: (0, i)),
                   pl.BlockSpec((E, LANES), lambda i: (i, 0)),
                   pl.BlockSpec((E, LANES), lambda i: (i, 0))],
        out_shape=[jax.ShapeDtypeStruct((T, D), F32),
                   jax.ShapeDtypeStruct((TOP_K, T), F32),
                   jax.ShapeDtypeStruct((TOP_K // 2, T), I32),
                   jax.ShapeDtypeStruct((nb * E, LANES), I32),
                   jax.ShapeDtypeStruct((nb * E, LANES), I32)],
        compiler_params=_cparams(("arbitrary",)),
        name="moe_router",
    )(x1, mod3, mod3, g_norm2.reshape(1, D), w_router.T, b_router.reshape(E, 1))


def _moe_kernel(pp_sm, w_sm, off_sm, pc_sm, h2_ref, wg_ref, wu_ref, wd_ref, *rest, T):
    out_ref, xs_scr = rest[-2:]
    j = pl.program_id(0)
    e = pl.program_id(1)
    E = N_EXPERTS
    TB, D = h2_ref.shape
    CH = MOE_CH
    npair = TOP_K // 2

    def positions(t):
        out = []
        for m in range(npair):
            word = pp_sm[m * T + j * TB + t]
            out.append(word & 0xFFFF)
            out.append(word >> 16)
        return out

    @pl.when(e == 0)
    def _dispatch():
        def zero_pad(ee, carry):
            st = off_sm[j * E + ee]
            n = pc_sm[j * E + ee]

            @pl.when(n > 0)
            def _():
                r = pl.multiple_of(st + n - SUBLANES, SUBLANES)
                xs_scr[pl.ds(r, SUBLANES), :] = jnp.zeros((SUBLANES, D), F32)
            return carry

        lax.fori_loop(0, E, zero_pad, 0)
        total = pl.multiple_of(off_sm[j * E + E - 1] + pc_sm[j * E + E - 1], SUBLANES)
        xs_scr[pl.ds(total, CH), :] = jnp.zeros((CH, D), F32)

        def scatter(t, carry):
            row = h2_ref[pl.ds(t, 1), :]
            for p in positions(t):
                xs_scr[pl.ds(p, 1), :] = row
            return carry

        lax.fori_loop(0, TB, scatter, 0)

    st = off_sm[j * E + e]
    n = pc_sm[j * E + e]
    rows = lax.broadcasted_iota(I32, (CH, D), 0)

    def chunk(c, carry):
        r0 = pl.multiple_of(st + c * CH, SUBLANES)
        xin = xs_scr[pl.ds(r0, CH), :]
        xb = xin.astype(BF16)
        g = jnp.dot(xb, wg_ref[...], preferred_element_type=F32)
        u = jnp.dot(xb, wu_ref[...], preferred_element_type=F32)
        y = jnp.dot((_silu(g) * u).astype(BF16), wd_ref[...], preferred_element_type=F32)
        xs_scr[pl.ds(r0, CH), :] = jnp.where(rows + c * CH < n, y, xin)
        return carry

    lax.fori_loop(0, (n + CH - 1) // CH, chunk, 0)

    @pl.when(e == E - 1)
    def _combine():
        def gather(t, carry):
            acc = jnp.zeros((1, D), F32)
            for k, p in enumerate(positions(t)):
                acc = acc + w_sm[k * T + j * TB + t] * xs_scr[pl.ds(p, 1), :]
            out_ref[pl.ds(t, 1), :] = acc
            return carry

        lax.fori_loop(0, TB, gather, 0)


def _moe(h2, wts, pp, off, pc, wg, wu, wd, n_split):
    T, D = h2.shape
    E = N_EXPERTS
    Ts = T // n_split
    TB = min(MOE_TB, Ts)
    nb = Ts // TB
    rows = TOP_K * TB + E * SUBLANES + MOE_CH
    routed = None
    for s in range(n_split):
        tok = slice(s * Ts, (s + 1) * Ts)
        blk = slice(s * nb * E, (s + 1) * nb * E)
        in_specs = [pl.BlockSpec((TB, D), lambda j, e, *_, s=s: (s * nb + j, 0),
                                 pipeline_mode=pl.Buffered(1)),
                    pl.BlockSpec((None, D, D_EXPERT), lambda j, e, *_: (e, 0, 0)),
                    pl.BlockSpec((None, D, D_EXPERT), lambda j, e, *_: (e, 0, 0)),
                    pl.BlockSpec((None, D_EXPERT, D), lambda j, e, *_: (e, 0, 0))]
        args = [pp[:, tok].reshape(-1), wts[:, tok].reshape(-1), off[blk], pc[blk], h2, wg, wu, wd]
        aliases = {}
        if routed is not None:
            in_specs.append(pl.BlockSpec(memory_space=pl.ANY))
            args.append(routed)
            aliases = {len(args) - 1: 0}
        grid_spec = pltpu.PrefetchScalarGridSpec(
            num_scalar_prefetch=4,
            grid=(nb, E),
            in_specs=in_specs,
            out_specs=pl.BlockSpec((TB, D), lambda j, e, *_, s=s: (s * nb + j, 0),
                                   pipeline_mode=pl.Buffered(1)),
            scratch_shapes=[pltpu.VMEM((rows, D), F32)],
        )
        routed = pl.pallas_call(
            functools.partial(_moe_kernel, T=Ts),
            grid_spec=grid_spec,
            out_shape=jax.ShapeDtypeStruct((T, D), F32),
            input_output_aliases=aliases,
            compiler_params=_cparams(("arbitrary", "arbitrary")),
            name="moe_experts",
        )(*args)
    return routed


def _final_kernel(x1_ref, routed_ref, h2_ref, gt_ref, wsg_ref, wsu_ref, wsd_ref, gf_ref, o_ref):
    hb = h2_ref[...].astype(BF16)
    g = jnp.dot(hb, wsg_ref[...], preferred_element_type=F32)
    u = jnp.dot(hb, wsu_ref[...], preferred_element_type=F32)
    shared = jnp.dot((_silu(g) * u).astype(BF16), wsd_ref[...], preferred_element_type=F32)
    xo = x1_ref[...] + gt_ref[...] * (routed_ref[...] + shared)
    o_ref[...] = _rms(xo) * gf_ref[...]


def _final(x1, routed, h2, mod3, wsg, wsu, wsd, g_final, S):
    T, D = x1.shape
    tm = 512
    per_b = S // tm
    full = lambda shape: pl.BlockSpec(shape, lambda i: (0, 0))
    return pl.pallas_call(
        _final_kernel,
        grid=(T // tm,),
        in_specs=[pl.BlockSpec((tm, D), lambda i: (i, 0)),
                  pl.BlockSpec((tm, D), lambda i: (i, 0)),
                  pl.BlockSpec((tm, D), lambda i: (i, 0)),
                  pl.BlockSpec((None, 1, D), lambda i: ((i // per_b) * 6 + 5, 0, 0)),
                  full(wsg.shape), full(wsu.shape), full(wsd.shape), full((1, D))],
        out_specs=pl.BlockSpec((tm, D), lambda i: (i, 0)),
        out_shape=jax.ShapeDtypeStruct((T, D), F32),
        compiler_params=_cparams(("arbitrary",)),
        name="final_out",
    )(x1, routed, h2, mod3, wsg, wsu, wsd, g_final.reshape(1, D))


def _pack_w_in(w_in):
    D = w_in.shape[0]
    splits = [MLA_Q_RANK, MLA_KV_RANK, MLA_ROPE, RET_HEADS * RET_QK, RET_HEADS * RET_QK,
              RET_HEADS * RET_V, RET_HEADS * RET_V, D, D]
    idx = [int(v) for v in np.cumsum(splits)[:-1]]
    wcq, wckv, wkr, wrq, wrk, wrv, wrg, wga, wgb = jnp.split(w_in, idx, axis=1)
    hm = MLA_ROPE // 2
    zl = jnp.zeros((D, MLA_NOPE), w_in.dtype)
    zr = jnp.zeros((D, LANES - MLA_NOPE - MLA_ROPE), w_in.dtype)
    kr1 = jnp.concatenate([zl, wkr, zr], axis=1)
    kr2 = jnp.concatenate([zl, wkr[:, hm:], wkr[:, :hm], zr], axis=1)
    pad = jnp.zeros((D, Z_COLS - (Z_KR2 + LANES)), w_in.dtype)
    w = jnp.concatenate([wrv, wrg, wga, wgb, wrq, wrk, wcq, wckv, kr1, kr2, pad], axis=1)
    return w.astype(BF16)


def _pack_mla_weights(w_uq, w_ukv):
    H = MLA_HEADS
    hm = MLA_ROPE // 2
    wq = w_uq.reshape(MLA_Q_RANK, H, MLA_NOPE + MLA_ROPE)
    nope, pe = wq[..., :MLA_NOPE], wq[..., MLA_NOPE:]
    zpad = jnp.zeros((MLA_Q_RANK, H, LANES - MLA_NOPE - MLA_ROPE), w_uq.dtype)
    wq1 = jnp.concatenate([nope, pe, zpad], axis=-1).reshape(MLA_Q_RANK, H * LANES)
    wq2 = jnp.concatenate([jnp.zeros_like(nope), pe[..., hm:], pe[..., :hm], zpad],
                          axis=-1).reshape(MLA_Q_RANK, H * LANES)
    wkv = w_ukv.reshape(MLA_KV_RANK, H, MLA_NOPE + MLA_V)
    kn, vv = wkv[..., :MLA_NOPE], wkv[..., MLA_NOPE:]
    wk = jnp.concatenate([kn, jnp.zeros((MLA_KV_RANK, H, LANES - MLA_NOPE), w_ukv.dtype)],
                         axis=-1).reshape(MLA_KV_RANK, H * LANES)
    wv = vv.reshape(MLA_KV_RANK, H * MLA_V)
    return wq1.astype(BF16), wq2.astype(BF16), wk.astype(BF16), wv.astype(BF16)


def kernel(x, c, positions, w_ada, b_ada, g_norm1, w_in, g_cq, w_uq, g_ckv, w_ukv, g_ret,
           w_o_mla, w_o_ret, w_out, g_norm2, w_router, b_router, w_exp_gate, w_exp_up,
           w_exp_down, w_sh_gate, w_sh_up, w_sh_down, g_final):
    B, S, D = x.shape
    T = B * S
    x2 = x.reshape(T, D)

    mod = _ada(c, w_ada, b_ada)
    mod3 = mod.reshape(B * 6, 1, D)
    cr, sr, cm, sm = _rope_tables(positions)

    z = _inproj(x2, mod3, g_norm1, _pack_w_in(w_in), S)
    wq1, wq2, wk, wv = _pack_mla_weights(w_uq, w_ukv)
    q, k, v = _mla_up(z, cm, sm, g_cq, g_ckv, wq1, wq2, wk, wv)
    o_mla = _attention(q, k, v, B, S)
    o_ret = _retention(z, cr, sr, g_ret, B, S)
    x1 = _merge(x2, o_mla, o_ret, z, mod3, w_o_mla.astype(BF16), w_o_ret.astype(BF16),
                w_out.astype(BF16), S)

    h2, wts, pp, off_rep, pc_rep = _router(x1, mod3, g_norm2, w_router, b_router, S)
    routed = _moe(h2, wts, pp, off_rep[:, 0], pc_rep[:, 0],
                  w_exp_gate.astype(BF16), w_exp_up.astype(BF16), w_exp_down.astype(BF16), n_split=B)
    out = _final(x1, routed, h2, mod3, w_sh_gate.astype(BF16), w_sh_up.astype(BF16),
                 w_sh_down.astype(BF16), g_final, S)
    return out.reshape(B, S, D)
```

```python
import functools
import math

import numpy as np
import jax
import jax.numpy as jnp
from jax import lax
from jax.experimental import pallas as pl
from jax.experimental.pallas import tpu as pltpu

F32 = jnp.float32
BF16 = jnp.bfloat16
I32 = jnp.int32

MLA_HEADS = 8
MLA_Q_RANK = 384
MLA_KV_RANK = 256
MLA_NOPE = 64
MLA_ROPE = 32
MLA_V = 64
RET_HEADS = 4
RET_QK = 128
RET_V = 256
RET_CHUNK = 128
ROPE_THETA = 10000.0
N_EXPERTS = 64
TOP_K = 8
N_GROUPS = 8
TOPK_GROUPS = 4
D_EXPERT = 256
ROUTED_SCALE = 2.5
RMS_EPS = 1e-6
GN_EPS = 1e-5

LANES = 128
SUBLANES = 8
VMEM_LIMIT = 56 * 1024 * 1024

Z_RV, Z_RG, Z_GA, Z_GB = 0, 1024, 2048, 3072
Z_RQ, Z_RK = 4096, 4608
Z_CQKV = 5120
Z_KR1, Z_KR2 = 5760, 5888
Z_COLS = 6144

LOG2E = 1.4426950408889634

MOE_TB = 1024
MOE_CH = 128
MOE_EPS = 2


def _cparams(sem):
    return pltpu.CompilerParams(dimension_semantics=sem, vmem_limit_bytes=VMEM_LIMIT)


def _rms(x):
    return x * lax.rsqrt(jnp.mean(x * x, axis=-1, keepdims=True) + RMS_EPS)


def _silu(x):
    return x * jax.nn.sigmoid(x)


def _load_row_tiles(ref, nrows):
    nchunk = ref.shape[0] // nrows
    return jnp.concatenate([ref[pl.ds(c, nrows, stride=nchunk), :] for c in range(nchunk)], axis=1)


def _store_row_tiles(ref, val):
    nrows, d = val.shape
    nchunk = d // LANES
    for c in range(nchunk):
        ref[pl.ds(c, nrows, stride=nchunk), :] = val[:, c * LANES:(c + 1) * LANES]


def _ada_kernel(c_ref, w_ref, b_ref, o_ref):
    c = c_ref[...]
    o_ref[...] = jnp.dot(_silu(c).astype(BF16), w_ref[...].astype(BF16),
                         preferred_element_type=F32) + b_ref[...]


def _ada(c, w_ada, b_ada):
    B, D = c.shape
    n_out = w_ada.shape[1]
    cp = jnp.zeros((SUBLANES, D), F32).at[:B].set(c)
    tn = D
    out = pl.pallas_call(
        _ada_kernel,
        grid=(n_out // tn,),
        in_specs=[pl.BlockSpec((SUBLANES, D), lambda j: (0, 0)),
                  pl.BlockSpec((D, tn), lambda j: (0, j)),
                  pl.BlockSpec((1, tn), lambda j: (0, j))],
        out_specs=pl.BlockSpec((SUBLANES, tn), lambda j: (0, j)),
        out_shape=jax.ShapeDtypeStruct((SUBLANES, n_out), F32),
        compiler_params=_cparams(("arbitrary",)),
        name="ada_mod",
    )(cp, w_ada, b_ada.reshape(1, n_out))
    return out[:B]


def _rope_kernel(pos_ref, inv_ref, cr_ref, sr_ref, cm_ref, sm_ref):
    ang = pos_ref[...].astype(F32) * inv_ref[...]
    c = jnp.cos(ang)
    s = jnp.sin(ang)
    lane = lax.broadcasted_iota(I32, c.shape, 1)
    half = RET_QK // 2
    cr_ref[...] = jnp.where(lane < half, c, pltpu.roll(c, half, 1))
    sr_ref[...] = jnp.where(lane < half, -s, pltpu.roll(s, half, 1))
    hm = MLA_ROPE // 2
    cm_ref[...] = jnp.where(lane < MLA_NOPE, 1.0,
                            jnp.where(lane < MLA_NOPE + hm, c,
                                      jnp.where(lane < MLA_NOPE + 2 * hm, pltpu.roll(c, hm, 1), 0.0)))
    sm_ref[...] = jnp.where(lane < MLA_NOPE, 0.0,
                            jnp.where(lane < MLA_NOPE + hm, -s,
                                      jnp.where(lane < MLA_NOPE + 2 * hm, pltpu.roll(s, hm, 1), 0.0)))


def _rope_tables(positions):
    T = positions.size
    tm = min(T, 1024)
    inv_r = 1.0 / (ROPE_THETA ** (jnp.arange(0, RET_QK, 2, dtype=F32) / RET_QK))
    inv_m = 1.0 / (ROPE_THETA ** (jnp.arange(0, MLA_ROPE, 2, dtype=F32) / MLA_ROPE))
    inv = jnp.zeros((1, LANES), F32).at[0, :RET_QK // 2].set(inv_r)
    inv = inv.at[0, MLA_NOPE:MLA_NOPE + MLA_ROPE // 2].set(inv_m)
    tab = jax.ShapeDtypeStruct((T, LANES), F32)
    spec = pl.BlockSpec((tm, LANES), lambda i: (i, 0))
    return pl.pallas_call(
        _rope_kernel,
        grid=(T // tm,),
        in_specs=[pl.BlockSpec((tm, 1), lambda i: (i, 0)),
                  pl.BlockSpec((1, LANES), lambda i: (0, 0))],
        out_specs=[spec, spec, spec, spec],
        out_shape=[tab, tab, tab, tab],
        compiler_params=_cparams(("arbitrary",)),
        name="rope_tables",
    )(positions.reshape(T, 1), inv)


def _inproj_kernel(x_ref, sc_ref, sh_ref, g_ref, w_ref, z_ref, h_scr):
    @pl.when(pl.program_id(1) == 0)
    def _():
        h = _rms(x_ref[...]) * g_ref[...] * (1.0 + sc_ref[...]) + sh_ref[...]
        h_scr[...] = h.astype(BF16)

    z_ref[...] = jnp.dot(h_scr[...], w_ref[...], preferred_element_type=F32).astype(z_ref.dtype)


def _inproj(x2, mod3, g_norm1, w_pack, S):
    T, D = x2.shape
    N = w_pack.shape[1]
    tm, tn = 512, 2048
    per_b = S // tm
    return pl.pallas_call(
        _inproj_kernel,
        grid=(T // tm, N // tn),
        in_specs=[pl.BlockSpec((tm, D), lambda i, j: (i, 0)),
                  pl.BlockSpec((None, 1, D), lambda i, j: ((i // per_b) * 6 + 1, 0, 0)),
                  pl.BlockSpec((None, 1, D), lambda i, j: ((i // per_b) * 6 + 0, 0, 0)),
                  pl.BlockSpec((1, D), lambda i, j: (0, 0)),
                  pl.BlockSpec((D, tn), lambda i, j: (0, j))],
        out_specs=pl.BlockSpec((tm, tn), lambda i, j: (i, j)),
        out_shape=jax.ShapeDtypeStruct((T, N), BF16),
        scratch_shapes=[pltpu.VMEM((tm, D), BF16)],
        compiler_params=_cparams(("arbitrary", "arbitrary")),
        name="in_proj",
    )(x2, mod3, mod3, g_norm1.reshape(1, D), w_pack)


def _mla_up_kernel(zc_ref, kr1_ref, kr2_ref, cm_ref, sm_ref, gq_ref, gkv_ref,
                   wq1_ref, wq2_ref, wk_ref, wv_ref, q_ref, k_ref, v_ref):
    zc = zc_ref[...].astype(F32)
    cqn = (_rms(zc[:, :MLA_Q_RANK]) * gq_ref[...]).astype(BF16)
    ckvn = (_rms(zc[:, MLA_Q_RANK:]) * gkv_ref[...]).astype(BF16)
    cm = cm_ref[...]
    sm = sm_ref[...]
    q1 = jnp.dot(cqn, wq1_ref[...], preferred_element_type=F32)
    q2 = jnp.dot(cqn, wq2_ref[...], preferred_element_type=F32)
    kn = jnp.dot(ckvn, wk_ref[...], preferred_element_type=F32)
    kpe = kr1_ref[...].astype(F32) * cm + kr2_ref[...].astype(F32) * sm
    qscale = (MLA_NOPE + MLA_ROPE) ** -0.5 * LOG2E
    for h in range(MLA_HEADS):
        sl = slice(h * LANES, (h + 1) * LANES)
        q_ref[:, sl] = ((q1[:, sl] * cm + q2[:, sl] * sm) * qscale).astype(BF16)
        k_ref[:, sl] = (kn[:, sl] + kpe).astype(BF16)
    v_ref[...] = lax.dot_general(wv_ref[...], ckvn, (((1,), (1,)), ((), ())),
                                 preferred_element_type=F32).astype(BF16)


def _mla_up(z, cm, sm, g_cq, g_ckv, wq1, wq2, wk, wv):
    T = z.shape[0]
    tm = 512
    HW = MLA_HEADS * LANES
    wc = MLA_Q_RANK + MLA_KV_RANK
    full = lambda shape: pl.BlockSpec(shape, lambda i: (0, 0))
    return pl.pallas_call(
        _mla_up_kernel,
        grid=(T // tm,),
        in_specs=[pl.BlockSpec((tm, wc), lambda i: (i, Z_CQKV // wc)),
                  pl.BlockSpec((tm, LANES), lambda i: (i, Z_KR1 // LANES)),
                  pl.BlockSpec((tm, LANES), lambda i: (i, Z_KR2 // LANES)),
                  pl.BlockSpec((tm, LANES), lambda i: (i, 0)),
                  pl.BlockSpec((tm, LANES), lambda i: (i, 0)),
                  full((1, MLA_Q_RANK)), full((1, MLA_KV_RANK)),
                  full(wq1.shape), full(wq2.shape), full(wk.shape), full(wv.shape)],
        out_specs=[pl.BlockSpec((tm, HW), lambda i: (i, 0)),
                   pl.BlockSpec((tm, HW), lambda i: (i, 0)),
                   pl.BlockSpec((MLA_HEADS * MLA_V, tm), lambda i: (0, i))],
        out_shape=[jax.ShapeDtypeStruct((T, HW), BF16),
                   jax.ShapeDtypeStruct((T, HW), BF16),
                   jax.ShapeDtypeStruct((MLA_HEADS * MLA_V, T), BF16)],
        compiler_params=_cparams(("arbitrary",)),
        name="mla_up",
    )(z, z, z, cm, sm, g_cq.reshape(1, -1), g_ckv.reshape(1, -1), wq1, wq2, wk, wv)


def _attn_kernel(q_ref, k_ref, vt_ref, o_ref, acc_scr, *, bq, bk):
    qi = pl.program_id(2)
    krow = lax.broadcasted_iota(I32, (bk, bq), 0)
    qcol = lax.broadcasted_iota(I32, (bk, bq), 1)
    nsub = bq // bk
    qs = [q_ref[:, h * LANES:(h + 1) * LANES] for h in range(2)]
    acc_scr[...] = jnp.zeros(acc_scr.shape, F32)

    def block(kb, carry, diag_off):
        k0 = pl.multiple_of(kb * bk, bk)
        new = []
        for h in range(2):
            m_prev, l_prev = carry[2 * h], carry[2 * h + 1]
            kblk = k_ref[pl.ds(k0, bk), h * LANES:(h + 1) * LANES]
            st = lax.dot_general(kblk, qs[h], (((1,), (1,)), ((), ())),
                                 preferred_element_type=F32)
            if diag_off is not None:
                st = jnp.where(qcol >= krow + diag_off, st, -jnp.inf)
            m_new = jnp.maximum(m_prev, jnp.max(st, axis=0, keepdims=True))
            alpha = jnp.exp2(m_prev - m_new)
            p = jnp.exp2(st - m_new)
            l_new = alpha * l_prev + jnp.sum(p, axis=0, keepdims=True)
            rows = slice(h * MLA_V, (h + 1) * MLA_V)
            vt = vt_ref[rows, pl.ds(k0, bk)]
            acc_scr[rows, :] = alpha * acc_scr[rows, :] + jnp.dot(
                vt, p.astype(BF16), preferred_element_type=F32)
            new += [m_new, l_new]
        return tuple(new)

    init = (jnp.full((1, bq), -jnp.inf, F32), jnp.zeros((1, bq), F32)) * 2
    carry = lax.fori_loop(0, qi * nsub, lambda kb, c: block(kb, c, None), init)
    for d in range(nsub):
        carry = block(qi * nsub + d, carry, d * bk)
    ot = jnp.concatenate([acc_scr[:MLA_V, :] / carry[1], acc_scr[MLA_V:, :] / carry[3]], axis=0)
    o_ref[...] = ot.T.astype(o_ref.dtype)


def _attention(q, k, vt, B, S):
    T = q.shape[0]
    bq = min(512, S)
    bk = min(512, S)
    nq = S // bq
    hp = MLA_HEADS // 2
    kern = functools.partial(_attn_kernel, bq=bq, bk=bk)
    return pl.pallas_call(
        kern,
        grid=(B, hp, nq),
        in_specs=[pl.BlockSpec((bq, 2 * LANES), lambda b, p, i: (b * nq + i, p)),
                  pl.BlockSpec((S, 2 * LANES), lambda b, p, i: (b, p)),
                  pl.BlockSpec((2 * MLA_V, S), lambda b, p, i: (p, b))],
        out_specs=pl.BlockSpec((bq, 2 * MLA_V), lambda b, p, i: (b * nq + i, p)),
        out_shape=jax.ShapeDtypeStruct((T, MLA_HEADS * MLA_V), BF16),
        scratch_shapes=[pltpu.VMEM((2 * MLA_V, bq), F32)],
        compiler_params=_cparams(("arbitrary", "arbitrary", "arbitrary")),
        name="mla_attention",
    )(q, k, vt)


def _ret_kernel(rq_ref, rk_ref, rv_ref, rg_ref, cr_ref, sr_ref, dec_ref, xi_ref, zeta_ref,
                g_ref, o_ref, state_scr):
    @pl.when(pl.program_id(2) == 0)
    def _():
        state_scr[...] = jnp.zeros(state_scr.shape, F32)

    cr = cr_ref[...]
    sr = sr_ref[...]
    half = RET_QK // 2
    rq = rq_ref[...].astype(F32)
    rk = rk_ref[...].astype(F32)
    q = rq * cr + pltpu.roll(rq, half, 1) * sr
    k = (rk * cr + pltpu.roll(rk, half, 1) * sr) * (RET_QK ** -0.5)
    v = rv_ref[...]
    xi = xi_ref[...]
    state = state_scr[...]
    qb = q.astype(BF16)
    s = lax.dot_general(qb, k.astype(BF16), (((1,), (1,)), ((), ())),
                        preferred_element_type=F32) * dec_ref[...]
    inner = jnp.dot(s.astype(BF16), v, preferred_element_type=F32)
    cross = jnp.dot(qb, state.astype(BF16), preferred_element_type=F32) * xi
    kz = (k * zeta_ref[...]).astype(BF16)
    kv = lax.dot_general(kz, v, (((0,), (0,)), ((), ())), preferred_element_type=F32)
    state_scr[...] = xi[RET_CHUNK - 1:RET_CHUNK, :] * state + kv
    o = inner + cross
    mu = jnp.mean(o, axis=-1, keepdims=True)
    d = o - mu
    var = jnp.mean(d * d, axis=-1, keepdims=True)
    on = d * lax.rsqrt(var + GN_EPS) * g_ref[...]
    o_ref[...] = (_silu(rg_ref[...].astype(F32)) * on).astype(o_ref.dtype)


def _retention_tables():
    C = RET_CHUNK
    h = np.arange(RET_HEADS, dtype=np.float64)
    log_g = np.log(1.0 - np.exp2(-5.0 - h))
    idx = np.arange(C, dtype=np.float64)
    diff = idx[:, None] - idx[None, :]
    decay = np.where(diff[None] >= 0, np.exp(np.maximum(diff, 0.0)[None] * log_g[:, None, None]), 0.0)
    zeta = np.exp((C - 1 - idx)[None, :] * log_g[:, None])
    xi = np.exp((idx + 1.0)[None, :] * log_g[:, None])
    zeta_rep = np.broadcast_to(zeta[:, :, None], (RET_HEADS, C, RET_QK))
    xi_rep = np.broadcast_to(xi[:, :, None], (RET_HEADS, C, RET_V))
    return (jnp.asarray(decay, F32), jnp.asarray(xi_rep, F32), jnp.asarray(zeta_rep, F32))


def _retention(z, cr, sr, g_ret, B, S):
    T = z.shape[0]
    C = RET_CHUNK
    N = S // C
    H = RET_HEADS
    dec, xi, zeta = _retention_tables()
    row = lambda b, h, n: b * N + n
    return pl.pallas_call(
        _ret_kernel,
        grid=(B, H, N),
        in_specs=[pl.BlockSpec((C, RET_QK), lambda b, h, n: (row(b, h, n), Z_RQ // RET_QK + h)),
                  pl.BlockSpec((C, RET_QK), lambda b, h, n: (row(b, h, n), Z_RK // RET_QK + h)),
                  pl.BlockSpec((C, RET_V), lambda b, h, n: (row(b, h, n), Z_RV // RET_V + h)),
                  pl.BlockSpec((C, RET_V), lambda b, h, n: (row(b, h, n), Z_RG // RET_V + h)),
                  pl.BlockSpec((C, LANES), lambda b, h, n: (row(b, h, n), 0)),
                  pl.BlockSpec((C, LANES), lambda b, h, n: (row(b, h, n), 0)),
                  pl.BlockSpec((None, C, C), lambda b, h, n: (h, 0, 0)),
                  pl.BlockSpec((None, C, RET_V), lambda b, h, n: (h, 0, 0)),
                  pl.BlockSpec((None, C, RET_QK), lambda b, h, n: (h, 0, 0)),
                  pl.BlockSpec((1, RET_V), lambda b, h, n: (0, h))],
        out_specs=pl.BlockSpec((C, RET_V), lambda b, h, n: (row(b, h, n), h)),
        out_shape=jax.ShapeDtypeStruct((T, H * RET_V), BF16),
        scratch_shapes=[pltpu.VMEM((RET_QK, RET_V), F32)],
        compiler_params=_cparams(("arbitrary", "arbitrary", "arbitrary")),
        name="retention",
    )(z, z, z, z, cr, sr, dec, xi, zeta, g_ret.reshape(1, -1))


def _merge_kernel(x_ref, oa_ref, ob_ref, ga_ref, gb_ref, gt_ref, wa_ref, wb_ref, wo_ref, x1_ref):
    a = jnp.dot(oa_ref[...], wa_ref[...], preferred_element_type=F32)
    b = jnp.dot(ob_ref[...], wb_ref[...], preferred_element_type=F32)
    merged = (jax.nn.sigmoid(ga_ref[...].astype(F32)) * a
              + jax.nn.sigmoid(gb_ref[...].astype(F32)) * b)
    y = jnp.dot(merged.astype(BF16), wo_ref[...], preferred_element_type=F32)
    x1_ref[...] = x_ref[...] + gt_ref[...] * y


def _merge(x2, o_mla, o_ret, z, mod3, wa, wb, wo, S):
    T, D = x2.shape
    tm = 512
    per_b = S // tm
    full = lambda shape: pl.BlockSpec(shape, lambda i: (0, 0))
    return pl.pallas_call(
        _merge_kernel,
        grid=(T // tm,),
        in_specs=[pl.BlockSpec((tm, D), lambda i: (i, 0)),
                  pl.BlockSpec((tm, o_mla.shape[1]), lambda i: (i, 0)),
                  pl.BlockSpec((tm, D), lambda i: (i, 0)),
                  pl.BlockSpec((tm, D), lambda i: (i, Z_GA // D)),
                  pl.BlockSpec((tm, D), lambda i: (i, Z_GB // D)),
                  pl.BlockSpec((None, 1, D), lambda i: ((i // per_b) * 6 + 2, 0, 0)),
                  full(wa.shape), full(wb.shape), full(wo.shape)],
        out_specs=pl.BlockSpec((tm, D), lambda i: (i, 0)),
        out_shape=jax.ShapeDtypeStruct((T, D), F32),
        compiler_params=_cparams(("arbitrary",)),
        name="merge_out",
    )(x2, o_mla, o_ret, z, z, mod3, wa, wb, wo)


def _router_kernel(x1_ref, sc_ref, sh_ref, g_ref, wr_ref, br_ref,
                   h2_ref, wts_ref, pp_ref, off_ref, pc_ref):
    TB = x1_ref.shape[0]
    D = x1_ref.shape[1]
    E, G = N_EXPERTS, N_GROUPS
    per = E // G
    h2 = _rms(x1_ref[...]) * g_ref[...] * (1.0 + sc_ref[...]) + sh_ref[...]
    _store_row_tiles(h2_ref, h2)
    logits = lax.dot_general(wr_ref[...], h2, (((1,), (1,)), ((), ())),
                             precision=lax.Precision.HIGHEST,
                             preferred_element_type=F32)
    s = jax.nn.sigmoid(logits)
    biased = s + br_ref[...]
    sub = lax.broadcasted_iota(I32, (per, TB), 0)
    neg = -jnp.inf

    def first_argmax(vals, m, idx, sentinel):
        return jnp.min(jnp.where(vals == m, idx, sentinel), axis=0, keepdims=True)

    bg = [biased[g * per:(g + 1) * per, :] for g in range(G)]
    sg = [s[g * per:(g + 1) * per, :] for g in range(G)]
    gscore = []
    for g in range(G):
        m1 = jnp.max(bg[g], axis=0, keepdims=True)
        i1 = first_argmax(bg[g], m1, sub, per)
        m2 = jnp.max(jnp.where(sub == i1, neg, bg[g]), axis=0, keepdims=True)
        gscore.append(m1 + m2)
    gs = jnp.concatenate(gscore, axis=0)
    gidx = lax.broadcasted_iota(I32, (G, TB), 0)
    gsel = jnp.zeros((G, TB), F32)
    for _ in range(TOPK_GROUPS):
        m = jnp.max(gs, axis=0, keepdims=True)
        i = first_argmax(gs, m, gidx, G)
        hit = gidx == i
        gsel = jnp.where(hit, 1.0, gsel)
        gs = jnp.where(hit, neg, gs)
    cand = [jnp.where(gsel[g:g + 1, :] > 0.0, bg[g], neg) for g in range(G)]
    eidx = [sub + g * per for g in range(G)]
    sel = [jnp.zeros((per, TB), F32) for _ in range(G)]
    top_i, top_w = [], []
    for _ in range(TOP_K):
        m = functools.reduce(jnp.maximum, [jnp.max(c, axis=0, keepdims=True) for c in cand])
        i = functools.reduce(jnp.minimum,
                             [first_argmax(cand[g], m, eidx[g], E) for g in range(G)])
        w = jnp.zeros((1, TB), F32)
        for g in range(G):
            hit = eidx[g] == i
            w = w + jnp.sum(jnp.where(hit, sg[g], 0.0), axis=0, keepdims=True)
            sel[g] = jnp.where(hit, 1.0, sel[g])
            cand[g] = jnp.where(hit, neg, cand[g])
        top_i.append(i)
        top_w.append(w)
    wsum = functools.reduce(lambda a, b: a + b, top_w)
    wts_ref[...] = jnp.concatenate([w / wsum * ROUTED_SCALE for w in top_w], axis=0)

    mask = jnp.concatenate(sel, axis=0)
    t_row = lax.broadcasted_iota(I32, (TB, TB), 0)
    t_col = lax.broadcasted_iota(I32, (TB, TB), 1)
    before = jnp.where(t_row < t_col, 1.0, 0.0).astype(BF16)
    rank = jnp.dot(mask.astype(BF16), before, preferred_element_type=F32)
    cnt = jnp.sum(mask, axis=1, keepdims=True)
    pc_rep = jnp.broadcast_to(cnt, (E, LANES))
    e_row = lax.broadcasted_iota(I32, (E, E), 0)
    e_col = lax.broadcasted_iota(I32, (E, E), 1)
    lower = jnp.where(e_col < e_row, 1.0, 0.0)
    off_rep = jnp.dot(lower, pc_rep, precision=lax.Precision.HIGHEST,
                      preferred_element_type=F32)
    off_ref[...] = off_rep.astype(I32)
    pc_ref[...] = pc_rep.astype(I32)
    posfull = off_rep[:, :1] + rank
    pos = []
    for kk in range(TOP_K):
        p = jnp.zeros((1, TB), F32)
        for g in range(G):
            p = p + jnp.sum(jnp.where(eidx[g] == top_i[kk], posfull[g * per:(g + 1) * per, :], 0.0),
                            axis=0, keepdims=True)
        pos.append(p.astype(I32))
    pp_ref[...] = jnp.concatenate(
        [pos[2 * m] | (pos[2 * m + 1] << 16) for m in range(TOP_K // 2)], axis=0)


def _router(x1, mod3, g_norm2, w_router, b_router, S):
    T, D = x1.shape
    TB = min(MOE_TB, T)
    nb = T // TB
    per_b = S // TB
    E = N_EXPERTS
    return pl.pallas_call(
        _router_kernel,
        grid=(nb,),
        in_specs=[pl.BlockSpec((TB, D), lambda i: (i, 0)),
                  pl.BlockSpec((None, 1, D), lambda i: ((i // per_b) * 6 + 4, 0, 0)),
                  pl.BlockSpec((None, 1, D), lambda i: ((i // per_b) * 6 + 3, 0, 0)),
                  pl.BlockSpec((1, D), lambda i: (0, 0)),
                  pl.BlockSpec((E, D), lambda i: (0, 0)),
                  pl.BlockSpec((E, 1), lambda i: (0, 0))],
        out_specs=[pl.BlockSpec((TB * D // LANES, LANES), lambda i: (i, 0)),
                   pl.BlockSpec((TOP_K, TB), lambda i: (0, i)),
                   pl.BlockSpec((TOP_K // 2, TB), lambda i: (0, i)),
                   pl.BlockSpec((E, LANES), lambda i: (i, 0)),
                   pl.BlockSpec((E, LANES), lambda i: (i, 0))],
        out_shape=[jax.ShapeDtypeStruct((T * D // LANES, LANES), F32),
                   jax.ShapeDtypeStruct((TOP_K, T), F32),
                   jax.ShapeDtypeStruct((TOP_K // 2, T), I32),
                   jax.ShapeDtypeStruct((nb * E, LANES), I32),
                   jax.ShapeDtypeStruct((nb * E, LANES), I32)],
        compiler_params=_cparams(("arbitrary",)),
        name="moe_router",
    )(x1, mod3, mod3, g_norm2.reshape(1, D), w_router.T, b_router.reshape(E, 1))


def _moe_kernel(pp_sm, w_sm, off_sm, cnt_sm, h2_ref, wg_ref, wu_ref, wd_ref, *rest, TB):
    out_ref, xs_scr = rest[-2:]
    j = pl.program_id(0)
    step = pl.program_id(1)
    E = N_EXPERTS
    D = wg_ref.shape[1]
    RT = D // LANES
    CH = MOE_CH

    def tile(ref, r):
        return ref.at[pl.ds(pl.multiple_of(r * RT, RT), RT), :]

    def positions(t):
        out = []
        for m in range(TOP_K // 2):
            word = pp_sm[(j * TB + t) * (TOP_K // 2) + m]
            out.append(word & 0xFFFF)
            out.append(word >> 16)
        return out

    @pl.when(step == 0)
    def _dispatch():
        xs_scr[TOP_K * TB * RT:, :] = jnp.zeros((2 * CH * RT, LANES), F32)

        def scatter(t, carry):
            row = tile(h2_ref, t)[...]
            for p in positions(t):
                tile(xs_scr, p)[...] = row
            return carry

        lax.fori_loop(0, TB, scatter, 0, unroll=2)

    rows = lax.broadcasted_iota(I32, (CH, D), 0)

    def run_chunks(ee, r0, n_left, k):
        blks = [xs_scr.at[pl.ds(pl.multiple_of((r0 + i * CH) * RT, RT), CH * RT), :]
                for i in range(k)]
        xins = [_load_row_tiles(blk, CH) for blk in blks]
        outs = []
        for i, xin in enumerate(xins):
            xb = xin.astype(BF16)
            g = jnp.dot(xb, wg_ref[ee], preferred_element_type=F32)
            u = jnp.dot(xb, wu_ref[ee], preferred_element_type=F32)
            y = jnp.dot((_silu(g) * u).astype(BF16), wd_ref[ee], preferred_element_type=F32)
            outs.append(jnp.where(rows + i * CH < n_left, y, xin))
        for blk, out in zip(blks, outs):
            _store_row_tiles(blk, out)

    for ee in range(MOE_EPS):
        e = step * MOE_EPS + ee
        st = off_sm[j * E + e]
        n = cnt_sm[j * E + e]
        npairs = n // (2 * CH)

        def pair_body(i, carry, ee=ee, st=st, n=n):
            run_chunks(ee, st + i * 2 * CH, n - i * 2 * CH, 2)
            return carry

        lax.fori_loop(0, npairs, pair_body, 0)
        r1 = st + npairs * 2 * CH
        rem = n - npairs * 2 * CH

        @pl.when(rem > CH)
        def _(ee=ee, r1=r1, rem=rem):
            run_chunks(ee, r1, rem, 2)

        @pl.when((rem > 0) & (rem <= CH))
        def _(ee=ee, r1=r1, rem=rem):
            run_chunks(ee, r1, rem, 1)

    @pl.when(step == pl.num_programs(1) - 1)
    def _combine():
        def gather(t, carry):
            acc = None
            for k, p in enumerate(positions(t)):
                term = w_sm[(j * TB + t) * TOP_K + k] * tile(xs_scr, p)[...]
                acc = term if acc is None else acc + term
            tile(out_ref, t)[...] = acc
            return carry

        lax.fori_loop(0, TB, gather, 0, unroll=2)


def _moe(h2, wts, pp, off, pc, wg, wu, wd, n_split):
    D = wg.shape[1]
    RT = D // LANES
    T = h2.shape[0] // RT
    E = N_EXPERTS
    EPS = MOE_EPS
    Ts = T // n_split
    TB = min(MOE_TB, Ts)
    nb = Ts // TB
    rows = TOP_K * TB + 2 * MOE_CH
    routed = None
    for s in range(n_split):
        tok = slice(s * Ts, (s + 1) * Ts)
        blk = slice(s * nb * E, (s + 1) * nb * E)
        in_specs = [pl.BlockSpec((TB * RT, LANES), lambda j, e, *_, s=s: (s * nb + j, 0),
                                 pipeline_mode=pl.Buffered(1)),
                    pl.BlockSpec((EPS, D, D_EXPERT), lambda j, e, *_: (e, 0, 0)),
                    pl.BlockSpec((EPS, D, D_EXPERT), lambda j, e, *_: (e, 0, 0)),
                    pl.BlockSpec((EPS, D_EXPERT, D), lambda j, e, *_: (e, 0, 0))]
        args = [pp[:, tok].T.reshape(-1), wts[:, tok].T.reshape(-1), off[blk], pc[blk],
                h2, wg, wu, wd]
        aliases = {}
        if routed is not None:
            in_specs.append(pl.BlockSpec(memory_space=pl.ANY))
            args.append(routed)
            aliases = {len(args) - 1: 0}
        grid_spec = pltpu.PrefetchScalarGridSpec(
            num_scalar_prefetch=4,
            grid=(nb, E // EPS),
            in_specs=in_specs,
            out_specs=pl.BlockSpec((TB * RT, LANES), lambda j, e, *_, s=s: (s * nb + j, 0),
                                   pipeline_mode=pl.Buffered(1)),
            scratch_shapes=[pltpu.VMEM((rows * RT, LANES), F32)],
        )
        routed = pl.pallas_call(
            functools.partial(_moe_kernel, TB=TB),
            grid_spec=grid_spec,
            out_shape=jax.ShapeDtypeStruct((T * RT, LANES), F32),
            input_output_aliases=aliases,
            compiler_params=_cparams(("arbitrary", "arbitrary")),
            name="moe_experts",
        )(*args)
    return routed


def _final_kernel(x1_ref, routed_ref, h2_ref, gt_ref, wsg_ref, wsu_ref, wsd_ref, gf_ref, o_ref):
    tm = x1_ref.shape[0]
    hb = _load_row_tiles(h2_ref, tm).astype(BF16)
    g = jnp.dot(hb, wsg_ref[...], preferred_element_type=F32)
    u = jnp.dot(hb, wsu_ref[...], preferred_element_type=F32)
    shared = jnp.dot((_silu(g) * u).astype(BF16), wsd_ref[...], preferred_element_type=F32)
    xo = x1_ref[...] + gt_ref[...] * (_load_row_tiles(routed_ref, tm) + shared)
    o_ref[...] = _rms(xo) * gf_ref[...]


def _final(x1, routed, h2, mod3, wsg, wsu, wsd, g_final, S):
    T, D = x1.shape
    tm = 512
    per_b = S // tm
    full = lambda shape: pl.BlockSpec(shape, lambda i: (0, 0))
    return pl.pallas_call(
        _final_kernel,
        grid=(T // tm,),
        in_specs=[pl.BlockSpec((tm, D), lambda i: (i, 0)),
                  pl.BlockSpec((tm * D // LANES, LANES), lambda i: (i, 0)),
                  pl.BlockSpec((tm * D // LANES, LANES), lambda i: (i, 0)),
                  pl.BlockSpec((None, 1, D), lambda i: ((i // per_b) * 6 + 5, 0, 0)),
                  full(wsg.shape), full(wsu.shape), full(wsd.shape), full((1, D))],
        out_specs=pl.BlockSpec((tm, D), lambda i: (i, 0)),
        out_shape=jax.ShapeDtypeStruct((T, D), F32),
        compiler_params=_cparams(("arbitrary",)),
        name="final_out",
    )(x1, routed, h2, mod3, wsg, wsu, wsd, g_final.reshape(1, D))


def _pack_w_in(w_in):
    D = w_in.shape[0]
    splits = [MLA_Q_RANK, MLA_KV_RANK, MLA_ROPE, RET_HEADS * RET_QK, RET_HEADS * RET_QK,
              RET_HEADS * RET_V, RET_HEADS * RET_V, D, D]
    idx = [int(v) for v in np.cumsum(splits)[:-1]]
    wcq, wckv, wkr, wrq, wrk, wrv, wrg, wga, wgb = jnp.split(w_in, idx, axis=1)
    hm = MLA_ROPE // 2
    zl = jnp.zeros((D, MLA_NOPE), w_in.dtype)
    zr = jnp.zeros((D, LANES - MLA_NOPE - MLA_ROPE), w_in.dtype)
    kr1 = jnp.concatenate([zl, wkr, zr], axis=1)
    kr2 = jnp.concatenate([zl, wkr[:, hm:], wkr[:, :hm], zr], axis=1)
    pad = jnp.zeros((D, Z_COLS - (Z_KR2 + LANES)), w_in.dtype)
    w = jnp.concatenate([wrv, wrg, wga, wgb, wrq, wrk, wcq, wckv, kr1, kr2, pad], axis=1)
    return w.astype(BF16)


def _pack_mla_weights(w_uq, w_ukv):
    H = MLA_HEADS
    hm = MLA_ROPE // 2
    wq = w_uq.reshape(MLA_Q_RANK, H, MLA_NOPE + MLA_ROPE)
    nope, pe = wq[..., :MLA_NOPE], wq[..., MLA_NOPE:]
    zpad = jnp.zeros((MLA_Q_RANK, H, LANES - MLA_NOPE - MLA_ROPE), w_uq.dtype)
    wq1 = jnp.concatenate([nope, pe, zpad], axis=-1).reshape(MLA_Q_RANK, H * LANES)
    wq2 = jnp.concatenate([jnp.zeros_like(nope), pe[..., hm:], pe[..., :hm], zpad],
                          axis=-1).reshape(MLA_Q_RANK, H * LANES)
    wkv = w_ukv.reshape(MLA_KV_RANK, H, MLA_NOPE + MLA_V)
    kn, vv = wkv[..., :MLA_NOPE], wkv[..., MLA_NOPE:]
    wk = jnp.concatenate([kn, jnp.zeros((MLA_KV_RANK, H, LANES - MLA_NOPE), w_ukv.dtype)],
                         axis=-1).reshape(MLA_KV_RANK, H * LANES)
    wv = vv.reshape(MLA_KV_RANK, H * MLA_V).T
    return wq1.astype(BF16), wq2.astype(BF16), wk.astype(BF16), wv.astype(BF16)


def kernel(x, c, positions, w_ada, b_ada, g_norm1, w_in, g_cq, w_uq, g_ckv, w_ukv, g_ret,
           w_o_mla, w_o_ret, w_out, g_norm2, w_router, b_router, w_exp_gate, w_exp_up,
           w_exp_down, w_sh_gate, w_sh_up, w_sh_down, g_final):
    B, S, D = x.shape
    T = B * S
    x2 = x.reshape(T, D)

    mod = _ada(c, w_ada, b_ada)
    mod3 = mod.reshape(B * 6, 1, D)
    cr, sr, cm, sm = _rope_tables(positions)

    z = _inproj(x2, mod3, g_norm1, _pack_w_in(w_in), S)
    wq1, wq2, wk, wv = _pack_mla_weights(w_uq, w_ukv)
    q, k, v = _mla_up(z, cm, sm, g_cq, g_ckv, wq1, wq2, wk, wv)
    o_mla = _attention(q, k, v, B, S)
    o_ret = _retention(z, cr, sr, g_ret, B, S)
    x1 = _merge(x2, o_mla, o_ret, z, mod3, w_o_mla.astype(BF16), w_o_ret.astype(BF16),
                w_out.astype(BF16), S)

    h2, wts, pp, off_rep, pc_rep = _router(x1, mod3, g_norm2, w_router, b_router, S)
    routed = _moe(h2, wts, pp, off_rep[:, 0], pc_rep[:, 0],
                  w_exp_gate.astype(BF16), w_exp_up.astype(BF16), w_exp_down.astype(BF16), n_split=B)
    out = _final(x1, routed, h2, mod3, w_sh_gate.astype(BF16), w_sh_up.astype(BF16),
                 w_sh_down.astype(BF16), g_final, S)
    return out.reshape(B, S, D)
```

```python
import functools
import math

import numpy as np
import jax
import jax.numpy as jnp
from jax import lax
from jax.experimental import pallas as pl
from jax.experimental.pallas import tpu as pltpu

F32 = jnp.float32
BF16 = jnp.bfloat16
I32 = jnp.int32

MLA_HEADS = 8
MLA_Q_RANK = 384
MLA_KV_RANK = 256
MLA_NOPE = 64
MLA_ROPE = 32
MLA_V = 64
RET_HEADS = 4
RET_QK = 128
RET_V = 256
RET_BLOCK = 256
ROPE_THETA = 10000.0
N_EXPERTS = 64
TOP_K = 8
N_GROUPS = 8
TOPK_GROUPS = 4
D_EXPERT = 256
ROUTED_SCALE = 2.5
RMS_EPS = 1e-6
GN_EPS = 1e-5

LANES = 128
SUBLANES = 8
VMEM_LIMIT = 56 * 1024 * 1024

Z_RV, Z_RG, Z_GA, Z_GB = 0, 1024, 2048, 3072
Z_RQ, Z_RK = 4096, 4608
Z_CQKV = 5120
Z_KR1, Z_KR2 = 5760, 5888
Z_COLS = 6144

LOG2E = 1.4426950408889634

ATTN_BQ = 512
ATTN_BK = 512
ATTN_HEADS = 2
ATTN_LROWS = 16

MOE_TB = 1024
MOE_CH = 128
MOE_EPS = 2


def _cparams(sem, flags=None):
    return pltpu.CompilerParams(dimension_semantics=sem, vmem_limit_bytes=VMEM_LIMIT, flags=flags)


def _rms(x):
    return x * lax.rsqrt(jnp.mean(x * x, axis=-1, keepdims=True) + RMS_EPS)


def _silu(x):
    return x * jax.nn.sigmoid(x)


def _load_row_tiles(ref, nrows):
    nchunk = ref.shape[0] // nrows
    return jnp.concatenate([ref[pl.ds(c, nrows, stride=nchunk), :] for c in range(nchunk)], axis=1)


def _store_row_tiles(ref, val):
    nrows, d = val.shape
    nchunk = d // LANES
    for c in range(nchunk):
        ref[pl.ds(c, nrows, stride=nchunk), :] = val[:, c * LANES:(c + 1) * LANES]


def _ada_kernel(c_ref, w_ref, b_ref, o_ref):
    c = c_ref[...]
    o_ref[...] = jnp.dot(_silu(c).astype(BF16), w_ref[...].astype(BF16),
                         preferred_element_type=F32) + b_ref[...]


def _ada(c, w_ada, b_ada):
    B, D = c.shape
    n_out = w_ada.shape[1]
    cp = jnp.zeros((SUBLANES, D), F32).at[:B].set(c)
    tn = D
    out = pl.pallas_call(
        _ada_kernel,
        grid=(n_out // tn,),
        in_specs=[pl.BlockSpec((SUBLANES, D), lambda j: (0, 0)),
                  pl.BlockSpec((D, tn), lambda j: (0, j)),
                  pl.BlockSpec((1, tn), lambda j: (0, j))],
        out_specs=pl.BlockSpec((SUBLANES, tn), lambda j: (0, j)),
        out_shape=jax.ShapeDtypeStruct((SUBLANES, n_out), F32),
        compiler_params=_cparams(("arbitrary",)),
        name="ada_mod",
    )(cp, w_ada, b_ada.reshape(1, n_out))
    return out[:B]


def _rope_kernel(pos_ref, inv_ref, cr_ref, sr_ref, cm_ref, sm_ref):
    ang = pos_ref[...].astype(F32) * inv_ref[...]
    c = jnp.cos(ang)
    s = jnp.sin(ang)
    lane = lax.broadcasted_iota(I32, c.shape, 1)
    half = RET_QK // 2
    cr_ref[...] = jnp.where(lane < half, c, pltpu.roll(c, half, 1))
    sr_ref[...] = jnp.where(lane < half, -s, pltpu.roll(s, half, 1))
    hm = MLA_ROPE // 2
    cm_ref[...] = jnp.where(lane < MLA_NOPE, 1.0,
                            jnp.where(lane < MLA_NOPE + hm, c,
                                      jnp.where(lane < MLA_NOPE + 2 * hm, pltpu.roll(c, hm, 1), 0.0)))
    sm_ref[...] = jnp.where(lane < MLA_NOPE, 0.0,
                            jnp.where(lane < MLA_NOPE + hm, -s,
                                      jnp.where(lane < MLA_NOPE + 2 * hm, pltpu.roll(s, hm, 1), 0.0)))


def _rope_tables(positions):
    T = positions.size
    tm = min(T, 1024)
    inv_r = 1.0 / (ROPE_THETA ** (jnp.arange(0, RET_QK, 2, dtype=F32) / RET_QK))
    inv_m = 1.0 / (ROPE_THETA ** (jnp.arange(0, MLA_ROPE, 2, dtype=F32) / MLA_ROPE))
    inv = jnp.zeros((1, LANES), F32).at[0, :RET_QK // 2].set(inv_r)
    inv = inv.at[0, MLA_NOPE:MLA_NOPE + MLA_ROPE // 2].set(inv_m)
    tab = jax.ShapeDtypeStruct((T, LANES), F32)
    spec = pl.BlockSpec((tm, LANES), lambda i: (i, 0))
    return pl.pallas_call(
        _rope_kernel,
        grid=(T // tm,),
        in_specs=[pl.BlockSpec((tm, 1), lambda i: (i, 0)),
                  pl.BlockSpec((1, LANES), lambda i: (0, 0))],
        out_specs=[spec, spec, spec, spec],
        out_shape=[tab, tab, tab, tab],
        compiler_params=_cparams(("arbitrary",)),
        name="rope_tables",
    )(positions.reshape(T, 1), inv)


def _inproj_kernel(x_ref, sc_ref, sh_ref, g_ref, w_ref, z_ref, h_scr):
    @pl.when(pl.program_id(1) == 0)
    def _():
        h = _rms(x_ref[...]) * g_ref[...] * (1.0 + sc_ref[...]) + sh_ref[...]
        h_scr[...] = h.astype(BF16)

    z_ref[...] = jnp.dot(h_scr[...], w_ref[...], preferred_element_type=F32).astype(z_ref.dtype)


def _inproj(x2, mod3, g_norm1, w_pack, S):
    T, D = x2.shape
    N = w_pack.shape[1]
    tm, tn = 512, 2048
    per_b = S // tm
    return pl.pallas_call(
        _inproj_kernel,
        grid=(T // tm, N // tn),
        in_specs=[pl.BlockSpec((tm, D), lambda i, j: (i, 0)),
                  pl.BlockSpec((None, 1, D), lambda i, j: ((i // per_b) * 6 + 1, 0, 0)),
                  pl.BlockSpec((None, 1, D), lambda i, j: ((i // per_b) * 6 + 0, 0, 0)),
                  pl.BlockSpec((1, D), lambda i, j: (0, 0)),
                  pl.BlockSpec((D, tn), lambda i, j: (0, j))],
        out_specs=pl.BlockSpec((tm, tn), lambda i, j: (i, j)),
        out_shape=jax.ShapeDtypeStruct((T, N), BF16),
        scratch_shapes=[pltpu.VMEM((tm, D), BF16)],
        compiler_params=_cparams(("arbitrary", "arbitrary")),
        name="in_proj",
    )(x2, mod3, mod3, g_norm1.reshape(1, D), w_pack)


def _mla_up_kernel(zc_ref, kr1_ref, kr2_ref, cm_ref, sm_ref, gq_ref, gkv_ref,
                   wq1_ref, wq2_ref, wk_ref, wv_ref, q_ref, k_ref, v_ref):
    zc = zc_ref[...].astype(F32)
    cqn = (_rms(zc[:, :MLA_Q_RANK]) * gq_ref[...]).astype(BF16)
    ckvn = (_rms(zc[:, MLA_Q_RANK:]) * gkv_ref[...]).astype(BF16)
    cm = cm_ref[...]
    sm = sm_ref[...]
    q1 = jnp.dot(cqn, wq1_ref[...], preferred_element_type=F32)
    q2 = jnp.dot(cqn, wq2_ref[...], preferred_element_type=F32)
    kn = jnp.dot(ckvn, wk_ref[...], preferred_element_type=F32)
    kpe = kr1_ref[...].astype(F32) * cm + kr2_ref[...].astype(F32) * sm
    qscale = (MLA_NOPE + MLA_ROPE) ** -0.5 * LOG2E
    for h in range(MLA_HEADS):
        sl = slice(h * LANES, (h + 1) * LANES)
        q_ref[:, sl] = ((q1[:, sl] * cm + q2[:, sl] * sm) * qscale).astype(BF16)
        k_ref[:, sl] = (kn[:, sl] + kpe).astype(BF16)
    vt = lax.dot_general(wv_ref[...], ckvn, (((1,), (1,)), ((), ())),
                         preferred_element_type=F32).astype(BF16)
    vrows = MLA_V + ATTN_LROWS
    for h in range(MLA_HEADS):
        v_ref[h * vrows:h * vrows + MLA_V, :] = vt[h * MLA_V:(h + 1) * MLA_V, :]
        v_ref[h * vrows + MLA_V:(h + 1) * vrows, :] = jnp.ones((ATTN_LROWS, vt.shape[1]), BF16)


def _mla_up(z, cm, sm, g_cq, g_ckv, wq1, wq2, wk, wv):
    T = z.shape[0]
    tm = 512
    HW = MLA_HEADS * LANES
    wc = MLA_Q_RANK + MLA_KV_RANK
    full = lambda shape: pl.BlockSpec(shape, lambda i: (0, 0))
    return pl.pallas_call(
        _mla_up_kernel,
        grid=(T // tm,),
        in_specs=[pl.BlockSpec((tm, wc), lambda i: (i, Z_CQKV // wc)),
                  pl.BlockSpec((tm, LANES), lambda i: (i, Z_KR1 // LANES)),
                  pl.BlockSpec((tm, LANES), lambda i: (i, Z_KR2 // LANES)),
                  pl.BlockSpec((tm, LANES), lambda i: (i, 0)),
                  pl.BlockSpec((tm, LANES), lambda i: (i, 0)),
                  full((1, MLA_Q_RANK)), full((1, MLA_KV_RANK)),
                  full(wq1.shape), full(wq2.shape), full(wk.shape), full(wv.shape)],
        out_specs=[pl.BlockSpec((tm, HW), lambda i: (i, 0)),
                   pl.BlockSpec((tm, HW), lambda i: (i, 0)),
                   pl.BlockSpec((MLA_HEADS * (MLA_V + ATTN_LROWS), tm), lambda i: (0, i))],
        out_shape=[jax.ShapeDtypeStruct((T, HW), BF16),
                   jax.ShapeDtypeStruct((T, HW), BF16),
                   jax.ShapeDtypeStruct((MLA_HEADS * (MLA_V + ATTN_LROWS), T), BF16)],
        compiler_params=_cparams(("arbitrary",)),
        name="mla_up",
    )(z, z, z, cm, sm, g_cq.reshape(1, -1), g_ckv.reshape(1, -1), wq1, wq2, wk, wv)


def _attn_kernel(q_ref, k_ref, vt_ref, o_ref, *scr, bq, bk):
    nh = ATTN_HEADS
    slots = (scr[:nh], scr[nh:2 * nh])
    p_scrs, acc_scr, m_scr = scr[2 * nh:3 * nh], scr[3 * nh], scr[3 * nh + 1]
    qi = pl.program_id(2)
    vrows = MLA_V + ATTN_LROWS
    qs = [q_ref[:, h * LANES:(h + 1) * LANES] for h in range(nh)]
    acc_scr[...] = jnp.zeros(acc_scr.shape, F32)
    m_scr[...] = jnp.full(m_scr.shape, -jnp.inf, F32)
    sub8 = lax.broadcasted_iota(I32, (SUBLANES, bq), 0)
    lane8 = lax.broadcasted_iota(I32, (SUBLANES, bq), 1)
    pack = 2 * SUBLANES

    def scores(kb, slot):
        k0 = pl.multiple_of(kb * bk, bk)
        for h in range(nh):
            slots[slot][h][...] = lax.dot_general(
                k_ref[pl.ds(k0, bk), h * LANES:(h + 1) * LANES], qs[h],
                (((1,), (1,)), ((), ())), preferred_element_type=F32)

    def update(kb, slot, masked):
        k0 = pl.multiple_of(kb * bk, bk)
        for h in range(nh):
            st, pr = slots[slot][h], p_scrs[h]
            parts = [None, None]
            for r in range(bk // SUBLANES):
                rs = slice(r * SUBLANES, (r + 1) * SUBLANES)
                x = st[rs, :]
                if masked:
                    x = jnp.where(lane8 >= sub8 + r * SUBLANES, x, -jnp.inf)
                    st[rs, :] = x
                parts[r % 2] = x if parts[r % 2] is None else jnp.maximum(parts[r % 2], x)
            m_cur = jnp.max(jnp.maximum(parts[0], parts[1]), axis=0, keepdims=True)
            m_prev = m_scr[h:h + 1, :]
            m_new = jnp.maximum(m_prev, m_cur)
            m_scr[h:h + 1, :] = m_new
            alpha = jnp.exp2(m_prev - m_new)
            for r in range(bk // pack):
                rs = slice(r * pack, (r + 1) * pack)
                pr[rs, :] = jnp.exp2(st[rs, :] - m_new).astype(BF16)
            rows = slice(h * vrows, (h + 1) * vrows)
            acc_scr[rows, :] = alpha * acc_scr[rows, :] + jnp.dot(
                vt_ref[rows, pl.ds(k0, bk)], pr[...], preferred_element_type=F32)

    scores(0, 0)
    npairs = qi // 2

    def pair(i, carry):
        kb = 2 * i
        scores(kb + 1, 1)
        update(kb, 0, False)
        scores(kb + 2, 0)
        update(kb + 1, 1, False)
        return carry

    lax.fori_loop(0, npairs, pair, 0)

    @pl.when(qi % 2 == 0)
    def _():
        update(qi, 0, True)

    @pl.when(qi % 2 == 1)
    def _():
        scores(qi, 1)
        update(qi - 1, 0, False)
        update(qi, 1, True)

    ot = jnp.concatenate(
        [acc_scr[h * vrows:h * vrows + MLA_V, :]
         / acc_scr[h * vrows + MLA_V:h * vrows + MLA_V + 1, :] for h in range(nh)], axis=0)
    o_ref[...] = ot.T.astype(o_ref.dtype)


def _attention(q, k, vt, B, S):
    T = q.shape[0]
    bq = min(ATTN_BQ, S)
    bk = min(ATTN_BK, S)
    nq = S // bq
    nh = ATTN_HEADS
    vrows = MLA_V + ATTN_LROWS
    assert bq == bk and nh <= SUBLANES
    kern = functools.partial(_attn_kernel, bq=bq, bk=bk)
    return pl.pallas_call(
        kern,
        grid=(B, MLA_HEADS // nh, nq),
        in_specs=[pl.BlockSpec((bq, nh * LANES), lambda b, p, i: (b * nq + i, p)),
                  pl.BlockSpec((S, nh * LANES), lambda b, p, i: (b, p)),
                  pl.BlockSpec((nh * vrows, S), lambda b, p, i: (p, b))],
        out_specs=pl.BlockSpec((bq, nh * MLA_V), lambda b, p, i: (b * nq + i, p)),
        out_shape=jax.ShapeDtypeStruct((T, MLA_HEADS * MLA_V), BF16),
        scratch_shapes=([pltpu.VMEM((bk, bq), F32)] * (2 * nh) + [pltpu.VMEM((bk, bq), BF16)] * nh
                        + [pltpu.VMEM((nh * vrows, bq), F32), pltpu.VMEM((SUBLANES, bq), F32)]),
        compiler_params=_cparams(("arbitrary", "arbitrary", "arbitrary")),
        name="mla_attention",
    )(q, k, vt)


def _ret_kernel(rq_ref, rk_ref, rv_ref, rg_ref, cr_ref, sr_ref, dec_ref, xi_ref, zeta_ref,
                g_ref, o_ref, state_scr):
    @pl.when(pl.program_id(1) == 0)
    def _():
        state_scr[...] = jnp.zeros(state_scr.shape, F32)

    C = rq_ref.shape[0]
    cr = cr_ref[...]
    sr = sr_ref[...]
    half = RET_QK // 2
    for h in range(RET_HEADS):
        qk = slice(h * RET_QK, (h + 1) * RET_QK)
        vv = slice(h * RET_V, (h + 1) * RET_V)
        rq = rq_ref[:, qk].astype(F32)
        rk = rk_ref[:, qk].astype(F32)
        q = rq * cr + pltpu.roll(rq, half, 1) * sr
        k = (rk * cr + pltpu.roll(rk, half, 1) * sr) * (RET_QK ** -0.5)
        v = rv_ref[:, vv]
        xi = xi_ref[h]
        state = state_scr[h]
        qb = q.astype(BF16)
        s = lax.dot_general(qb, k.astype(BF16), (((1,), (1,)), ((), ())),
                            preferred_element_type=F32) * dec_ref[h]
        inner = jnp.dot(s.astype(BF16), v, preferred_element_type=F32)
        cross = jnp.dot(qb, state.astype(BF16), preferred_element_type=F32) * xi
        kz = (k * zeta_ref[h]).astype(BF16)
        kv = lax.dot_general(kz, v, (((0,), (0,)), ((), ())), preferred_element_type=F32)
        state_scr[h] = xi[C - 1:C, :] * state + kv
        o = inner + cross
        mu = jnp.mean(o, axis=-1, keepdims=True)
        d = o - mu
        var = jnp.mean(d * d, axis=-1, keepdims=True)
        on = d * lax.rsqrt(var + GN_EPS) * g_ref[:, vv]
        o_ref[:, vv] = (_silu(rg_ref[:, vv].astype(F32)) * on).astype(o_ref.dtype)


def _retention_tables():
    C = RET_BLOCK
    h = np.arange(RET_HEADS, dtype=np.float64)
    log_g = np.log(1.0 - np.exp2(-5.0 - h))
    idx = np.arange(C, dtype=np.float64)
    diff = idx[:, None] - idx[None, :]
    decay = np.where(diff[None] >= 0, np.exp(np.maximum(diff, 0.0)[None] * log_g[:, None, None]), 0.0)
    zeta = np.exp((C - 1 - idx)[None, :] * log_g[:, None])
    xi = np.exp((idx + 1.0)[None, :] * log_g[:, None])
    zeta_rep = np.broadcast_to(zeta[:, :, None], (RET_HEADS, C, RET_QK))
    xi_rep = np.broadcast_to(xi[:, :, None], (RET_HEADS, C, RET_V))
    return (jnp.asarray(decay, F32), jnp.asarray(xi_rep, F32), jnp.asarray(zeta_rep, F32))


def _retention(z, cr, sr, g_ret, B, S):
    T = z.shape[0]
    C = RET_BLOCK
    N = S // C
    H = RET_HEADS
    WQ, WV = H * RET_QK, H * RET_V
    dec, xi, zeta = _retention_tables()
    row = lambda b, n: b * N + n
    const = lambda shape: pl.BlockSpec(shape, lambda b, n: (0,) * len(shape))
    return pl.pallas_call(
        _ret_kernel,
        grid=(B, N),
        in_specs=[pl.BlockSpec((C, WQ), lambda b, n: (row(b, n), Z_RQ // WQ)),
                  pl.BlockSpec((C, WQ), lambda b, n: (row(b, n), Z_RK // WQ)),
                  pl.BlockSpec((C, WV), lambda b, n: (row(b, n), Z_RV // WV)),
                  pl.BlockSpec((C, WV), lambda b, n: (row(b, n), Z_RG // WV)),
                  pl.BlockSpec((C, LANES), lambda b, n: (row(b, n), 0)),
                  pl.BlockSpec((C, LANES), lambda b, n: (row(b, n), 0)),
                  const((H, C, C)), const((H, C, RET_V)), const((H, C, RET_QK)), const((1, WV))],
        out_specs=pl.BlockSpec((C, WV), lambda b, n: (row(b, n), 0)),
        out_shape=jax.ShapeDtypeStruct((T, WV), BF16),
        scratch_shapes=[pltpu.VMEM((H, RET_QK, RET_V), F32)],
        compiler_params=_cparams(("arbitrary", "arbitrary")),
        name="retention",
    )(z, z, z, z, cr, sr, dec, xi, zeta, g_ret.reshape(1, -1))


def _merge_kernel(x_ref, oa_ref, ob_ref, ga_ref, gb_ref, gt_ref, wa_ref, wb_ref, wo_ref, x1_ref):
    a = jnp.dot(oa_ref[...], wa_ref[...], preferred_element_type=F32)
    b = jnp.dot(ob_ref[...], wb_ref[...], preferred_element_type=F32)
    merged = (jax.nn.sigmoid(ga_ref[...].astype(F32)) * a
              + jax.nn.sigmoid(gb_ref[...].astype(F32)) * b)
    y = jnp.dot(merged.astype(BF16), wo_ref[...], preferred_element_type=F32)
    x1_ref[...] = x_ref[...] + gt_ref[...] * y


def _merge(x2, o_mla, o_ret, z, mod3, wa, wb, wo, S):
    T, D = x2.shape
    tm = 512
    per_b = S // tm
    full = lambda shape: pl.BlockSpec(shape, lambda i: (0, 0))
    return pl.pallas_call(
        _merge_kernel,
        grid=(T // tm,),
        in_specs=[pl.BlockSpec((tm, D), lambda i: (i, 0)),
                  pl.BlockSpec((tm, o_mla.shape[1]), lambda i: (i, 0)),
                  pl.BlockSpec((tm, D), lambda i: (i, 0)),
                  pl.BlockSpec((tm, D), lambda i: (i, Z_GA // D)),
                  pl.BlockSpec((tm, D), lambda i: (i, Z_GB // D)),
                  pl.BlockSpec((None, 1, D), lambda i: ((i // per_b) * 6 + 2, 0, 0)),
                  full(wa.shape), full(wb.shape), full(wo.shape)],
        out_specs=pl.BlockSpec((tm, D), lambda i: (i, 0)),
        out_shape=jax.ShapeDtypeStruct((T, D), F32),
        compiler_params=_cparams(("arbitrary",)),
        name="merge_out",
    )(x2, o_mla, o_ret, z, z, mod3, wa, wb, wo)


def _router_kernel(x1_ref, sc_ref, sh_ref, g_ref, wr_ref, br_ref,
                   h2_ref, wts_ref, pp_ref, off_ref, pc_ref):
    TB = x1_ref.shape[0]
    D = x1_ref.shape[1]
    E, G = N_EXPERTS, N_GROUPS
    per = E // G
    h2 = _rms(x1_ref[...]) * g_ref[...] * (1.0 + sc_ref[...]) + sh_ref[...]
    _store_row_tiles(h2_ref, h2)
    logits = lax.dot_general(wr_ref[...], h2, (((1,), (1,)), ((), ())),
                             precision=lax.Precision.HIGHEST,
                             preferred_element_type=F32)
    s = jax.nn.sigmoid(logits)
    biased = s + br_ref[...]
    sub = lax.broadcasted_iota(I32, (per, TB), 0)
    neg = -jnp.inf

    def first_argmax(vals, m, idx, sentinel):
        return jnp.min(jnp.where(vals == m, idx, sentinel), axis=0, keepdims=True)

    bg = [biased[g * per:(g + 1) * per, :] for g in range(G)]
    sg = [s[g * per:(g + 1) * per, :] for g in range(G)]
    gscore = []
    for g in range(G):
        m1 = jnp.max(bg[g], axis=0, keepdims=True)
        i1 = first_argmax(bg[g], m1, sub, per)
        m2 = jnp.max(jnp.where(sub == i1, neg, bg[g]), axis=0, keepdims=True)
        gscore.append(m1 + m2)
    gs = jnp.concatenate(gscore, axis=0)
    gidx = lax.broadcasted_iota(I32, (G, TB), 0)
    gsel = jnp.zeros((G, TB), F32)
    for _ in range(TOPK_GROUPS):
        m = jnp.max(gs, axis=0, keepdims=True)
        i = first_argmax(gs, m, gidx, G)
        hit = gidx == i
        gsel = jnp.where(hit, 1.0, gsel)
        gs = jnp.where(hit, neg, gs)
    cand = [jnp.where(gsel[g:g + 1, :] > 0.0, bg[g], neg) for g in range(G)]
    eidx = [sub + g * per for g in range(G)]
    sel = [jnp.zeros((per, TB), F32) for _ in range(G)]
    top_i, top_w = [], []
    for _ in range(TOP_K):
        m = functools.reduce(jnp.maximum, [jnp.max(c, axis=0, keepdims=True) for c in cand])
        i = functools.reduce(jnp.minimum,
                             [first_argmax(cand[g], m, eidx[g], E) for g in range(G)])
        w = jnp.zeros((1, TB), F32)
        for g in range(G):
            hit = eidx[g] == i
            w = w + jnp.sum(jnp.where(hit, sg[g], 0.0), axis=0, keepdims=True)
            sel[g] = jnp.where(hit, 1.0, sel[g])
            cand[g] = jnp.where(hit, neg, cand[g])
        top_i.append(i)
        top_w.append(w)
    wsum = functools.reduce(lambda a, b: a + b, top_w)
    wts_ref[...] = jnp.concatenate([w / wsum * ROUTED_SCALE for w in top_w], axis=0)

    mask = jnp.concatenate(sel, axis=0)
    t_row = lax.broadcasted_iota(I32, (TB, TB), 0)
    t_col = lax.broadcasted_iota(I32, (TB, TB), 1)
    before = jnp.where(t_row < t_col, 1.0, 0.0).astype(BF16)
    rank = jnp.dot(mask.astype(BF16), before, preferred_element_type=F32)
    cnt = jnp.sum(mask, axis=1, keepdims=True)
    pc_rep = jnp.broadcast_to(cnt, (E, LANES))
    e_row = lax.broadcasted_iota(I32, (E, E), 0)
    e_col = lax.broadcasted_iota(I32, (E, E), 1)
    lower = jnp.where(e_col < e_row, 1.0, 0.0)
    off_rep = jnp.dot(lower, pc_rep, precision=lax.Precision.HIGHEST,
                      preferred_element_type=F32)
    off_ref[...] = off_rep.astype(I32)
    pc_ref[...] = pc_rep.astype(I32)
    posfull = off_rep[:, :1] + rank
    pos = []
    for kk in range(TOP_K):
        p = jnp.zeros((1, TB), F32)
        for g in range(G):
            p = p + jnp.sum(jnp.where(eidx[g] == top_i[kk], posfull[g * per:(g + 1) * per, :], 0.0),
                            axis=0, keepdims=True)
        pos.append(p.astype(I32))
    pp_ref[...] = jnp.concatenate(pos, axis=0) * (D // LANES)


def _router(x1, mod3, g_norm2, w_router, b_router, S):
    T, D = x1.shape
    TB = min(MOE_TB, T)
    nb = T // TB
    per_b = S // TB
    E = N_EXPERTS
    return pl.pallas_call(
        _router_kernel,
        grid=(nb,),
        in_specs=[pl.BlockSpec((TB, D), lambda i: (i, 0)),
                  pl.BlockSpec((None, 1, D), lambda i: ((i // per_b) * 6 + 4, 0, 0)),
                  pl.BlockSpec((None, 1, D), lambda i: ((i // per_b) * 6 + 3, 0, 0)),
                  pl.BlockSpec((1, D), lambda i: (0, 0)),
                  pl.BlockSpec((E, D), lambda i: (0, 0)),
                  pl.BlockSpec((E, 1), lambda i: (0, 0))],
        out_specs=[pl.BlockSpec((TB * D // LANES, LANES), lambda i: (i, 0)),
                   pl.BlockSpec((TOP_K, TB), lambda i: (0, i)),
                   pl.BlockSpec((TOP_K, TB), lambda i: (0, i)),
                   pl.BlockSpec((E, LANES), lambda i: (i, 0)),
                   pl.BlockSpec((E, LANES), lambda i: (i, 0))],
        out_shape=[jax.ShapeDtypeStruct((T * D // LANES, LANES), F32),
                   jax.ShapeDtypeStruct((TOP_K, T), F32),
                   jax.ShapeDtypeStruct((TOP_K, T), I32),
                   jax.ShapeDtypeStruct((nb * E, LANES), I32),
                   jax.ShapeDtypeStruct((nb * E, LANES), I32)],
        compiler_params=_cparams(("arbitrary",)),
        name="moe_router",
    )(x1, mod3, mod3, g_norm2.reshape(1, D), w_router.T, b_router.reshape(E, 1))


def _moe_kernel(pp_sm, w_sm, off_sm, cnt_sm, h2_ref, wg_ref, wu_ref, wd_ref, *rest, TB):
    out_ref, xs_scr = rest[-2:]
    j = pl.program_id(0)
    step = pl.program_id(1)
    E = N_EXPERTS
    D = wg_ref.shape[1]
    RT = D // LANES
    CH = MOE_CH

    def tile(ref, first):
        return ref.at[pl.ds(pl.multiple_of(first, RT), RT), :]

    def positions(t):
        return [pp_sm[(j * TB + t) * TOP_K + k] for k in range(TOP_K)]

    @pl.when(step == 0)
    def _dispatch():
        xs_scr[TOP_K * TB * RT:, :] = jnp.zeros((2 * CH * RT, LANES), F32)

        def scatter(t, carry):
            row = tile(h2_ref, t * RT)[...]
            for p in positions(t):
                tile(xs_scr, p)[...] = row
            return carry

        lax.fori_loop(0, TB, scatter, 0, unroll=4)

    def run_rows(ee, r0, n_left, nrows):
        blk = xs_scr.at[pl.ds(pl.multiple_of(r0 * RT, RT), nrows * RT), :]
        xin = _load_row_tiles(blk, nrows)
        xb = xin.astype(BF16)
        g = jnp.dot(xb, wg_ref[ee], preferred_element_type=F32)
        u = jnp.dot(xb, wu_ref[ee], preferred_element_type=F32)
        y = jnp.dot((_silu(g) * u).astype(BF16), wd_ref[ee], preferred_element_type=F32)
        rows = lax.broadcasted_iota(I32, (nrows, D), 0)
        _store_row_tiles(blk, jnp.where(rows < n_left, y, xin))

    for ee in range(MOE_EPS):
        e = step * MOE_EPS + ee
        st = off_sm[j * E + e]
        n = cnt_sm[j * E + e]
        nbig = n // (2 * CH)

        def big_body(i, carry, ee=ee, st=st, n=n):
            run_rows(ee, st + i * 2 * CH, n - i * 2 * CH, 2 * CH)
            return carry

        lax.fori_loop(0, nbig, big_body, 0)
        r1 = st + nbig * 2 * CH
        rem = n - nbig * 2 * CH

        @pl.when(rem > CH)
        def _(ee=ee, r1=r1, rem=rem):
            run_rows(ee, r1, rem, 2 * CH)

        @pl.when((rem > 0) & (rem <= CH))
        def _(ee=ee, r1=r1, rem=rem):
            run_rows(ee, r1, rem, CH)

    @pl.when(step == pl.num_programs(1) - 1)
    def _combine():
        def gather(t, carry):
            acc = None
            for k, p in enumerate(positions(t)):
                term = w_sm[(j * TB + t) * TOP_K + k] * tile(xs_scr, p)[...]
                acc = term if acc is None else acc + term
            tile(out_ref, t * RT)[...] = acc
            return carry

        lax.fori_loop(0, TB, gather, 0, unroll=4)


def _moe(h2, wts, pp, off, pc, wg, wu, wd, n_split):
    D = wg.shape[1]
    RT = D // LANES
    T = h2.shape[0] // RT
    E = N_EXPERTS
    EPS = MOE_EPS
    Ts = T // n_split
    TB = min(MOE_TB, Ts)
    nb = Ts // TB
    rows = TOP_K * TB + 2 * MOE_CH
    routed = None
    for s in range(n_split):
        tok = slice(s * Ts, (s + 1) * Ts)
        blk = slice(s * nb * E, (s + 1) * nb * E)
        in_specs = [pl.BlockSpec((TB * RT, LANES), lambda j, e, *_, s=s: (s * nb + j, 0),
                                 pipeline_mode=pl.Buffered(1)),
                    pl.BlockSpec((EPS, D, D_EXPERT), lambda j, e, *_: (e, 0, 0)),
                    pl.BlockSpec((EPS, D, D_EXPERT), lambda j, e, *_: (e, 0, 0)),
                    pl.BlockSpec((EPS, D_EXPERT, D), lambda j, e, *_: (e, 0, 0))]
        args = [pp[:, tok].T.reshape(-1), wts[:, tok].T.reshape(-1), off[blk], pc[blk],
                h2, wg, wu, wd]
        aliases = {}
        if routed is not None:
            in_specs.append(pl.BlockSpec(memory_space=pl.ANY))
            args.append(routed)
            aliases = {len(args) - 1: 0}
        grid_spec = pltpu.PrefetchScalarGridSpec(
            num_scalar_prefetch=4,
            grid=(nb, E // EPS),
            in_specs=in_specs,
            out_specs=pl.BlockSpec((TB * RT, LANES), lambda j, e, *_, s=s: (s * nb + j, 0),
                                   pipeline_mode=pl.Buffered(1)),
            scratch_shapes=[pltpu.VMEM((rows * RT, LANES), F32)],
        )
        routed = pl.pallas_call(
            functools.partial(_moe_kernel, TB=TB),
            grid_spec=grid_spec,
            out_shape=jax.ShapeDtypeStruct((T * RT, LANES), F32),
            input_output_aliases=aliases,
            compiler_params=_cparams(("arbitrary", "arbitrary")),
            name="moe_experts",
        )(*args)
    return routed


def _final_kernel(x1_ref, routed_ref, h2_ref, gt_ref, wsg_ref, wsu_ref, wsd_ref, gf_ref, o_ref):
    tm = x1_ref.shape[0]
    hb = _load_row_tiles(h2_ref, tm).astype(BF16)
    g = jnp.dot(hb, wsg_ref[...], preferred_element_type=F32)
    u = jnp.dot(hb, wsu_ref[...], preferred_element_type=F32)
    shared = jnp.dot((_silu(g) * u).astype(BF16), wsd_ref[...], preferred_element_type=F32)
    xo = x1_ref[...] + gt_ref[...] * (_load_row_tiles(routed_ref, tm) + shared)
    o_ref[...] = _rms(xo) * gf_ref[...]


def _final(x1, routed, h2, mod3, wsg, wsu, wsd, g_final, S):
    T, D = x1.shape
    tm = 512
    per_b = S // tm
    full = lambda shape: pl.BlockSpec(shape, lambda i: (0, 0))
    return pl.pallas_call(
        _final_kernel,
        grid=(T // tm,),
        in_specs=[pl.BlockSpec((tm, D), lambda i: (i, 0)),
                  pl.BlockSpec((tm * D // LANES, LANES), lambda i: (i, 0)),
                  pl.BlockSpec((tm * D // LANES, LANES), lambda i: (i, 0)),
                  pl.BlockSpec((None, 1, D), lambda i: ((i // per_b) * 6 + 5, 0, 0)),
                  full(wsg.shape), full(wsu.shape), full(wsd.shape), full((1, D))],
        out_specs=pl.BlockSpec((tm, D), lambda i: (i, 0)),
        out_shape=jax.ShapeDtypeStruct((T, D), F32),
        compiler_params=_cparams(("arbitrary",)),
        name="final_out",
    )(x1, routed, h2, mod3, wsg, wsu, wsd, g_final.reshape(1, D))


def _pack_w_in(w_in):
    D = w_in.shape[0]
    splits = [MLA_Q_RANK, MLA_KV_RANK, MLA_ROPE, RET_HEADS * RET_QK, RET_HEADS * RET_QK,
              RET_HEADS * RET_V, RET_HEADS * RET_V, D, D]
    idx = [int(v) for v in np.cumsum(splits)[:-1]]
    wcq, wckv, wkr, wrq, wrk, wrv, wrg, wga, wgb = jnp.split(w_in, idx, axis=1)
    hm = MLA_ROPE // 2
    zl = jnp.zeros((D, MLA_NOPE), w_in.dtype)
    zr = jnp.zeros((D, LANES - MLA_NOPE - MLA_ROPE), w_in.dtype)
    kr1 = jnp.concatenate([zl, wkr, zr], axis=1)
    kr2 = jnp.concatenate([zl, wkr[:, hm:], wkr[:, :hm], zr], axis=1)
    pad = jnp.zeros((D, Z_COLS - (Z_KR2 + LANES)), w_in.dtype)
    w = jnp.concatenate([wrv, wrg, wga, wgb, wrq, wrk, wcq, wckv, kr1, kr2, pad], axis=1)
    return w.astype(BF16)


def _pack_mla_weights(w_uq, w_ukv):
    H = MLA_HEADS
    hm = MLA_ROPE // 2
    wq = w_uq.reshape(MLA_Q_RANK, H, MLA_NOPE + MLA_ROPE)
    nope, pe = wq[..., :MLA_NOPE], wq[..., MLA_NOPE:]
    zpad = jnp.zeros((MLA_Q_RANK, H, LANES - MLA_NOPE - MLA_ROPE), w_uq.dtype)
    wq1 = jnp.concatenate([nope, pe, zpad], axis=-1).reshape(MLA_Q_RANK, H * LANES)
    wq2 = jnp.concatenate([jnp.zeros_like(nope), pe[..., hm:], pe[..., :hm], zpad],
                          axis=-1).reshape(MLA_Q_RANK, H * LANES)
    wkv = w_ukv.reshape(MLA_KV_RANK, H, MLA_NOPE + MLA_V)
    kn, vv = wkv[..., :MLA_NOPE], wkv[..., MLA_NOPE:]
    wk = jnp.concatenate([kn, jnp.zeros((MLA_KV_RANK, H, LANES - MLA_NOPE), w_ukv.dtype)],
                         axis=-1).reshape(MLA_KV_RANK, H * LANES)
    wv = vv.reshape(MLA_KV_RANK, H * MLA_V).T
    return wq1.astype(BF16), wq2.astype(BF16), wk.astype(BF16), wv.astype(BF16)


def kernel(x, c, positions, w_ada, b_ada, g_norm1, w_in, g_cq, w_uq, g_ckv, w_ukv, g_ret,
           w_o_mla, w_o_ret, w_out, g_norm2, w_router, b_router, w_exp_gate, w_exp_up,
           w_exp_down, w_sh_gate, w_sh_up, w_sh_down, g_final):
    B, S, D = x.shape
    T = B * S
    x2 = x.reshape(T, D)

    mod = _ada(c, w_ada, b_ada)
    mod3 = mod.reshape(B * 6, 1, D)
    cr, sr, cm, sm = _rope_tables(positions)

    z = _inproj(x2, mod3, g_norm1, _pack_w_in(w_in), S)
    wq1, wq2, wk, wv = _pack_mla_weights(w_uq, w_ukv)
    q, k, v = _mla_up(z, cm, sm, g_cq, g_ckv, wq1, wq2, wk, wv)
    o_mla = _attention(q, k, v, B, S)
    o_ret = _retention(z, cr, sr, g_ret, B, S)
    x1 = _merge(x2, o_mla, o_ret, z, mod3, w_o_mla.astype(BF16), w_o_ret.astype(BF16),
                w_out.astype(BF16), S)

    h2, wts, pp, off_rep, pc_rep = _router(x1, mod3, g_norm2, w_router, b_router, S)
    routed = _moe(h2, wts, pp, off_rep[:, 0], pc_rep[:, 0],
                  w_exp_gate.astype(BF16), w_exp_up.astype(BF16), w_exp_down.astype(BF16), n_split=B)
    out = _final(x1, routed, h2, mod3, w_sh_gate.astype(BF16), w_sh_up.astype(BF16),
                 w_sh_down.astype(BF16), g_final, S)
    return out.reshape(B, S, D)
```

```python
import functools
import math

import numpy as np
import jax
import jax.numpy as jnp
from jax import lax
from jax.experimental import pallas as pl
from jax.experimental.pallas import tpu as pltpu

F32 = jnp.float32
BF16 = jnp.bfloat16
I32 = jnp.int32

MLA_HEADS = 8
MLA_Q_RANK = 384
MLA_KV_RANK = 256
MLA_NOPE = 64
MLA_ROPE = 32
MLA_V = 64
RET_HEADS = 4
RET_QK = 128
RET_V = 256
RET_BLOCK = 256
ROPE_THETA = 10000.0
N_EXPERTS = 64
TOP_K = 8
N_GROUPS = 8
TOPK_GROUPS = 4
D_EXPERT = 256
ROUTED_SCALE = 2.5
RMS_EPS = 1e-6
GN_EPS = 1e-5

LANES = 128
SUBLANES = 8
VMEM_LIMIT = 56 * 1024 * 1024

Z_RV, Z_RG, Z_GA, Z_GB = 0, 1024, 2048, 3072
Z_RQ, Z_RK = 4096, 4608
Z_CQKV = 5120
Z_KR1, Z_KR2 = 5760, 5888
Z_COLS = 6144

LOG2E = 1.4426950408889634

ATTN_BQ = 512
ATTN_BK = 512
ATTN_HEADS = 2
ATTN_LROWS = 16

MOE_TB = 1024
MOE_CH = 128
MOE_EPS = 4
MOE_VMEM_LIMIT = 58 * 1024 * 1024


def _cparams(sem, vmem_limit=VMEM_LIMIT):
    return pltpu.CompilerParams(dimension_semantics=sem, vmem_limit_bytes=vmem_limit)


def _rms(x):
    return x * lax.rsqrt(jnp.mean(x * x, axis=-1, keepdims=True) + RMS_EPS)


def _silu(x):
    return x * jax.nn.sigmoid(x)


def _load_row_tiles(ref, nrows):
    nchunk = ref.shape[0] // nrows
    return jnp.concatenate([ref[pl.ds(c, nrows, stride=nchunk), :] for c in range(nchunk)], axis=1)


def _store_row_tiles(ref, val):
    nrows, d = val.shape
    nchunk = d // LANES
    for c in range(nchunk):
        ref[pl.ds(c, nrows, stride=nchunk), :] = val[:, c * LANES:(c + 1) * LANES]


def _ada_kernel(c_ref, w_ref, b_ref, o_ref):
    c = c_ref[...]
    o_ref[...] = jnp.dot(_silu(c).astype(BF16), w_ref[...].astype(BF16),
                         preferred_element_type=F32) + b_ref[...]


def _ada(c, w_ada, b_ada):
    B, D = c.shape
    n_out = w_ada.shape[1]
    cp = jnp.zeros((SUBLANES, D), F32).at[:B].set(c)
    tn = D
    out = pl.pallas_call(
        _ada_kernel,
        grid=(n_out // tn,),
        in_specs=[pl.BlockSpec((SUBLANES, D), lambda j: (0, 0)),
                  pl.BlockSpec((D, tn), lambda j: (0, j)),
                  pl.BlockSpec((1, tn), lambda j: (0, j))],
        out_specs=pl.BlockSpec((SUBLANES, tn), lambda j: (0, j)),
        out_shape=jax.ShapeDtypeStruct((SUBLANES, n_out), F32),
        compiler_params=_cparams(("arbitrary",)),
        name="ada_mod",
    )(cp, w_ada, b_ada.reshape(1, n_out))
    return out[:B]


def _rope_kernel(pos_ref, inv_ref, cr_ref, sr_ref, cm_ref, sm_ref):
    ang = pos_ref[...].astype(F32) * inv_ref[...]
    c = jnp.cos(ang)
    s = jnp.sin(ang)
    lane = lax.broadcasted_iota(I32, c.shape, 1)
    half = RET_QK // 2
    cr_ref[...] = jnp.where(lane < half, c, pltpu.roll(c, half, 1))
    sr_ref[...] = jnp.where(lane < half, -s, pltpu.roll(s, half, 1))
    hm = MLA_ROPE // 2
    cm_ref[...] = jnp.where(lane < MLA_NOPE, 1.0,
                            jnp.where(lane < MLA_NOPE + hm, c,
                                      jnp.where(lane < MLA_NOPE + 2 * hm, pltpu.roll(c, hm, 1), 0.0)))
    sm_ref[...] = jnp.where(lane < MLA_NOPE, 0.0,
                            jnp.where(lane < MLA_NOPE + hm, -s,
                                      jnp.where(lane < MLA_NOPE + 2 * hm, pltpu.roll(s, hm, 1), 0.0)))


def _rope_tables(positions):
    T = positions.size
    tm = min(T, 1024)
    inv_r = 1.0 / (ROPE_THETA ** (jnp.arange(0, RET_QK, 2, dtype=F32) / RET_QK))
    inv_m = 1.0 / (ROPE_THETA ** (jnp.arange(0, MLA_ROPE, 2, dtype=F32) / MLA_ROPE))
    inv = jnp.zeros((1, LANES), F32).at[0, :RET_QK // 2].set(inv_r)
    inv = inv.at[0, MLA_NOPE:MLA_NOPE + MLA_ROPE // 2].set(inv_m)
    tab = jax.ShapeDtypeStruct((T, LANES), F32)
    spec = pl.BlockSpec((tm, LANES), lambda i: (i, 0))
    return pl.pallas_call(
        _rope_kernel,
        grid=(T // tm,),
        in_specs=[pl.BlockSpec((tm, 1), lambda i: (i, 0)),
                  pl.BlockSpec((1, LANES), lambda i: (0, 0))],
        out_specs=[spec, spec, spec, spec],
        out_shape=[tab, tab, tab, tab],
        compiler_params=_cparams(("arbitrary",)),
        name="rope_tables",
    )(positions.reshape(T, 1), inv)


def _inproj_kernel(x_ref, sc_ref, sh_ref, g_ref, w_ref, z_ref, h_scr):
    @pl.when(pl.program_id(1) == 0)
    def _():
        h = _rms(x_ref[...]) * g_ref[...] * (1.0 + sc_ref[...]) + sh_ref[...]
        h_scr[...] = h.astype(BF16)

    z_ref[...] = jnp.dot(h_scr[...], w_ref[...], preferred_element_type=F32).astype(z_ref.dtype)


def _inproj(x2, mod3, g_norm1, w_pack, S):
    T, D = x2.shape
    N = w_pack.shape[1]
    tm, tn = 1024, 2048
    per_b = S // tm
    return pl.pallas_call(
        _inproj_kernel,
        grid=(T // tm, N // tn),
        in_specs=[pl.BlockSpec((tm, D), lambda i, j: (i, 0)),
                  pl.BlockSpec((None, 1, D), lambda i, j: ((i // per_b) * 6 + 1, 0, 0)),
                  pl.BlockSpec((None, 1, D), lambda i, j: ((i // per_b) * 6 + 0, 0, 0)),
                  pl.BlockSpec((1, D), lambda i, j: (0, 0)),
                  pl.BlockSpec((D, tn), lambda i, j: (0, j))],
        out_specs=pl.BlockSpec((tm, tn), lambda i, j: (i, j)),
        out_shape=jax.ShapeDtypeStruct((T, N), BF16),
        scratch_shapes=[pltpu.VMEM((tm, D), BF16)],
        compiler_params=_cparams(("arbitrary", "arbitrary")),
        name="in_proj",
    )(x2, mod3, mod3, g_norm1.reshape(1, D), w_pack)


def _mla_up_kernel(zc_ref, kr1_ref, kr2_ref, cm_ref, sm_ref, gq_ref, gkv_ref,
                   wq1_ref, wq2_ref, wk_ref, wv_ref, q_ref, k_ref, v_ref):
    zc = zc_ref[...].astype(F32)
    cqn = (_rms(zc[:, :MLA_Q_RANK]) * gq_ref[...]).astype(BF16)
    ckvn = (_rms(zc[:, MLA_Q_RANK:]) * gkv_ref[...]).astype(BF16)
    cm = cm_ref[...]
    sm = sm_ref[...]
    q1 = jnp.dot(cqn, wq1_ref[...], preferred_element_type=F32)
    q2 = jnp.dot(cqn, wq2_ref[...], preferred_element_type=F32)
    kn = jnp.dot(ckvn, wk_ref[...], preferred_element_type=F32)
    kpe = kr1_ref[...].astype(F32) * cm + kr2_ref[...].astype(F32) * sm
    qscale = (MLA_NOPE + MLA_ROPE) ** -0.5 * LOG2E
    for h in range(MLA_HEADS):
        sl = slice(h * LANES, (h + 1) * LANES)
        q_ref[:, sl] = ((q1[:, sl] * cm + q2[:, sl] * sm) * qscale).astype(BF16)
        k_ref[:, sl] = (kn[:, sl] + kpe).astype(BF16)
    vt = lax.dot_general(wv_ref[...], ckvn, (((1,), (1,)), ((), ())),
                         preferred_element_type=F32).astype(BF16)
    vrows = MLA_V + ATTN_LROWS
    for h in range(MLA_HEADS):
        v_ref[h * vrows:h * vrows + MLA_V, :] = vt[h * MLA_V:(h + 1) * MLA_V, :]
        v_ref[h * vrows + MLA_V:(h + 1) * vrows, :] = jnp.ones((ATTN_LROWS, vt.shape[1]), BF16)


def _mla_up(z, cm, sm, g_cq, g_ckv, wq1, wq2, wk, wv):
    T = z.shape[0]
    tm = 512
    HW = MLA_HEADS * LANES
    wc = MLA_Q_RANK + MLA_KV_RANK
    full = lambda shape: pl.BlockSpec(shape, lambda i: (0, 0))
    return pl.pallas_call(
        _mla_up_kernel,
        grid=(T // tm,),
        in_specs=[pl.BlockSpec((tm, wc), lambda i: (i, Z_CQKV // wc)),
                  pl.BlockSpec((tm, LANES), lambda i: (i, Z_KR1 // LANES)),
                  pl.BlockSpec((tm, LANES), lambda i: (i, Z_KR2 // LANES)),
                  pl.BlockSpec((tm, LANES), lambda i: (i, 0)),
                  pl.BlockSpec((tm, LANES), lambda i: (i, 0)),
                  full((1, MLA_Q_RANK)), full((1, MLA_KV_RANK)),
                  full(wq1.shape), full(wq2.shape), full(wk.shape), full(wv.shape)],
        out_specs=[pl.BlockSpec((tm, HW), lambda i: (i, 0)),
                   pl.BlockSpec((tm, HW), lambda i: (i, 0)),
                   pl.BlockSpec((MLA_HEADS * (MLA_V + ATTN_LROWS), tm), lambda i: (0, i))],
        out_shape=[jax.ShapeDtypeStruct((T, HW), BF16),
                   jax.ShapeDtypeStruct((T, HW), BF16),
                   jax.ShapeDtypeStruct((MLA_HEADS * (MLA_V + ATTN_LROWS), T), BF16)],
        compiler_params=_cparams(("arbitrary",)),
        name="mla_up",
    )(z, z, z, cm, sm, g_cq.reshape(1, -1), g_ckv.reshape(1, -1), wq1, wq2, wk, wv)


def _attn_kernel(q_ref, k_ref, vt_ref, o_ref, *scr, bq, bk):
    nh = ATTN_HEADS
    slots = (scr[:nh], scr[nh:2 * nh])
    p_scrs, acc_scr, m_scr = scr[2 * nh:3 * nh], scr[3 * nh], scr[3 * nh + 1]
    qi = pl.program_id(2)
    vrows = MLA_V + ATTN_LROWS
    qs = [q_ref[:, h * LANES:(h + 1) * LANES] for h in range(nh)]
    acc_scr[...] = jnp.zeros(acc_scr.shape, F32)
    m_scr[...] = jnp.full(m_scr.shape, -jnp.inf, F32)
    sub8 = lax.broadcasted_iota(I32, (SUBLANES, bq), 0)
    lane8 = lax.broadcasted_iota(I32, (SUBLANES, bq), 1)
    pack = 2 * SUBLANES

    def scores(kb, slot):
        k0 = pl.multiple_of(kb * bk, bk)
        for h in range(nh):
            slots[slot][h][...] = lax.dot_general(
                k_ref[pl.ds(k0, bk), h * LANES:(h + 1) * LANES], qs[h],
                (((1,), (1,)), ((), ())), preferred_element_type=F32)

    def update(kb, slot, masked):
        k0 = pl.multiple_of(kb * bk, bk)
        for h in range(nh):
            st, pr = slots[slot][h], p_scrs[h]
            parts = [None, None]
            for r in range(bk // SUBLANES):
                rs = slice(r * SUBLANES, (r + 1) * SUBLANES)
                x = st[rs, :]
                if masked:
                    x = jnp.where(lane8 >= sub8 + r * SUBLANES, x, -jnp.inf)
                    st[rs, :] = x
                parts[r % 2] = x if parts[r % 2] is None else jnp.maximum(parts[r % 2], x)
            m_cur = jnp.max(jnp.maximum(parts[0], parts[1]), axis=0, keepdims=True)
            m_prev = m_scr[h:h + 1, :]
            m_new = jnp.maximum(m_prev, m_cur)
            m_scr[h:h + 1, :] = m_new
            alpha = jnp.exp2(m_prev - m_new)
            for r in range(bk // pack):
                rs = slice(r * pack, (r + 1) * pack)
                pr[rs, :] = jnp.exp2(st[rs, :] - m_new).astype(BF16)
            rows = slice(h * vrows, (h + 1) * vrows)
            acc_scr[rows, :] = alpha * acc_scr[rows, :] + jnp.dot(
                vt_ref[rows, pl.ds(k0, bk)], pr[...], preferred_element_type=F32)

    scores(0, 0)
    npairs = qi // 2

    def pair(i, carry):
        kb = 2 * i
        scores(kb + 1, 1)
        update(kb, 0, False)
        scores(kb + 2, 0)
        update(kb + 1, 1, False)
        return carry

    lax.fori_loop(0, npairs, pair, 0)

    @pl.when(qi % 2 == 0)
    def _():
        update(qi, 0, True)

    @pl.when(qi % 2 == 1)
    def _():
        scores(qi, 1)
        update(qi - 1, 0, False)
        update(qi, 1, True)

    ot = jnp.concatenate(
        [acc_scr[h * vrows:h * vrows + MLA_V, :]
         / acc_scr[h * vrows + MLA_V:h * vrows + MLA_V + 1, :] for h in range(nh)], axis=0)
    o_ref[...] = ot.T.astype(o_ref.dtype)


def _attention(q, k, vt, B, S):
    T = q.shape[0]
    bq = min(ATTN_BQ, S)
    bk = min(ATTN_BK, S)
    nq = S // bq
    nh = ATTN_HEADS
    vrows = MLA_V + ATTN_LROWS
    assert bq == bk and nh <= SUBLANES
    kern = functools.partial(_attn_kernel, bq=bq, bk=bk)
    return pl.pallas_call(
        kern,
        grid=(B, MLA_HEADS // nh, nq),
        in_specs=[pl.BlockSpec((bq, nh * LANES), lambda b, p, i: (b * nq + i, p)),
                  pl.BlockSpec((S, nh * LANES), lambda b, p, i: (b, p)),
                  pl.BlockSpec((nh * vrows, S), lambda b, p, i: (p, b))],
        out_specs=pl.BlockSpec((bq, nh * MLA_V), lambda b, p, i: (b * nq + i, p)),
        out_shape=jax.ShapeDtypeStruct((T, MLA_HEADS * MLA_V), BF16),
        scratch_shapes=([pltpu.VMEM((bk, bq), F32)] * (2 * nh) + [pltpu.VMEM((bk, bq), BF16)] * nh
                        + [pltpu.VMEM((nh * vrows, bq), F32), pltpu.VMEM((SUBLANES, bq), F32)]),
        compiler_params=_cparams(("arbitrary", "arbitrary", "arbitrary")),
        name="mla_attention",
    )(q, k, vt)


def _ret_kernel(rq_ref, rk_ref, rv_ref, rg_ref, cr_ref, sr_ref, dec_ref, xi_ref, zeta_ref,
                g_ref, o_ref, state_scr):
    @pl.when(pl.program_id(1) == 0)
    def _():
        state_scr[...] = jnp.zeros(state_scr.shape, F32)

    C = rq_ref.shape[0]
    cr = cr_ref[...]
    sr = sr_ref[...]
    half = RET_QK // 2
    for h in range(RET_HEADS):
        qk = slice(h * RET_QK, (h + 1) * RET_QK)
        vv = slice(h * RET_V, (h + 1) * RET_V)
        rq = rq_ref[:, qk].astype(F32)
        rk = rk_ref[:, qk].astype(F32)
        q = rq * cr + pltpu.roll(rq, half, 1) * sr
        k = (rk * cr + pltpu.roll(rk, half, 1) * sr) * (RET_QK ** -0.5)
        v = rv_ref[:, vv]
        xi = xi_ref[h]
        state = state_scr[h]
        qb = q.astype(BF16)
        s = lax.dot_general(qb, k.astype(BF16), (((1,), (1,)), ((), ())),
                            preferred_element_type=F32) * dec_ref[h]
        inner = jnp.dot(s.astype(BF16), v, preferred_element_type=F32)
        cross = jnp.dot(qb, state.astype(BF16), preferred_element_type=F32) * xi
        kz = (k * zeta_ref[h]).astype(BF16)
        kv = lax.dot_general(kz, v, (((0,), (0,)), ((), ())), preferred_element_type=F32)
        state_scr[h] = xi[C - 1:C, :] * state + kv
        o = inner + cross
        mu = jnp.mean(o, axis=-1, keepdims=True)
        d = o - mu
        var = jnp.mean(d * d, axis=-1, keepdims=True)
        on = d * lax.rsqrt(var + GN_EPS) * g_ref[:, vv]
        o_ref[:, vv] = (_silu(rg_ref[:, vv].astype(F32)) * on).astype(o_ref.dtype)


def _retention_tables():
    C = RET_BLOCK
    h = np.arange(RET_HEADS, dtype=np.float64)
    log_g = np.log(1.0 - np.exp2(-5.0 - h))
    idx = np.arange(C, dtype=np.float64)
    diff = idx[:, None] - idx[None, :]
    decay = np.where(diff[None] >= 0, np.exp(np.maximum(diff, 0.0)[None] * log_g[:, None, None]), 0.0)
    zeta = np.exp((C - 1 - idx)[None, :] * log_g[:, None])
    xi = np.exp((idx + 1.0)[None, :] * log_g[:, None])
    zeta_rep = np.broadcast_to(zeta[:, :, None], (RET_HEADS, C, RET_QK))
    xi_rep = np.broadcast_to(xi[:, :, None], (RET_HEADS, C, RET_V))
    return (jnp.asarray(decay, F32), jnp.asarray(xi_rep, F32), jnp.asarray(zeta_rep, F32))


def _retention(z, cr, sr, g_ret, B, S):
    T = z.shape[0]
    C = RET_BLOCK
    N = S // C
    H = RET_HEADS
    WQ, WV = H * RET_QK, H * RET_V
    dec, xi, zeta = _retention_tables()
    row = lambda b, n: b * N + n
    const = lambda shape: pl.BlockSpec(shape, lambda b, n: (0,) * len(shape))
    return pl.pallas_call(
        _ret_kernel,
        grid=(B, N),
        in_specs=[pl.BlockSpec((C, WQ), lambda b, n: (row(b, n), Z_RQ // WQ)),
                  pl.BlockSpec((C, WQ), lambda b, n: (row(b, n), Z_RK // WQ)),
                  pl.BlockSpec((C, WV), lambda b, n: (row(b, n), Z_RV // WV)),
                  pl.BlockSpec((C, WV), lambda b, n: (row(b, n), Z_RG // WV)),
                  pl.BlockSpec((C, LANES), lambda b, n: (row(b, n), 0)),
                  pl.BlockSpec((C, LANES), lambda b, n: (row(b, n), 0)),
                  const((H, C, C)), const((H, C, RET_V)), const((H, C, RET_QK)), const((1, WV))],
        out_specs=pl.BlockSpec((C, WV), lambda b, n: (row(b, n), 0)),
        out_shape=jax.ShapeDtypeStruct((T, WV), BF16),
        scratch_shapes=[pltpu.VMEM((H, RET_QK, RET_V), F32)],
        compiler_params=_cparams(("arbitrary", "arbitrary")),
        name="retention",
    )(z, z, z, z, cr, sr, dec, xi, zeta, g_ret.reshape(1, -1))


def _merge_kernel(x_ref, oa_ref, ob_ref, ga_ref, gb_ref, gt_ref, wa_ref, wb_ref, wo_ref, x1_ref):
    a = jnp.dot(oa_ref[...], wa_ref[...], preferred_element_type=F32)
    b = jnp.dot(ob_ref[...], wb_ref[...], preferred_element_type=F32)
    merged = (jax.nn.sigmoid(ga_ref[...].astype(F32)) * a
              + jax.nn.sigmoid(gb_ref[...].astype(F32)) * b)
    y = jnp.dot(merged.astype(BF16), wo_ref[...], preferred_element_type=F32)
    x1_ref[...] = x_ref[...] + gt_ref[...] * y


def _merge(x2, o_mla, o_ret, z, mod3, wa, wb, wo, S):
    T, D = x2.shape
    tm = 512
    per_b = S // tm
    full = lambda shape: pl.BlockSpec(shape, lambda i: (0, 0))
    return pl.pallas_call(
        _merge_kernel,
        grid=(T // tm,),
        in_specs=[pl.BlockSpec((tm, D), lambda i: (i, 0)),
                  pl.BlockSpec((tm, o_mla.shape[1]), lambda i: (i, 0)),
                  pl.BlockSpec((tm, D), lambda i: (i, 0)),
                  pl.BlockSpec((tm, D), lambda i: (i, Z_GA // D)),
                  pl.BlockSpec((tm, D), lambda i: (i, Z_GB // D)),
                  pl.BlockSpec((None, 1, D), lambda i: ((i // per_b) * 6 + 2, 0, 0)),
                  full(wa.shape), full(wb.shape), full(wo.shape)],
        out_specs=pl.BlockSpec((tm, D), lambda i: (i, 0)),
        out_shape=jax.ShapeDtypeStruct((T, D), F32),
        compiler_params=_cparams(("arbitrary",)),
        name="merge_out",
    )(x2, o_mla, o_ret, z, z, mod3, wa, wb, wo)


def _router_kernel(x1_ref, sc_ref, sh_ref, g_ref, wr_ref, br_ref,
                   h2_ref, wts_ref, pp_ref, off_ref, pc_ref):
    TB = x1_ref.shape[0]
    D = x1_ref.shape[1]
    E, G = N_EXPERTS, N_GROUPS
    per = E // G
    h2 = _rms(x1_ref[...]) * g_ref[...] * (1.0 + sc_ref[...]) + sh_ref[...]
    _store_row_tiles(h2_ref, h2)
    logits = lax.dot_general(wr_ref[...], h2, (((1,), (1,)), ((), ())),
                             precision=lax.Precision.HIGHEST,
                             preferred_element_type=F32)
    s = jax.nn.sigmoid(logits)
    biased = s + br_ref[...]
    sub = lax.broadcasted_iota(I32, (per, TB), 0)
    neg = -jnp.inf

    def first_argmax(vals, m, idx, sentinel):
        return jnp.min(jnp.where(vals == m, idx, sentinel), axis=0, keepdims=True)

    bg = [biased[g * per:(g + 1) * per, :] for g in range(G)]
    sg = [s[g * per:(g + 1) * per, :] for g in range(G)]
    gscore = []
    for g in range(G):
        m1 = jnp.max(bg[g], axis=0, keepdims=True)
        i1 = first_argmax(bg[g], m1, sub, per)
        m2 = jnp.max(jnp.where(sub == i1, neg, bg[g]), axis=0, keepdims=True)
        gscore.append(m1 + m2)
    gs = jnp.concatenate(gscore, axis=0)
    gidx = lax.broadcasted_iota(I32, (G, TB), 0)
    gsel = jnp.zeros((G, TB), F32)
    for _ in range(TOPK_GROUPS):
        m = jnp.max(gs, axis=0, keepdims=True)
        i = first_argmax(gs, m, gidx, G)
        hit = gidx == i
        gsel = jnp.where(hit, 1.0, gsel)
        gs = jnp.where(hit, neg, gs)
    cand = [jnp.where(gsel[g:g + 1, :] > 0.0, bg[g], neg) for g in range(G)]
    eidx = [sub + g * per for g in range(G)]
    sel = [jnp.zeros((per, TB), F32) for _ in range(G)]
    top_i, top_w = [], []
    for _ in range(TOP_K):
        m = functools.reduce(jnp.maximum, [jnp.max(c, axis=0, keepdims=True) for c in cand])
        i = functools.reduce(jnp.minimum,
                             [first_argmax(cand[g], m, eidx[g], E) for g in range(G)])
        w = jnp.zeros((1, TB), F32)
        for g in range(G):
            hit = eidx[g] == i
            w = w + jnp.sum(jnp.where(hit, sg[g], 0.0), axis=0, keepdims=True)
            sel[g] = jnp.where(hit, 1.0, sel[g])
            cand[g] = jnp.where(hit, neg, cand[g])
        top_i.append(i)
        top_w.append(w)
    wsum = functools.reduce(lambda a, b: a + b, top_w)
    wts_ref[...] = jnp.concatenate([w / wsum * ROUTED_SCALE for w in top_w], axis=0)

    mask = jnp.concatenate(sel, axis=0)
    t_row = lax.broadcasted_iota(I32, (TB, TB), 0)
    t_col = lax.broadcasted_iota(I32, (TB, TB), 1)
    before = jnp.where(t_row < t_col, 1.0, 0.0).astype(BF16)
    rank = jnp.dot(mask.astype(BF16), before, preferred_element_type=F32)
    cnt = jnp.sum(mask, axis=1, keepdims=True)
    pc_rep = jnp.broadcast_to(cnt, (E, LANES))
    e_row = lax.broadcasted_iota(I32, (E, E), 0)
    e_col = lax.broadcasted_iota(I32, (E, E), 1)
    lower = jnp.where(e_col < e_row, 1.0, 0.0)
    off_rep = jnp.dot(lower, pc_rep, precision=lax.Precision.HIGHEST,
                      preferred_element_type=F32)
    off_ref[...] = off_rep.astype(I32)
    pc_ref[...] = pc_rep.astype(I32)
    posfull = off_rep[:, :1] + rank
    pos = []
    for kk in range(TOP_K):
        p = jnp.zeros((1, TB), F32)
        for g in range(G):
            p = p + jnp.sum(jnp.where(eidx[g] == top_i[kk], posfull[g * per:(g + 1) * per, :], 0.0),
                            axis=0, keepdims=True)
        pos.append(p.astype(I32))
    pp_ref[...] = jnp.concatenate(pos, axis=0) * (D // LANES)


def _router(x1, mod3, g_norm2, w_router, b_router, S):
    T, D = x1.shape
    TB = min(MOE_TB, T)
    nb = T // TB
    per_b = S // TB
    E = N_EXPERTS
    return pl.pallas_call(
        _router_kernel,
        grid=(nb,),
        in_specs=[pl.BlockSpec((TB, D), lambda i: (i, 0)),
                  pl.BlockSpec((None, 1, D), lambda i: ((i // per_b) * 6 + 4, 0, 0)),
                  pl.BlockSpec((None, 1, D), lambda i: ((i // per_b) * 6 + 3, 0, 0)),
                  pl.BlockSpec((1, D), lambda i: (0, 0)),
                  pl.BlockSpec((E, D), lambda i: (0, 0)),
                  pl.BlockSpec((E, 1), lambda i: (0, 0))],
        out_specs=[pl.BlockSpec((TB * D // LANES, LANES), lambda i: (i, 0)),
                   pl.BlockSpec((TOP_K, TB), lambda i: (0, i)),
                   pl.BlockSpec((TOP_K, TB), lambda i: (0, i)),
                   pl.BlockSpec((E, LANES), lambda i: (i, 0)),
                   pl.BlockSpec((E, LANES), lambda i: (i, 0))],
        out_shape=[jax.ShapeDtypeStruct((T * D // LANES, LANES), F32),
                   jax.ShapeDtypeStruct((TOP_K, T), F32),
                   jax.ShapeDtypeStruct((TOP_K, T), I32),
                   jax.ShapeDtypeStruct((nb * E, LANES), I32),
                   jax.ShapeDtypeStruct((nb * E, LANES), I32)],
        compiler_params=_cparams(("arbitrary",)),
        name="moe_router",
    )(x1, mod3, mod3, g_norm2.reshape(1, D), w_router.T, b_router.reshape(E, 1))


def _moe_kernel(pp_sm, w_sm, off_sm, cnt_sm, h2_ref, wgu_ref, wd_ref, *rest, TB):
    out_ref, xs_scr = rest[-2:]
    j = pl.program_id(0)
    step = pl.program_id(1)
    E = N_EXPERTS
    D = wgu_ref.shape[1]
    RT = D // LANES
    CH = MOE_CH

    def tile(ref, first):
        return ref.at[pl.ds(pl.multiple_of(first, RT), RT), :]

    def positions(t):
        return [pp_sm[(j * TB + t) * TOP_K + k] for k in range(TOP_K)]

    @pl.when(step == 0)
    def _dispatch():
        xs_scr[TOP_K * TB * RT:, :] = jnp.zeros((2 * CH * RT, LANES), F32)

        def scatter(t, carry):
            row = tile(h2_ref, t * RT)[...]
            for p in positions(t):
                tile(xs_scr, p)[...] = row
            return carry

        lax.fori_loop(0, TB, scatter, 0, unroll=4)

    def run_rows(ee, r0, n_left, nrows):
        blk = xs_scr.at[pl.ds(pl.multiple_of(r0 * RT, RT), nrows * RT), :]
        xin = _load_row_tiles(blk, nrows)
        xb = xin.astype(BF16)
        gu = jnp.dot(xb, wgu_ref[ee], preferred_element_type=F32)
        a = _silu(gu[:, :D_EXPERT]) * gu[:, D_EXPERT:]
        y = jnp.dot(a.astype(BF16), wd_ref[ee], preferred_element_type=F32)
        rows = lax.broadcasted_iota(I32, (nrows, D), 0)
        _store_row_tiles(blk, jnp.where(rows < n_left, y, xin))

    for ee in range(MOE_EPS):
        e = step * MOE_EPS + ee
        st = off_sm[j * E + e]
        n = cnt_sm[j * E + e]
        nbig = n // (2 * CH)

        def big_body(i, carry, ee=ee, st=st, n=n):
            run_rows(ee, st + i * 2 * CH, n - i * 2 * CH, 2 * CH)
            return carry

        lax.fori_loop(0, nbig, big_body, 0)
        r1 = st + nbig * 2 * CH
        rem = n - nbig * 2 * CH

        @pl.when(rem > CH)
        def _(ee=ee, r1=r1, rem=rem):
            run_rows(ee, r1, rem, 2 * CH)

        @pl.when((rem > 0) & (rem <= CH))
        def _(ee=ee, r1=r1, rem=rem):
            run_rows(ee, r1, rem, CH)

    @pl.when(step == pl.num_programs(1) - 1)
    def _combine():
        def gather(t, carry):
            acc = None
            for k, p in enumerate(positions(t)):
                term = w_sm[(j * TB + t) * TOP_K + k] * tile(xs_scr, p)[...]
                acc = term if acc is None else acc + term
            tile(out_ref, t * RT)[...] = acc
            return carry

        lax.fori_loop(0, TB, gather, 0, unroll=4)


def _moe(h2, wts, pp, off, pc, wgu, wd, n_split):
    D = wgu.shape[1]
    RT = D // LANES
    T = h2.shape[0] // RT
    E = N_EXPERTS
    EPS = MOE_EPS
    Ts = T // n_split
    TB = min(MOE_TB, Ts)
    nb = Ts // TB
    rows = TOP_K * TB + 2 * MOE_CH
    routed = None
    for s in range(n_split):
        tok = slice(s * Ts, (s + 1) * Ts)
        blk = slice(s * nb * E, (s + 1) * nb * E)
        in_specs = [pl.BlockSpec((TB * RT, LANES), lambda j, e, *_, s=s: (s * nb + j, 0),
                                 pipeline_mode=pl.Buffered(1)),
                    pl.BlockSpec((EPS, D, 2 * D_EXPERT), lambda j, e, *_: (e, 0, 0)),
                    pl.BlockSpec((EPS, D_EXPERT, D), lambda j, e, *_: (e, 0, 0))]
        args = [pp[:, tok].T.reshape(-1), wts[:, tok].T.reshape(-1), off[blk], pc[blk],
                h2, wgu, wd]
        aliases = {}
        if routed is not None:
            in_specs.append(pl.BlockSpec(memory_space=pl.ANY))
            args.append(routed)
            aliases = {len(args) - 1: 0}
        grid_spec = pltpu.PrefetchScalarGridSpec(
            num_scalar_prefetch=4,
            grid=(nb, E // EPS),
            in_specs=in_specs,
            out_specs=pl.BlockSpec((TB * RT, LANES), lambda j, e, *_, s=s: (s * nb + j, 0),
                                   pipeline_mode=pl.Buffered(1)),
            scratch_shapes=[pltpu.VMEM((rows * RT, LANES), F32)],
        )
        routed = pl.pallas_call(
            functools.partial(_moe_kernel, TB=TB),
            grid_spec=grid_spec,
            out_shape=jax.ShapeDtypeStruct((T * RT, LANES), F32),
            input_output_aliases=aliases,
            compiler_params=_cparams(("arbitrary", "arbitrary"), MOE_VMEM_LIMIT),
            name="moe_experts",
        )(*args)
    return routed


def _final_kernel(x1_ref, routed_ref, h2_ref, gt_ref, wsg_ref, wsu_ref, wsd_ref, gf_ref, o_ref):
    tm = x1_ref.shape[0]
    hb = _load_row_tiles(h2_ref, tm).astype(BF16)
    g = jnp.dot(hb, wsg_ref[...], preferred_element_type=F32)
    u = jnp.dot(hb, wsu_ref[...], preferred_element_type=F32)
    shared = jnp.dot((_silu(g) * u).astype(BF16), wsd_ref[...], preferred_element_type=F32)
    xo = x1_ref[...] + gt_ref[...] * (_load_row_tiles(routed_ref, tm) + shared)
    o_ref[...] = _rms(xo) * gf_ref[...]


def _final(x1, routed, h2, mod3, wsg, wsu, wsd, g_final, S):
    T, D = x1.shape
    tm = 512
    per_b = S // tm
    full = lambda shape: pl.BlockSpec(shape, lambda i: (0, 0))
    return pl.pallas_call(
        _final_kernel,
        grid=(T // tm,),
        in_specs=[pl.BlockSpec((tm, D), lambda i: (i, 0)),
                  pl.BlockSpec((tm * D // LANES, LANES), lambda i: (i, 0)),
                  pl.BlockSpec((tm * D // LANES, LANES), lambda i: (i, 0)),
                  pl.BlockSpec((None, 1, D), lambda i: ((i // per_b) * 6 + 5, 0, 0)),
                  full(wsg.shape), full(wsu.shape), full(wsd.shape), full((1, D))],
        out_specs=pl.BlockSpec((tm, D), lambda i: (i, 0)),
        out_shape=jax.ShapeDtypeStruct((T, D), F32),
        compiler_params=_cparams(("arbitrary",)),
        name="final_out",
    )(x1, routed, h2, mod3, wsg, wsu, wsd, g_final.reshape(1, D))


def _pack_w_in(w_in):
    D = w_in.shape[0]
    splits = [MLA_Q_RANK, MLA_KV_RANK, MLA_ROPE, RET_HEADS * RET_QK, RET_HEADS * RET_QK,
              RET_HEADS * RET_V, RET_HEADS * RET_V, D, D]
    idx = [int(v) for v in np.cumsum(splits)[:-1]]
    wcq, wckv, wkr, wrq, wrk, wrv, wrg, wga, wgb = jnp.split(w_in, idx, axis=1)
    hm = MLA_ROPE // 2
    zl = jnp.zeros((D, MLA_NOPE), w_in.dtype)
    zr = jnp.zeros((D, LANES - MLA_NOPE - MLA_ROPE), w_in.dtype)
    kr1 = jnp.concatenate([zl, wkr, zr], axis=1)
    kr2 = jnp.concatenate([zl, wkr[:, hm:], wkr[:, :hm], zr], axis=1)
    pad = jnp.zeros((D, Z_COLS - (Z_KR2 + LANES)), w_in.dtype)
    w = jnp.concatenate([wrv, wrg, wga, wgb, wrq, wrk, wcq, wckv, kr1, kr2, pad], axis=1)
    return w.astype(BF16)


def _pack_mla_weights(w_uq, w_ukv):
    H = MLA_HEADS
    hm = MLA_ROPE // 2
    wq = w_uq.reshape(MLA_Q_RANK, H, MLA_NOPE + MLA_ROPE)
    nope, pe = wq[..., :MLA_NOPE], wq[..., MLA_NOPE:]
    zpad = jnp.zeros((MLA_Q_RANK, H, LANES - MLA_NOPE - MLA_ROPE), w_uq.dtype)
    wq1 = jnp.concatenate([nope, pe, zpad], axis=-1).reshape(MLA_Q_RANK, H * LANES)
    wq2 = jnp.concatenate([jnp.zeros_like(nope), pe[..., hm:], pe[..., :hm], zpad],
                          axis=-1).reshape(MLA_Q_RANK, H * LANES)
    wkv = w_ukv.reshape(MLA_KV_RANK, H, MLA_NOPE + MLA_V)
    kn, vv = wkv[..., :MLA_NOPE], wkv[..., MLA_NOPE:]
    wk = jnp.concatenate([kn, jnp.zeros((MLA_KV_RANK, H, LANES - MLA_NOPE), w_ukv.dtype)],
                         axis=-1).reshape(MLA_KV_RANK, H * LANES)
    wv = vv.reshape(MLA_KV_RANK, H * MLA_V).T
    return wq1.astype(BF16), wq2.astype(BF16), wk.astype(BF16), wv.astype(BF16)


def kernel(x, c, positions, w_ada, b_ada, g_norm1, w_in, g_cq, w_uq, g_ckv, w_ukv, g_ret,
           w_o_mla, w_o_ret, w_out, g_norm2, w_router, b_router, w_exp_gate, w_exp_up,
           w_exp_down, w_sh_gate, w_sh_up, w_sh_down, g_final):
    B, S, D = x.shape
    T = B * S
    x2 = x.reshape(T, D)

    mod = _ada(c, w_ada, b_ada)
    mod3 = mod.reshape(B * 6, 1, D)
    cr, sr, cm, sm = _rope_tables(positions)

    z = _inproj(x2, mod3, g_norm1, _pack_w_in(w_in), S)
    wq1, wq2, wk, wv = _pack_mla_weights(w_uq, w_ukv)
    q, k, v = _mla_up(z, cm, sm, g_cq, g_ckv, wq1, wq2, wk, wv)
    o_mla = _attention(q, k, v, B, S)
    o_ret = _retention(z, cr, sr, g_ret, B, S)
    x1 = _merge(x2, o_mla, o_ret, z, mod3, w_o_mla.astype(BF16), w_o_ret.astype(BF16),
                w_out.astype(BF16), S)

    h2, wts, pp, off_rep, pc_rep = _router(x1, mod3, g_norm2, w_router, b_router, S)
    routed = _moe(h2, wts, pp, off_rep[:, 0], pc_rep[:, 0],
                  jnp.concatenate([w_exp_gate.astype(BF16), w_exp_up.astype(BF16)], axis=2),
                  w_exp_down.astype(BF16), n_split=B)
    out = _final(x1, routed, h2, mod3, w_sh_gate.astype(BF16), w_sh_up.astype(BF16),
                 w_sh_down.astype(BF16), g_final, S)
    return out.reshape(B, S, D)
```

```python
import functools
import math

import numpy as np
import jax
import jax.numpy as jnp
from jax import lax
from jax.experimental import pallas as pl
from jax.experimental.pallas import tpu as pltpu

F32 = jnp.float32
BF16 = jnp.bfloat16
I32 = jnp.int32

MLA_HEADS = 8
MLA_Q_RANK = 384
MLA_KV_RANK = 256
MLA_NOPE = 64
MLA_ROPE = 32
MLA_V = 64
RET_HEADS = 4
RET_QK = 128
RET_V = 256
RET_BLOCK = 256
ROPE_THETA = 10000.0
N_EXPERTS = 64
TOP_K = 8
N_GROUPS = 8
TOPK_GROUPS = 4
D_EXPERT = 256
ROUTED_SCALE = 2.5
RMS_EPS = 1e-6
GN_EPS = 1e-5

LANES = 128
SUBLANES = 8
VMEM_LIMIT = 56 * 1024 * 1024

Z_RV, Z_RG, Z_GA, Z_GB = 0, 1024, 2048, 3072
Z_RQ, Z_RK = 4096, 4608
Z_CQKV = 5120
Z_KR1, Z_KR2 = 5760, 5888
Z_COLS = 6144

LOG2E = 1.4426950408889634

ATTN_BQ = 512
ATTN_BK = 512
ATTN_HEADS = 2
ATTN_LROWS = 16

MOE_TB = 1024
MOE_CH = 128
MOE_EPS = 4
MOE_VMEM_LIMIT = 58 * 1024 * 1024


def _cparams(sem, vmem_limit=VMEM_LIMIT):
    return pltpu.CompilerParams(dimension_semantics=sem, vmem_limit_bytes=vmem_limit)


def _rms(x):
    return x * lax.rsqrt(jnp.mean(x * x, axis=-1, keepdims=True) + RMS_EPS)


def _silu(x):
    return x * jax.nn.sigmoid(x)


def _load_row_tiles(ref, nrows):
    nchunk = ref.shape[0] // nrows
    return jnp.concatenate([ref[pl.ds(c, nrows, stride=nchunk), :] for c in range(nchunk)], axis=1)


def _store_row_tiles(ref, val):
    nrows, d = val.shape
    nchunk = d // LANES
    for c in range(nchunk):
        ref[pl.ds(c, nrows, stride=nchunk), :] = val[:, c * LANES:(c + 1) * LANES]


def _ada_kernel(c_ref, w_ref, b_ref, o_ref):
    c = c_ref[...]
    o_ref[...] = jnp.dot(_silu(c).astype(BF16), w_ref[...].astype(BF16),
                         preferred_element_type=F32) + b_ref[...]


def _ada(c, w_ada, b_ada):
    B, D = c.shape
    n_out = w_ada.shape[1]
    cp = jnp.zeros((SUBLANES, D), F32).at[:B].set(c)
    tn = D
    out = pl.pallas_call(
        _ada_kernel,
        grid=(n_out // tn,),
        in_specs=[pl.BlockSpec((SUBLANES, D), lambda j: (0, 0)),
                  pl.BlockSpec((D, tn), lambda j: (0, j)),
                  pl.BlockSpec((1, tn), lambda j: (0, j))],
        out_specs=pl.BlockSpec((SUBLANES, tn), lambda j: (0, j)),
        out_shape=jax.ShapeDtypeStruct((SUBLANES, n_out), F32),
        compiler_params=_cparams(("arbitrary",)),
        name="ada_mod",
    )(cp, w_ada, b_ada.reshape(1, n_out))
    return out[:B]


def _rope_kernel(pos_ref, inv_ref, cr_ref, sr_ref, cm_ref, sm_ref):
    ang = pos_ref[...].astype(F32) * inv_ref[...]
    c = jnp.cos(ang)
    s = jnp.sin(ang)
    lane = lax.broadcasted_iota(I32, c.shape, 1)
    half = RET_QK // 2
    cr_ref[...] = jnp.where(lane < half, c, pltpu.roll(c, half, 1))
    sr_ref[...] = jnp.where(lane < half, -s, pltpu.roll(s, half, 1))
    hm = MLA_ROPE // 2
    cm_ref[...] = jnp.where(lane < MLA_NOPE, 1.0,
                            jnp.where(lane < MLA_NOPE + hm, c,
                                      jnp.where(lane < MLA_NOPE + 2 * hm, pltpu.roll(c, hm, 1), 0.0)))
    sm_ref[...] = jnp.where(lane < MLA_NOPE, 0.0,
                            jnp.where(lane < MLA_NOPE + hm, -s,
                                      jnp.where(lane < MLA_NOPE + 2 * hm, pltpu.roll(s, hm, 1), 0.0)))


def _rope_tables(positions):
    T = positions.size
    tm = min(T, 1024)
    inv_r = 1.0 / (ROPE_THETA ** (jnp.arange(0, RET_QK, 2, dtype=F32) / RET_QK))
    inv_m = 1.0 / (ROPE_THETA ** (jnp.arange(0, MLA_ROPE, 2, dtype=F32) / MLA_ROPE))
    inv = jnp.zeros((1, LANES), F32).at[0, :RET_QK // 2].set(inv_r)
    inv = inv.at[0, MLA_NOPE:MLA_NOPE + MLA_ROPE // 2].set(inv_m)
    tab = jax.ShapeDtypeStruct((T, LANES), F32)
    spec = pl.BlockSpec((tm, LANES), lambda i: (i, 0))
    return pl.pallas_call(
        _rope_kernel,
        grid=(T // tm,),
        in_specs=[pl.BlockSpec((tm, 1), lambda i: (i, 0)),
                  pl.BlockSpec((1, LANES), lambda i: (0, 0))],
        out_specs=[spec, spec, spec, spec],
        out_shape=[tab, tab, tab, tab],
        compiler_params=_cparams(("arbitrary",)),
        name="rope_tables",
    )(positions.reshape(T, 1), inv)


def _inproj_kernel(x_ref, sc_ref, sh_ref, g_ref, w_ref, z_ref, h_scr):
    @pl.when(pl.program_id(1) == 0)
    def _():
        h = _rms(x_ref[...]) * g_ref[...] * (1.0 + sc_ref[...]) + sh_ref[...]
        h_scr[...] = h.astype(BF16)

    z_ref[...] = jnp.dot(h_scr[...], w_ref[...], preferred_element_type=F32).astype(z_ref.dtype)


def _inproj(x2, mod3, g_norm1, w_pack, S):
    T, D = x2.shape
    N = w_pack.shape[1]
    tm, tn = 1024, 2048
    per_b = S // tm
    return pl.pallas_call(
        _inproj_kernel,
        grid=(T // tm, N // tn),
        in_specs=[pl.BlockSpec((tm, D), lambda i, j: (i, 0)),
                  pl.BlockSpec((None, 1, D), lambda i, j: ((i // per_b) * 6 + 1, 0, 0)),
                  pl.BlockSpec((None, 1, D), lambda i, j: ((i // per_b) * 6 + 0, 0, 0)),
                  pl.BlockSpec((1, D), lambda i, j: (0, 0)),
                  pl.BlockSpec((D, tn), lambda i, j: (0, j))],
        out_specs=pl.BlockSpec((tm, tn), lambda i, j: (i, j)),
        out_shape=jax.ShapeDtypeStruct((T, N), BF16),
        scratch_shapes=[pltpu.VMEM((tm, D), BF16)],
        compiler_params=_cparams(("arbitrary", "arbitrary")),
        name="in_proj",
    )(x2, mod3, mod3, g_norm1.reshape(1, D), w_pack)


def _mla_up_kernel(zc_ref, kr1_ref, kr2_ref, cm_ref, sm_ref, gq_ref, gkv_ref,
                   wq1_ref, wq2_ref, wk_ref, wv_ref, q_ref, k_ref, v_ref):
    zc = zc_ref[...].astype(F32)
    cqn = (_rms(zc[:, :MLA_Q_RANK]) * gq_ref[...]).astype(BF16)
    ckvn = (_rms(zc[:, MLA_Q_RANK:]) * gkv_ref[...]).astype(BF16)
    cm = cm_ref[...]
    sm = sm_ref[...]
    q1 = jnp.dot(cqn, wq1_ref[...], preferred_element_type=F32)
    q2 = jnp.dot(cqn, wq2_ref[...], preferred_element_type=F32)
    kn = jnp.dot(ckvn, wk_ref[...], preferred_element_type=F32)
    kpe = kr1_ref[...].astype(F32) * cm + kr2_ref[...].astype(F32) * sm
    qscale = (MLA_NOPE + MLA_ROPE) ** -0.5 * LOG2E
    for h in range(MLA_HEADS):
        sl = slice(h * LANES, (h + 1) * LANES)
        q_ref[:, sl] = ((q1[:, sl] * cm + q2[:, sl] * sm) * qscale).astype(BF16)
        k_ref[:, sl] = (kn[:, sl] + kpe).astype(BF16)
    vt = lax.dot_general(wv_ref[...], ckvn, (((1,), (1,)), ((), ())),
                         preferred_element_type=F32).astype(BF16)
    vrows = MLA_V + ATTN_LROWS
    for h in range(MLA_HEADS):
        v_ref[h * vrows:h * vrows + MLA_V, :] = vt[h * MLA_V:(h + 1) * MLA_V, :]
        v_ref[h * vrows + MLA_V:(h + 1) * vrows, :] = jnp.ones((ATTN_LROWS, vt.shape[1]), BF16)


def _mla_up(z, cm, sm, g_cq, g_ckv, wq1, wq2, wk, wv):
    T = z.shape[0]
    tm = 512
    HW = MLA_HEADS * LANES
    wc = MLA_Q_RANK + MLA_KV_RANK
    full = lambda shape: pl.BlockSpec(shape, lambda i: (0, 0))
    return pl.pallas_call(
        _mla_up_kernel,
        grid=(T // tm,),
        in_specs=[pl.BlockSpec((tm, wc), lambda i: (i, Z_CQKV // wc)),
                  pl.BlockSpec((tm, LANES), lambda i: (i, Z_KR1 // LANES)),
                  pl.BlockSpec((tm, LANES), lambda i: (i, Z_KR2 // LANES)),
                  pl.BlockSpec((tm, LANES), lambda i: (i, 0)),
                  pl.BlockSpec((tm, LANES), lambda i: (i, 0)),
                  full((1, MLA_Q_RANK)), full((1, MLA_KV_RANK)),
                  full(wq1.shape), full(wq2.shape), full(wk.shape), full(wv.shape)],
        out_specs=[pl.BlockSpec((tm, HW), lambda i: (i, 0)),
                   pl.BlockSpec((tm, HW), lambda i: (i, 0)),
                   pl.BlockSpec((MLA_HEADS * (MLA_V + ATTN_LROWS), tm), lambda i: (0, i))],
        out_shape=[jax.ShapeDtypeStruct((T, HW), BF16),
                   jax.ShapeDtypeStruct((T, HW), BF16),
                   jax.ShapeDtypeStruct((MLA_HEADS * (MLA_V + ATTN_LROWS), T), BF16)],
        compiler_params=_cparams(("arbitrary",)),
        name="mla_up",
    )(z, z, z, cm, sm, g_cq.reshape(1, -1), g_ckv.reshape(1, -1), wq1, wq2, wk, wv)


def _attn_kernel(q_ref, k_ref, vt_ref, o_ref, *scr, bq, bk):
    nh = ATTN_HEADS
    slots = (scr[:nh], scr[nh:2 * nh])
    p_scrs, acc_scr, m_scr = scr[2 * nh:3 * nh], scr[3 * nh], scr[3 * nh + 1]
    mx_scrs = scr[3 * nh + 2:3 * nh + 4]
    qi = pl.program_id(2)
    vrows = MLA_V + ATTN_LROWS
    qs = [q_ref[:, h * LANES:(h + 1) * LANES] for h in range(nh)]
    acc_scr[...] = jnp.zeros(acc_scr.shape, F32)
    m_scr[...] = jnp.full(m_scr.shape, -jnp.inf, F32)
    sub8 = lax.broadcasted_iota(I32, (SUBLANES, bq), 0)
    lane8 = lax.broadcasted_iota(I32, (SUBLANES, bq), 1)
    pack = 2 * SUBLANES

    def scores(kb, slot):
        k0 = pl.multiple_of(kb * bk, bk)
        for h in range(nh):
            s = lax.dot_general(k_ref[pl.ds(k0, bk), h * LANES:(h + 1) * LANES], qs[h],
                                (((1,), (1,)), ((), ())), preferred_element_type=F32)
            slots[slot][h][...] = s
            mx_scrs[slot][h * SUBLANES:(h + 1) * SUBLANES, :] = jnp.max(
                s.reshape(bk // SUBLANES, SUBLANES, bq), axis=0)

    def update(kb, slot, masked):
        k0 = pl.multiple_of(kb * bk, bk)
        for h in range(nh):
            st, pr = slots[slot][h], p_scrs[h]
            if masked:
                parts = [None, None]
                for r in range(bk // SUBLANES):
                    rs = slice(r * SUBLANES, (r + 1) * SUBLANES)
                    x = jnp.where(lane8 >= sub8 + r * SUBLANES, st[rs, :], -jnp.inf)
                    st[rs, :] = x
                    parts[r % 2] = x if parts[r % 2] is None else jnp.maximum(parts[r % 2], x)
                part = jnp.maximum(parts[0], parts[1])
            else:
                part = mx_scrs[slot][h * SUBLANES:(h + 1) * SUBLANES, :]
            m_cur = jnp.max(part, axis=0, keepdims=True)
            m_prev = m_scr[h:h + 1, :]
            m_new = jnp.maximum(m_prev, m_cur)
            m_scr[h:h + 1, :] = m_new
            alpha = jnp.exp2(m_prev - m_new)
            for r in range(bk // pack):
                rs = slice(r * pack, (r + 1) * pack)
                pr[rs, :] = jnp.exp2(st[rs, :] - m_new).astype(BF16)
            rows = slice(h * vrows, (h + 1) * vrows)
            acc_scr[rows, :] = alpha * acc_scr[rows, :] + jnp.dot(
                vt_ref[rows, pl.ds(k0, bk)], pr[...], preferred_element_type=F32)

    scores(0, 0)
    npairs = qi // 2

    def pair(i, carry):
        kb = 2 * i
        scores(kb + 1, 1)
        update(kb, 0, False)
        scores(kb + 2, 0)
        update(kb + 1, 1, False)
        return carry

    lax.fori_loop(0, npairs, pair, 0)

    @pl.when(qi % 2 == 0)
    def _():
        update(qi, 0, True)

    @pl.when(qi % 2 == 1)
    def _():
        scores(qi, 1)
        update(qi - 1, 0, False)
        update(qi, 1, True)

    ot = jnp.concatenate(
        [acc_scr[h * vrows:h * vrows + MLA_V, :]
         / acc_scr[h * vrows + MLA_V:h * vrows + MLA_V + 1, :] for h in range(nh)], axis=0)
    o_ref[...] = ot.T.astype(o_ref.dtype)


def _attention(q, k, vt, B, S):
    T = q.shape[0]
    bq = min(ATTN_BQ, S)
    bk = min(ATTN_BK, S)
    nq = S // bq
    nh = ATTN_HEADS
    vrows = MLA_V + ATTN_LROWS
    assert bq == bk and nh <= SUBLANES
    kern = functools.partial(_attn_kernel, bq=bq, bk=bk)
    return pl.pallas_call(
        kern,
        grid=(B, MLA_HEADS // nh, nq),
        in_specs=[pl.BlockSpec((bq, nh * LANES), lambda b, p, i: (b * nq + i, p)),
                  pl.BlockSpec((S, nh * LANES), lambda b, p, i: (b, p)),
                  pl.BlockSpec((nh * vrows, S), lambda b, p, i: (p, b))],
        out_specs=pl.BlockSpec((bq, nh * MLA_V), lambda b, p, i: (b * nq + i, p)),
        out_shape=jax.ShapeDtypeStruct((T, MLA_HEADS * MLA_V), BF16),
        scratch_shapes=([pltpu.VMEM((bk, bq), F32)] * (2 * nh) + [pltpu.VMEM((bk, bq), BF16)] * nh
                        + [pltpu.VMEM((nh * vrows, bq), F32), pltpu.VMEM((SUBLANES, bq), F32)]
                        + [pltpu.VMEM((nh * SUBLANES, bq), F32)] * 2),
        compiler_params=_cparams(("arbitrary", "arbitrary", "arbitrary")),
        name="mla_attention",
    )(q, k, vt)


def _ret_kernel(rq_ref, rk_ref, rv_ref, rg_ref, cr_ref, sr_ref, dec_ref, xi_ref, zeta_ref,
                g_ref, o_ref, state_scr):
    @pl.when(pl.program_id(1) == 0)
    def _():
        state_scr[...] = jnp.zeros(state_scr.shape, F32)

    C = rq_ref.shape[0]
    cr = cr_ref[...]
    sr = sr_ref[...]
    half = RET_QK // 2
    for h in range(RET_HEADS):
        qk = slice(h * RET_QK, (h + 1) * RET_QK)
        vv = slice(h * RET_V, (h + 1) * RET_V)
        rq = rq_ref[:, qk].astype(F32)
        rk = rk_ref[:, qk].astype(F32)
        q = rq * cr + pltpu.roll(rq, half, 1) * sr
        k = (rk * cr + pltpu.roll(rk, half, 1) * sr) * (RET_QK ** -0.5)
        v = rv_ref[:, vv]
        xi = xi_ref[h]
        state = state_scr[h]
        qb = q.astype(BF16)
        s = lax.dot_general(qb, k.astype(BF16), (((1,), (1,)), ((), ())),
                            preferred_element_type=F32) * dec_ref[h]
        inner = jnp.dot(s.astype(BF16), v, preferred_element_type=F32)
        cross = jnp.dot(qb, state.astype(BF16), preferred_element_type=F32) * xi
        kz = (k * zeta_ref[h]).astype(BF16)
        kv = lax.dot_general(kz, v, (((0,), (0,)), ((), ())), preferred_element_type=F32)
        state_scr[h] = xi[C - 1:C, :] * state + kv
        o = inner + cross
        mu = jnp.mean(o, axis=-1, keepdims=True)
        d = o - mu
        var = jnp.mean(d * d, axis=-1, keepdims=True)
        on = d * lax.rsqrt(var + GN_EPS) * g_ref[:, vv]
        o_ref[:, vv] = (_silu(rg_ref[:, vv].astype(F32)) * on).astype(o_ref.dtype)


def _retention_tables():
    C = RET_BLOCK
    h = np.arange(RET_HEADS, dtype=np.float64)
    log_g = np.log(1.0 - np.exp2(-5.0 - h))
    idx = np.arange(C, dtype=np.float64)
    diff = idx[:, None] - idx[None, :]
    decay = np.where(diff[None] >= 0, np.exp(np.maximum(diff, 0.0)[None] * log_g[:, None, None]), 0.0)
    zeta = np.exp((C - 1 - idx)[None, :] * log_g[:, None])
    xi = np.exp((idx + 1.0)[None, :] * log_g[:, None])
    zeta_rep = np.broadcast_to(zeta[:, :, None], (RET_HEADS, C, RET_QK))
    xi_rep = np.broadcast_to(xi[:, :, None], (RET_HEADS, C, RET_V))
    return (jnp.asarray(decay, F32), jnp.asarray(xi_rep, F32), jnp.asarray(zeta_rep, F32))


def _retention(z, cr, sr, g_ret, B, S):
    T = z.shape[0]
    C = RET_BLOCK
    N = S // C
    H = RET_HEADS
    WQ, WV = H * RET_QK, H * RET_V
    dec, xi, zeta = _retention_tables()
    row = lambda b, n: b * N + n
    const = lambda shape: pl.BlockSpec(shape, lambda b, n: (0,) * len(shape))
    return pl.pallas_call(
        _ret_kernel,
        grid=(B, N),
        in_specs=[pl.BlockSpec((C, WQ), lambda b, n: (row(b, n), Z_RQ // WQ)),
                  pl.BlockSpec((C, WQ), lambda b, n: (row(b, n), Z_RK // WQ)),
                  pl.BlockSpec((C, WV), lambda b, n: (row(b, n), Z_RV // WV)),
                  pl.BlockSpec((C, WV), lambda b, n: (row(b, n), Z_RG // WV)),
                  pl.BlockSpec((C, LANES), lambda b, n: (row(b, n), 0)),
                  pl.BlockSpec((C, LANES), lambda b, n: (row(b, n), 0)),
                  const((H, C, C)), const((H, C, RET_V)), const((H, C, RET_QK)), const((1, WV))],
        out_specs=pl.BlockSpec((C, WV), lambda b, n: (row(b, n), 0)),
        out_shape=jax.ShapeDtypeStruct((T, WV), BF16),
        scratch_shapes=[pltpu.VMEM((H, RET_QK, RET_V), F32)],
        compiler_params=_cparams(("arbitrary", "arbitrary")),
        name="retention",
    )(z, z, z, z, cr, sr, dec, xi, zeta, g_ret.reshape(1, -1))


def _merge_kernel(x_ref, oa_ref, ob_ref, ga_ref, gb_ref, gt_ref, wa_ref, wb_ref, wo_ref, x1_ref):
    a = jnp.dot(oa_ref[...], wa_ref[...], preferred_element_type=F32)
    b = jnp.dot(ob_ref[...], wb_ref[...], preferred_element_type=F32)
    merged = (jax.nn.sigmoid(ga_ref[...].astype(F32)) * a
              + jax.nn.sigmoid(gb_ref[...].astype(F32)) * b)
    y = jnp.dot(merged.astype(BF16), wo_ref[...], preferred_element_type=F32)
    x1_ref[...] = x_ref[...] + gt_ref[...] * y


def _merge(x2, o_mla, o_ret, z, mod3, wa, wb, wo, S):
    T, D = x2.shape
    tm = 512
    per_b = S // tm
    full = lambda shape: pl.BlockSpec(shape, lambda i: (0, 0))
    return pl.pallas_call(
        _merge_kernel,
        grid=(T // tm,),
        in_specs=[pl.BlockSpec((tm, D), lambda i: (i, 0)),
                  pl.BlockSpec((tm, o_mla.shape[1]), lambda i: (i, 0)),
                  pl.BlockSpec((tm, D), lambda i: (i, 0)),
                  pl.BlockSpec((tm, D), lambda i: (i, Z_GA // D)),
                  pl.BlockSpec((tm, D), lambda i: (i, Z_GB // D)),
                  pl.BlockSpec((None, 1, D), lambda i: ((i // per_b) * 6 + 2, 0, 0)),
                  full(wa.shape), full(wb.shape), full(wo.shape)],
        out_specs=pl.BlockSpec((tm, D), lambda i: (i, 0)),
        out_shape=jax.ShapeDtypeStruct((T, D), F32),
        compiler_params=_cparams(("arbitrary",)),
        name="merge_out",
    )(x2, o_mla, o_ret, z, z, mod3, wa, wb, wo)


def _router_kernel(x1_ref, sc_ref, sh_ref, g_ref, wr_ref, br_ref,
                   h2_ref, wts_ref, pp_ref, off_ref, pc_ref):
    TB = x1_ref.shape[0]
    D = x1_ref.shape[1]
    E, G = N_EXPERTS, N_GROUPS
    per = E // G
    h2 = _rms(x1_ref[...]) * g_ref[...] * (1.0 + sc_ref[...]) + sh_ref[...]
    _store_row_tiles(h2_ref, h2)
    logits = lax.dot_general(wr_ref[...], h2, (((1,), (1,)), ((), ())),
                             precision=lax.Precision.HIGHEST,
                             preferred_element_type=F32)
    s = jax.nn.sigmoid(logits)
    biased = s + br_ref[...]
    sub = lax.broadcasted_iota(I32, (per, TB), 0)
    neg = -jnp.inf

    def first_argmax(vals, m, idx, sentinel):
        return jnp.min(jnp.where(vals == m, idx, sentinel), axis=0, keepdims=True)

    bg = [biased[g * per:(g + 1) * per, :] for g in range(G)]
    sg = [s[g * per:(g + 1) * per, :] for g in range(G)]
    gscore = []
    for g in range(G):
        m1 = jnp.max(bg[g], axis=0, keepdims=True)
        i1 = first_argmax(bg[g], m1, sub, per)
        m2 = jnp.max(jnp.where(sub == i1, neg, bg[g]), axis=0, keepdims=True)
        gscore.append(m1 + m2)
    gs = jnp.concatenate(gscore, axis=0)
    gidx = lax.broadcasted_iota(I32, (G, TB), 0)
    gsel = jnp.zeros((G, TB), F32)
    for _ in range(TOPK_GROUPS):
        m = jnp.max(gs, axis=0, keepdims=True)
        i = first_argmax(gs, m, gidx, G)
        hit = gidx == i
        gsel = jnp.where(hit, 1.0, gsel)
        gs = jnp.where(hit, neg, gs)
    cand = [jnp.where(gsel[g:g + 1, :] > 0.0, bg[g], neg) for g in range(G)]
    eidx = [sub + g * per for g in range(G)]
    sel = [jnp.zeros((per, TB), F32) for _ in range(G)]
    top_i, top_w = [], []
    for _ in range(TOP_K):
        m = functools.reduce(jnp.maximum, [jnp.max(c, axis=0, keepdims=True) for c in cand])
        i = functools.reduce(jnp.minimum,
                             [first_argmax(cand[g], m, eidx[g], E) for g in range(G)])
        w = jnp.zeros((1, TB), F32)
        for g in range(G):
            hit = eidx[g] == i
            w = w + jnp.sum(jnp.where(hit, sg[g], 0.0), axis=0, keepdims=True)
            sel[g] = jnp.where(hit, 1.0, sel[g])
            cand[g] = jnp.where(hit, neg, cand[g])
        top_i.append(i)
        top_w.append(w)
    wsum = functools.reduce(lambda a, b: a + b, top_w)
    wts_ref[...] = jnp.concatenate([w / wsum * ROUTED_SCALE for w in top_w], axis=0)

    mask = jnp.concatenate(sel, axis=0)
    t_row = lax.broadcasted_iota(I32, (TB, TB), 0)
    t_col = lax.broadcasted_iota(I32, (TB, TB), 1)
    before = jnp.where(t_row < t_col, 1.0, 0.0).astype(BF16)
    rank = jnp.dot(mask.astype(BF16), before, preferred_element_type=F32)
    cnt = jnp.sum(mask, axis=1, keepdims=True)
    pc_rep = jnp.broadcast_to(cnt, (E, LANES))
    e_row = lax.broadcasted_iota(I32, (E, E), 0)
    e_col = lax.broadcasted_iota(I32, (E, E), 1)
    lower = jnp.where(e_col < e_row, 1.0, 0.0)
    off_rep = jnp.dot(lower, pc_rep, precision=lax.Precision.HIGHEST,
                      preferred_element_type=F32)
    off_ref[...] = off_rep.astype(I32)
    pc_ref[...] = pc_rep.astype(I32)
    posfull = off_rep[:, :1] + rank
    pos = []
    for kk in range(TOP_K):
        p = jnp.zeros((1, TB), F32)
        for g in range(G):
            p = p + jnp.sum(jnp.where(eidx[g] == top_i[kk], posfull[g * per:(g + 1) * per, :], 0.0),
                            axis=0, keepdims=True)
        pos.append(p.astype(I32))
    pp_ref[...] = jnp.concatenate(pos, axis=0) * (D // LANES)


def _router(x1, mod3, g_norm2, w_router, b_router, S):
    T, D = x1.shape
    TB = min(MOE_TB, T)
    nb = T // TB
    per_b = S // TB
    E = N_EXPERTS
    return pl.pallas_call(
        _router_kernel,
        grid=(nb,),
        in_specs=[pl.BlockSpec((TB, D), lambda i: (i, 0)),
                  pl.BlockSpec((None, 1, D), lambda i: ((i // per_b) * 6 + 4, 0, 0)),
                  pl.BlockSpec((None, 1, D), lambda i: ((i // per_b) * 6 + 3, 0, 0)),
                  pl.BlockSpec((1, D), lambda i: (0, 0)),
                  pl.BlockSpec((E, D), lambda i: (0, 0)),
                  pl.BlockSpec((E, 1), lambda i: (0, 0))],
        out_specs=[pl.BlockSpec((TB * D // LANES, LANES), lambda i: (i, 0)),
                   pl.BlockSpec((TOP_K, TB), lambda i: (0, i)),
                   pl.BlockSpec((TOP_K, TB), lambda i: (0, i)),
                   pl.BlockSpec((E, LANES), lambda i: (i, 0)),
                   pl.BlockSpec((E, LANES), lambda i: (i, 0))],
        out_shape=[jax.ShapeDtypeStruct((T * D // LANES, LANES), F32),
                   jax.ShapeDtypeStruct((TOP_K, T), F32),
                   jax.ShapeDtypeStruct((TOP_K, T), I32),
                   jax.ShapeDtypeStruct((nb * E, LANES), I32),
                   jax.ShapeDtypeStruct((nb * E, LANES), I32)],
        compiler_params=_cparams(("arbitrary",)),
        name="moe_router",
    )(x1, mod3, mod3, g_norm2.reshape(1, D), w_router.T, b_router.reshape(E, 1))


def _moe_kernel(pp_sm, w_sm, off_sm, cnt_sm, h2_ref, wgu_ref, wd_ref, *rest, TB):
    out_ref, xs_scr = rest[-2:]
    j = pl.program_id(0)
    step = pl.program_id(1)
    E = N_EXPERTS
    D = wgu_ref.shape[1]
    RT = D // LANES
    CH = MOE_CH

    def tile(ref, first):
        return ref.at[pl.ds(pl.multiple_of(first, RT), RT), :]

    def positions(t):
        return [pp_sm[(j * TB + t) * TOP_K + k] for k in range(TOP_K)]

    @pl.when(step == 0)
    def _dispatch():
        xs_scr[TOP_K * TB * RT:, :] = jnp.zeros((2 * CH * RT, LANES), F32)

        def scatter(t, carry):
            row = tile(h2_ref, t * RT)[...]
            for p in positions(t):
                tile(xs_scr, p)[...] = row
            return carry

        lax.fori_loop(0, TB, scatter, 0, unroll=4)

    def run_rows(ee, r0, n_left, nrows):
        blk = xs_scr.at[pl.ds(pl.multiple_of(r0 * RT, RT), nrows * RT), :]
        xin = _load_row_tiles(blk, nrows)
        xb = xin.astype(BF16)
        gu = jnp.dot(xb, wgu_ref[ee], preferred_element_type=F32)
        a = _silu(gu[:, :D_EXPERT]) * gu[:, D_EXPERT:]
        y = jnp.dot(a.astype(BF16), wd_ref[ee], preferred_element_type=F32)
        rows = lax.broadcasted_iota(I32, (nrows, D), 0)
        _store_row_tiles(blk, jnp.where(rows < n_left, y, xin))

    for ee in range(MOE_EPS):
        e = step * MOE_EPS + ee
        st = off_sm[j * E + e]
        n = cnt_sm[j * E + e]
        nbig = n // (2 * CH)

        def big_body(i, carry, ee=ee, st=st, n=n):
            run_rows(ee, st + i * 2 * CH, n - i * 2 * CH, 2 * CH)
            return carry

        lax.fori_loop(0, nbig, big_body, 0)
        r1 = st + nbig * 2 * CH
        rem = n - nbig * 2 * CH

        @pl.when(rem > CH)
        def _(ee=ee, r1=r1, rem=rem):
            run_rows(ee, r1, rem, 2 * CH)

        @pl.when((rem > 0) & (rem <= CH))
        def _(ee=ee, r1=r1, rem=rem):
            run_rows(ee, r1, rem, CH)

    @pl.when(step == pl.num_programs(1) - 1)
    def _combine():
        def gather(t, carry):
            acc = None
            for k, p in enumerate(positions(t)):
                term = w_sm[(j * TB + t) * TOP_K + k] * tile(xs_scr, p)[...]
                acc = term if acc is None else acc + term
            tile(out_ref, t * RT)[...] = acc
            return carry

        lax.fori_loop(0, TB, gather, 0, unroll=4)


def _moe(h2, wts, pp, off, pc, wgu, wd, n_split):
    D = wgu.shape[1]
    RT = D // LANES
    T = h2.shape[0] // RT
    E = N_EXPERTS
    EPS = MOE_EPS
    Ts = T // n_split
    TB = min(MOE_TB, Ts)
    nb = Ts // TB
    rows = TOP_K * TB + 2 * MOE_CH
    routed = None
    for s in range(n_split):
        tok = slice(s * Ts, (s + 1) * Ts)
        blk = slice(s * nb * E, (s + 1) * nb * E)
        in_specs = [pl.BlockSpec((TB * RT, LANES), lambda j, e, *_, s=s: (s * nb + j, 0),
                                 pipeline_mode=pl.Buffered(1)),
                    pl.BlockSpec((EPS, D, 2 * D_EXPERT), lambda j, e, *_: (e, 0, 0)),
                    pl.BlockSpec((EPS, D_EXPERT, D), lambda j, e, *_: (e, 0, 0))]
        args = [pp[:, tok].T.reshape(-1), wts[:, tok].T.reshape(-1), off[blk], pc[blk],
                h2, wgu, wd]
        aliases = {}
        if routed is not None:
            in_specs.append(pl.BlockSpec(memory_space=pl.ANY))
            args.append(routed)
            aliases = {len(args) - 1: 0}
        grid_spec = pltpu.PrefetchScalarGridSpec(
            num_scalar_prefetch=4,
            grid=(nb, E // EPS),
            in_specs=in_specs,
            out_specs=pl.BlockSpec((TB * RT, LANES), lambda j, e, *_, s=s: (s * nb + j, 0),
                                   pipeline_mode=pl.Buffered(1)),
            scratch_shapes=[pltpu.VMEM((rows * RT, LANES), F32)],
        )
        routed = pl.pallas_call(
            functools.partial(_moe_kernel, TB=TB),
            grid_spec=grid_spec,
            out_shape=jax.ShapeDtypeStruct((T * RT, LANES), F32),
            input_output_aliases=aliases,
            compiler_params=_cparams(("arbitrary", "arbitrary"), MOE_VMEM_LIMIT),
            name="moe_experts",
        )(*args)
    return routed


def _final_kernel(x1_ref, routed_ref, h2_ref, gt_ref, wsg_ref, wsu_ref, wsd_ref, gf_ref, o_ref):
    tm = x1_ref.shape[0]
    hb = _load_row_tiles(h2_ref, tm).astype(BF16)
    g = jnp.dot(hb, wsg_ref[...], preferred_element_type=F32)
    u = jnp.dot(hb, wsu_ref[...], preferred_element_type=F32)
    shared = jnp.dot((_silu(g) * u).astype(BF16), wsd_ref[...], preferred_element_type=F32)
    xo = x1_ref[...] + gt_ref[...] * (_load_row_tiles(routed_ref, tm) + shared)
    o_ref[...] = _rms(xo) * gf_ref[...]


def _final(x1, routed, h2, mod3, wsg, wsu, wsd, g_final, S):
    T, D = x1.shape
    tm = 512
    per_b = S // tm
    full = lambda shape: pl.BlockSpec(shape, lambda i: (0, 0))
    return pl.pallas_call(
        _final_kernel,
        grid=(T // tm,),
        in_specs=[pl.BlockSpec((tm, D), lambda i: (i, 0)),
                  pl.BlockSpec((tm * D // LANES, LANES), lambda i: (i, 0)),
                  pl.BlockSpec((tm * D // LANES, LANES), lambda i: (i, 0)),
                  pl.BlockSpec((None, 1, D), lambda i: ((i // per_b) * 6 + 5, 0, 0)),
                  full(wsg.shape), full(wsu.shape), full(wsd.shape), full((1, D))],
        out_specs=pl.BlockSpec((tm, D), lambda i: (i, 0)),
        out_shape=jax.ShapeDtypeStruct((T, D), F32),
        compiler_params=_cparams(("arbitrary",)),
        name="final_out",
    )(x1, routed, h2, mod3, wsg, wsu, wsd, g_final.reshape(1, D))


def _pack_w_in_kernel(w_ref, o_ref):
    splits = [MLA_Q_RANK, MLA_KV_RANK, MLA_ROPE, RET_HEADS * RET_QK, RET_HEADS * RET_QK,
              RET_HEADS * RET_V, RET_HEADS * RET_V]
    gate_w = (w_ref.shape[1] - sum(splits)) // 2
    splits += [gate_w, gate_w]
    edges = [0] + [int(v) for v in np.cumsum(splits)]
    wcq, wckv, wkr, wrq, wrk, wrv, wrg, wga, wgb = [
        w_ref[:, edges[i]:edges[i + 1]] for i in range(len(splits))]
    rows = w_ref.shape[0]
    hm = MLA_ROPE // 2
    zl = jnp.zeros((rows, MLA_NOPE), F32)
    zr = jnp.zeros((rows, LANES - MLA_NOPE - MLA_ROPE), F32)
    pad = jnp.zeros((rows, Z_COLS - (Z_KR2 + LANES)), F32)
    w = jnp.concatenate([wrv, wrg, wga, wgb, wrq, wrk, wcq, wckv,
                         zl, wkr, zr, zl, wkr[:, hm:], wkr[:, :hm], zr, pad], axis=1)
    o_ref[...] = w.astype(BF16)


def _pack_w_in(w_in):
    D, n_in = w_in.shape
    tr = 256
    return pl.pallas_call(
        _pack_w_in_kernel,
        grid=(D // tr,),
        in_specs=[pl.BlockSpec((tr, n_in), lambda i: (i, 0))],
        out_specs=pl.BlockSpec((tr, Z_COLS), lambda i: (i, 0)),
        out_shape=jax.ShapeDtypeStruct((D, Z_COLS), BF16),
        compiler_params=_cparams(("arbitrary",)),
        name="pack_w_in",
    )(w_in)


def _pack_mla_weights(w_uq, w_ukv):
    H = MLA_HEADS
    hm = MLA_ROPE // 2
    wq = w_uq.reshape(MLA_Q_RANK, H, MLA_NOPE + MLA_ROPE)
    nope, pe = wq[..., :MLA_NOPE], wq[..., MLA_NOPE:]
    zpad = jnp.zeros((MLA_Q_RANK, H, LANES - MLA_NOPE - MLA_ROPE), w_uq.dtype)
    wq1 = jnp.concatenate([nope, pe, zpad], axis=-1).reshape(MLA_Q_RANK, H * LANES)
    wq2 = jnp.concatenate([jnp.zeros_like(nope), pe[..., hm:], pe[..., :hm], zpad],
                          axis=-1).reshape(MLA_Q_RANK, H * LANES)
    wkv = w_ukv.reshape(MLA_KV_RANK, H, MLA_NOPE + MLA_V)
    kn, vv = wkv[..., :MLA_NOPE], wkv[..., MLA_NOPE:]
    wk = jnp.concatenate([kn, jnp.zeros((MLA_KV_RANK, H, LANES - MLA_NOPE), w_ukv.dtype)],
                         axis=-1).reshape(MLA_KV_RANK, H * LANES)
    wv = vv.reshape(MLA_KV_RANK, H * MLA_V).T
    return wq1.astype(BF16), wq2.astype(BF16), wk.astype(BF16), wv.astype(BF16)


def kernel(x, c, positions, w_ada, b_ada, g_norm1, w_in, g_cq, w_uq, g_ckv, w_ukv, g_ret,
           w_o_mla, w_o_ret, w_out, g_norm2, w_router, b_router, w_exp_gate, w_exp_up,
           w_exp_down, w_sh_gate, w_sh_up, w_sh_down, g_final):
    B, S, D = x.shape
    T = B * S
    x2 = x.reshape(T, D)

    mod = _ada(c, w_ada, b_ada)
    mod3 = mod.reshape(B * 6, 1, D)
    cr, sr, cm, sm = _rope_tables(positions)

    z = _inproj(x2, mod3, g_norm1, _pack_w_in(w_in), S)
    wq1, wq2, wk, wv = _pack_mla_weights(w_uq, w_ukv)
    q, k, v = _mla_up(z, cm, sm, g_cq, g_ckv, wq1, wq2, wk, wv)
    o_mla = _attention(q, k, v, B, S)
    o_ret = _retention(z, cr, sr, g_ret, B, S)
    x1 = _merge(x2, o_mla, o_ret, z, mod3, w_o_mla.astype(BF16), w_o_ret.astype(BF16),
                w_out.astype(BF16), S)

    h2, wts, pp, off_rep, pc_rep = _router(x1, mod3, g_norm2, w_router, b_router, S)
    routed = _moe(h2, wts, pp, off_rep[:, 0], pc_rep[:, 0],
                  jnp.concatenate([w_exp_gate, w_exp_up], axis=2).astype(BF16),
                  w_exp_down.astype(BF16), n_split=B)
    out = _final(x1, routed, h2, mod3, w_sh_gate.astype(BF16), w_sh_up.astype(BF16),
                 w_sh_down.astype(BF16), g_final, S)
    return out.reshape(B, S, D)
```

```python
import functools

import numpy as np
import jax
import jax.numpy as jnp
from jax import lax
from jax.experimental import pallas as pl
from jax.experimental.pallas import tpu as pltpu

F32 = jnp.float32
BF16 = jnp.bfloat16
I32 = jnp.int32

MLA_HEADS = 8
MLA_Q_RANK = 384
MLA_KV_RANK = 256
MLA_NOPE = 64
MLA_ROPE = 32
MLA_V = 64
RET_HEADS = 4
RET_QK = 128
RET_V = 256
RET_BLOCK = 256
ROPE_THETA = 10000.0
N_EXPERTS = 64
TOP_K = 8
N_GROUPS = 8
TOPK_GROUPS = 4
D_EXPERT = 256
ROUTED_SCALE = 2.5
RMS_EPS = 1e-6
GN_EPS = 1e-5

LANES = 128
SUBLANES = 8
VMEM_LIMIT = 56 * 1024 * 1024

Z_RV, Z_RG, Z_GA, Z_GB = 0, 1024, 2048, 3072
Z_RQ, Z_RK = 4096, 4608
Z_CQKV = 5120
Z_KR1, Z_KR2 = 5760, 5888
Z_COLS = 6144

LOG2E = 1.4426950408889634

MOD_SH1, MOD_SC1, MOD_GT1, MOD_SH2, MOD_SC2, MOD_GT2 = range(6)
N_MOD = 6

ROW_TILE = 512
ROPE_TILE = 1024
INPROJ_TM, INPROJ_TN = 1024, 2048
WPACK_ROWS = 256

ATTN_BQ = 512
ATTN_BK = 512
ATTN_HEADS = 2
ATTN_LROWS = 16

MOE_TB = 1024
MOE_CH = 144
MOE_EPS = 4
MOE_VMEM_LIMIT = 58 * 1024 * 1024


def _cparams(sem, vmem_limit=VMEM_LIMIT):
    return pltpu.CompilerParams(dimension_semantics=sem, vmem_limit_bytes=vmem_limit)


def _mod_spec(D, per_b, which):
    return pl.BlockSpec((None, 1, D), lambda i, *_: ((i // per_b) * N_MOD + which, 0, 0))


def _rms(x):
    return x * lax.rsqrt(jnp.mean(x * x, axis=-1, keepdims=True) + RMS_EPS)


def _silu(x):
    return x * jax.nn.sigmoid(x)


def _load_row_tiles(ref, nrows):
    nchunk = ref.shape[0] // nrows
    return jnp.concatenate([ref[pl.ds(c, nrows, stride=nchunk), :] for c in range(nchunk)], axis=1)


def _store_row_tiles(ref, val):
    nrows, d = val.shape
    nchunk = d // LANES
    for c in range(nchunk):
        ref[pl.ds(c, nrows, stride=nchunk), :] = val[:, c * LANES:(c + 1) * LANES]


def _ada_kernel(c_ref, w_ref, b_ref, o_ref):
    c = c_ref[...]
    o_ref[...] = jnp.dot(_silu(c).astype(BF16), w_ref[...].astype(BF16),
                         preferred_element_type=F32) + b_ref[...]


def _ada(c, w_ada, b_ada):
    B, D = c.shape
    n_out = w_ada.shape[1]
    cp = jnp.zeros((SUBLANES, D), F32).at[:B].set(c)
    tn = D
    out = pl.pallas_call(
        _ada_kernel,
        grid=(n_out // tn,),
        in_specs=[pl.BlockSpec((SUBLANES, D), lambda j: (0, 0)),
                  pl.BlockSpec((D, tn), lambda j: (0, j)),
                  pl.BlockSpec((1, tn), lambda j: (0, j))],
        out_specs=pl.BlockSpec((SUBLANES, tn), lambda j: (0, j)),
        out_shape=jax.ShapeDtypeStruct((SUBLANES, n_out), F32),
        compiler_params=_cparams(("arbitrary",)),
        name="ada_mod",
    )(cp, w_ada, b_ada.reshape(1, n_out))
    return out[:B]


def _rope_kernel(pos_ref, inv_ref, cr_ref, sr_ref, cm_ref, sm_ref):
    ang = pos_ref[...].astype(F32) * inv_ref[...]
    c = jnp.cos(ang)
    s = jnp.sin(ang)
    lane = lax.broadcasted_iota(I32, c.shape, 1)
    half = RET_QK // 2
    cr_ref[...] = jnp.where(lane < half, c, pltpu.roll(c, half, 1))
    sr_ref[...] = jnp.where(lane < half, -s, pltpu.roll(s, half, 1))
    hm = MLA_ROPE // 2
    cm_ref[...] = jnp.where(lane < MLA_NOPE, 1.0,
                            jnp.where(lane < MLA_NOPE + hm, c,
                                      jnp.where(lane < MLA_NOPE + 2 * hm, pltpu.roll(c, hm, 1), 0.0)))
    sm_ref[...] = jnp.where(lane < MLA_NOPE, 0.0,
                            jnp.where(lane < MLA_NOPE + hm, -s,
                                      jnp.where(lane < MLA_NOPE + 2 * hm, pltpu.roll(s, hm, 1), 0.0)))


def _rope_tables(positions):
    T = positions.size
    tm = min(T, ROPE_TILE)
    inv_r = 1.0 / (ROPE_THETA ** (jnp.arange(0, RET_QK, 2, dtype=F32) / RET_QK))
    inv_m = 1.0 / (ROPE_THETA ** (jnp.arange(0, MLA_ROPE, 2, dtype=F32) / MLA_ROPE))
    inv = jnp.zeros((1, LANES), F32).at[0, :RET_QK // 2].set(inv_r)
    inv = inv.at[0, MLA_NOPE:MLA_NOPE + MLA_ROPE // 2].set(inv_m)
    tab = jax.ShapeDtypeStruct((T, LANES), F32)
    spec = pl.BlockSpec((tm, LANES), lambda i: (i, 0))
    return pl.pallas_call(
        _rope_kernel,
        grid=(T // tm,),
        in_specs=[pl.BlockSpec((tm, 1), lambda i: (i, 0)),
                  pl.BlockSpec((1, LANES), lambda i: (0, 0))],
        out_specs=[spec, spec, spec, spec],
        out_shape=[tab, tab, tab, tab],
        compiler_params=_cparams(("arbitrary",)),
        name="rope_tables",
    )(positions.reshape(T, 1), inv)


def _inproj_kernel(x_ref, sc_ref, sh_ref, g_ref, w_ref, z_ref, h_scr):
    @pl.when(pl.program_id(1) == 0)
    def _():
        h = _rms(x_ref[...]) * g_ref[...] * (1.0 + sc_ref[...]) + sh_ref[...]
        h_scr[...] = h.astype(BF16)

    z_ref[...] = jnp.dot(h_scr[...], w_ref[...], preferred_element_type=F32).astype(z_ref.dtype)


def _inproj(x2, mod3, g_norm1, w_pack, S):
    T, D = x2.shape
    N = w_pack.shape[1]
    tm, tn = INPROJ_TM, INPROJ_TN
    per_b = S // tm
    return pl.pallas_call(
        _inproj_kernel,
        grid=(T // tm, N // tn),
        in_specs=[pl.BlockSpec((tm, D), lambda i, j: (i, 0)),
                  _mod_spec(D, per_b, MOD_SC1), _mod_spec(D, per_b, MOD_SH1),
                  pl.BlockSpec((1, D), lambda i, j: (0, 0)),
                  pl.BlockSpec((D, tn), lambda i, j: (0, j))],
        out_specs=pl.BlockSpec((tm, tn), lambda i, j: (i, j)),
        out_shape=jax.ShapeDtypeStruct((T, N), BF16),
        scratch_shapes=[pltpu.VMEM((tm, D), BF16)],
        compiler_params=_cparams(("arbitrary", "arbitrary")),
        name="in_proj",
    )(x2, mod3, mod3, g_norm1.reshape(1, D), w_pack)


def _mla_up_kernel(zc_ref, kr1_ref, kr2_ref, cm_ref, sm_ref, gq_ref, gkv_ref,
                   wq1_ref, wq2_ref, wk_ref, wv_ref, q_ref, k_ref, v_ref):
    zc = zc_ref[...].astype(F32)
    cqn = (_rms(zc[:, :MLA_Q_RANK]) * gq_ref[...]).astype(BF16)
    ckvn = (_rms(zc[:, MLA_Q_RANK:]) * gkv_ref[...]).astype(BF16)
    cm = cm_ref[...]
    sm = sm_ref[...]
    q1 = jnp.dot(cqn, wq1_ref[...], preferred_element_type=F32)
    q2 = jnp.dot(cqn, wq2_ref[...], preferred_element_type=F32)
    kn = jnp.dot(ckvn, wk_ref[...], preferred_element_type=F32)
    kpe = kr1_ref[...].astype(F32) * cm + kr2_ref[...].astype(F32) * sm
    qscale = (MLA_NOPE + MLA_ROPE) ** -0.5 * LOG2E
    for h in range(MLA_HEADS):
        sl = slice(h * LANES, (h + 1) * LANES)
        q_ref[:, sl] = ((q1[:, sl] * cm + q2[:, sl] * sm) * qscale).astype(BF16)
        k_ref[:, sl] = (kn[:, sl] + kpe).astype(BF16)
    vt = lax.dot_general(wv_ref[...], ckvn, (((1,), (1,)), ((), ())),
                         preferred_element_type=F32).astype(BF16)
    vrows = MLA_V + ATTN_LROWS
    for h in range(MLA_HEADS):
        v_ref[h * vrows:h * vrows + MLA_V, :] = vt[h * MLA_V:(h + 1) * MLA_V, :]
        v_ref[h * vrows + MLA_V:(h + 1) * vrows, :] = jnp.ones((ATTN_LROWS, vt.shape[1]), BF16)


def _mla_up(z, cm, sm, g_cq, g_ckv, wq1, wq2, wk, wv):
    T = z.shape[0]
    tm = ROW_TILE
    HW = MLA_HEADS * LANES
    wc = MLA_Q_RANK + MLA_KV_RANK
    full = lambda shape: pl.BlockSpec(shape, lambda i: (0, 0))
    return pl.pallas_call(
        _mla_up_kernel,
        grid=(T // tm,),
        in_specs=[pl.BlockSpec((tm, wc), lambda i: (i, Z_CQKV // wc)),
                  pl.BlockSpec((tm, LANES), lambda i: (i, Z_KR1 // LANES)),
                  pl.BlockSpec((tm, LANES), lambda i: (i, Z_KR2 // LANES)),
                  pl.BlockSpec((tm, LANES), lambda i: (i, 0)),
                  pl.BlockSpec((tm, LANES), lambda i: (i, 0)),
                  full((1, MLA_Q_RANK)), full((1, MLA_KV_RANK)),
                  full(wq1.shape), full(wq2.shape), full(wk.shape), full(wv.shape)],
        out_specs=[pl.BlockSpec((tm, HW), lambda i: (i, 0)),
                   pl.BlockSpec((tm, HW), lambda i: (i, 0)),
                   pl.BlockSpec((MLA_HEADS * (MLA_V + ATTN_LROWS), tm), lambda i: (0, i))],
        out_shape=[jax.ShapeDtypeStruct((T, HW), BF16),
                   jax.ShapeDtypeStruct((T, HW), BF16),
                   jax.ShapeDtypeStruct((MLA_HEADS * (MLA_V + ATTN_LROWS), T), BF16)],
        compiler_params=_cparams(("arbitrary",)),
        name="mla_up",
    )(z, z, z, cm, sm, g_cq.reshape(1, -1), g_ckv.reshape(1, -1), wq1, wq2, wk, wv)


def _attn_kernel(q_ref, k_ref, vt_ref, o_ref, *scr, bq, bk):
    nh = ATTN_HEADS
    slots = (scr[:nh], scr[nh:2 * nh])
    p_scrs, acc_scr, m_scr = scr[2 * nh:3 * nh], scr[3 * nh], scr[3 * nh + 1]
    mx_scrs = scr[3 * nh + 2:3 * nh + 4]
    qi = pl.program_id(2)
    vrows = MLA_V + ATTN_LROWS
    qs = [q_ref[:, h * LANES:(h + 1) * LANES] for h in range(nh)]
    acc_scr[...] = jnp.zeros(acc_scr.shape, F32)
    m_scr[...] = jnp.full(m_scr.shape, -jnp.inf, F32)
    sub8 = lax.broadcasted_iota(I32, (SUBLANES, bq), 0)
    lane8 = lax.broadcasted_iota(I32, (SUBLANES, bq), 1)
    pack = 2 * SUBLANES

    def scores(kb, slot):
        k0 = pl.multiple_of(kb * bk, bk)
        for h in range(nh):
            s = lax.dot_general(k_ref[pl.ds(k0, bk), h * LANES:(h + 1) * LANES], qs[h],
                                (((1,), (1,)), ((), ())), preferred_element_type=F32)
            slots[slot][h][...] = s
            mx_scrs[slot][h * SUBLANES:(h + 1) * SUBLANES, :] = jnp.max(
                s.reshape(bk // SUBLANES, SUBLANES, bq), axis=0)

    def update(kb, slot, masked):
        k0 = pl.multiple_of(kb * bk, bk)
        for h in range(nh):
            st, pr = slots[slot][h], p_scrs[h]
            if masked:
                parts = [None, None]
                for r in range(bk // SUBLANES):
                    rs = slice(r * SUBLANES, (r + 1) * SUBLANES)
                    x = jnp.where(lane8 >= sub8 + r * SUBLANES, st[rs, :], -jnp.inf)
                    st[rs, :] = x
                    parts[r % 2] = x if parts[r % 2] is None else jnp.maximum(parts[r % 2], x)
                part = jnp.maximum(parts[0], parts[1])
            else:
                part = mx_scrs[slot][h * SUBLANES:(h + 1) * SUBLANES, :]
            m_cur = jnp.max(part, axis=0, keepdims=True)
            m_prev = m_scr[h:h + 1, :]
            m_new = jnp.maximum(m_prev, m_cur)
            m_scr[h:h + 1, :] = m_new
            alpha = jnp.exp2(m_prev - m_new)
            for r in range(bk // pack):
                rs = slice(r * pack, (r + 1) * pack)
                pr[rs, :] = jnp.exp2(st[rs, :] - m_new).astype(BF16)
            rows = slice(h * vrows, (h + 1) * vrows)
            acc_scr[rows, :] = alpha * acc_scr[rows, :] + jnp.dot(
                vt_ref[rows, pl.ds(k0, bk)], pr[...], preferred_element_type=F32)

    scores(0, 0)
    npairs = qi // 2

    def pair(i, carry):
        kb = 2 * i
        scores(kb + 1, 1)
        update(kb, 0, False)
        scores(kb + 2, 0)
        update(kb + 1, 1, False)
        return carry

    lax.fori_loop(0, npairs, pair, 0)

    @pl.when(qi % 2 == 0)
    def _():
        update(qi, 0, True)

    @pl.when(qi % 2 == 1)
    def _():
        scores(qi, 1)
        update(qi - 1, 0, False)
        update(qi, 1, True)

    ot = jnp.concatenate(
        [acc_scr[h * vrows:h * vrows + MLA_V, :]
         / acc_scr[h * vrows + MLA_V:h * vrows + MLA_V + 1, :] for h in range(nh)], axis=0)
    o_ref[...] = ot.T.astype(o_ref.dtype)


def _attention(q, k, vt, B, S):
    T = q.shape[0]
    bq = min(ATTN_BQ, S)
    bk = min(ATTN_BK, S)
    nq = S // bq
    nh = ATTN_HEADS
    vrows = MLA_V + ATTN_LROWS
    assert bq == bk and nh <= SUBLANES
    kern = functools.partial(_attn_kernel, bq=bq, bk=bk)
    return pl.pallas_call(
        kern,
        grid=(B, MLA_HEADS // nh, nq),
        in_specs=[pl.BlockSpec((bq, nh * LANES), lambda b, p, i: (b * nq + i, p)),
                  pl.BlockSpec((S, nh * LANES), lambda b, p, i: (b, p)),
                  pl.BlockSpec((nh * vrows, S), lambda b, p, i: (p, b))],
        out_specs=pl.BlockSpec((bq, nh * MLA_V), lambda b, p, i: (b * nq + i, p)),
        out_shape=jax.ShapeDtypeStruct((T, MLA_HEADS * MLA_V), BF16),
        scratch_shapes=([pltpu.VMEM((bk, bq), F32)] * (2 * nh) + [pltpu.VMEM((bk, bq), BF16)] * nh
                        + [pltpu.VMEM((nh * vrows, bq), F32), pltpu.VMEM((SUBLANES, bq), F32)]
                        + [pltpu.VMEM((nh * SUBLANES, bq), F32)] * 2),
        compiler_params=_cparams(("arbitrary", "arbitrary", "arbitrary")),
        name="mla_attention",
    )(q, k, vt)


def _ret_kernel(rq_ref, rk_ref, rv_ref, rg_ref, cr_ref, sr_ref, dec_ref, xi_ref, zeta_ref,
                g_ref, o_ref, state_scr):
    @pl.when(pl.program_id(1) == 0)
    def _():
        state_scr[...] = jnp.zeros(state_scr.shape, F32)

    C = rq_ref.shape[0]
    cr = cr_ref[...]
    sr = sr_ref[...]
    half = RET_QK // 2
    for h in range(RET_HEADS):
        qk = slice(h * RET_QK, (h + 1) * RET_QK)
        vv = slice(h * RET_V, (h + 1) * RET_V)
        rq = rq_ref[:, qk].astype(F32)
        rk = rk_ref[:, qk].astype(F32)
        q = rq * cr + pltpu.roll(rq, half, 1) * sr
        k = (rk * cr + pltpu.roll(rk, half, 1) * sr) * (RET_QK ** -0.5)
        v = rv_ref[:, vv]
        xi = xi_ref[h]
        state = state_scr[h]
        qb = q.astype(BF16)
        s = lax.dot_general(qb, k.astype(BF16), (((1,), (1,)), ((), ())),
                            preferred_element_type=F32) * dec_ref[h]
        inner = jnp.dot(s.astype(BF16), v, preferred_element_type=F32)
        cross = jnp.dot(qb, state.astype(BF16), preferred_element_type=F32) * xi
        kz = (k * zeta_ref[h]).astype(BF16)
        kv = lax.dot_general(kz, v, (((0,), (0,)), ((), ())), preferred_element_type=F32)
        state_scr[h] = xi[C - 1:C, :] * state + kv
        o = inner + cross
        mu = jnp.mean(o, axis=-1, keepdims=True)
        d = o - mu
        var = jnp.mean(d * d, axis=-1, keepdims=True)
        on = d * lax.rsqrt(var + GN_EPS) * g_ref[:, vv]
        o_ref[:, vv] = (_silu(rg_ref[:, vv].astype(F32)) * on).astype(o_ref.dtype)


def _retention_tables():
    C = RET_BLOCK
    h = np.arange(RET_HEADS, dtype=np.float64)
    log_g = np.log(1.0 - np.exp2(-5.0 - h))
    idx = np.arange(C, dtype=np.float64)
    diff = idx[:, None] - idx[None, :]
    decay = np.where(diff[None] >= 0, np.exp(np.maximum(diff, 0.0)[None] * log_g[:, None, None]), 0.0)
    zeta = np.exp((C - 1 - idx)[None, :] * log_g[:, None])
    xi = np.exp((idx + 1.0)[None, :] * log_g[:, None])
    zeta_rep = np.broadcast_to(zeta[:, :, None], (RET_HEADS, C, RET_QK))
    xi_rep = np.broadcast_to(xi[:, :, None], (RET_HEADS, C, RET_V))
    return (jnp.asarray(decay, F32), jnp.asarray(xi_rep, F32), jnp.asarray(zeta_rep, F32))


def _retention(z, cr, sr, g_ret, B, S):
    T = z.shape[0]
    C = RET_BLOCK
    N = S // C
    H = RET_HEADS
    WQ, WV = H * RET_QK, H * RET_V
    dec, xi, zeta = _retention_tables()
    row = lambda b, n: b * N + n
    const = lambda shape: pl.BlockSpec(shape, lambda b, n: (0,) * len(shape))
    return pl.pallas_call(
        _ret_kernel,
        grid=(B, N),
        in_specs=[pl.BlockSpec((C, WQ), lambda b, n: (row(b, n), Z_RQ // WQ)),
                  pl.BlockSpec((C, WQ), lambda b, n: (row(b, n), Z_RK // WQ)),
                  pl.BlockSpec((C, WV), lambda b, n: (row(b, n), Z_RV // WV)),
                  pl.BlockSpec((C, WV), lambda b, n: (row(b, n), Z_RG // WV)),
                  pl.BlockSpec((C, LANES), lambda b, n: (row(b, n), 0)),
                  pl.BlockSpec((C, LANES), lambda b, n: (row(b, n), 0)),
                  const((H, C, C)), const((H, C, RET_V)), const((H, C, RET_QK)), const((1, WV))],
        out_specs=pl.BlockSpec((C, WV), lambda b, n: (row(b, n), 0)),
        out_shape=jax.ShapeDtypeStruct((T, WV), BF16),
        scratch_shapes=[pltpu.VMEM((H, RET_QK, RET_V), F32)],
        compiler_params=_cparams(("arbitrary", "arbitrary")),
        name="retention",
    )(z, z, z, z, cr, sr, dec, xi, zeta, g_ret.reshape(1, -1))


def _merge_kernel(x_ref, oa_ref, ob_ref, ga_ref, gb_ref, gt_ref, wa_ref, wb_ref, wo_ref, x1_ref):
    a = jnp.dot(oa_ref[...], wa_ref[...], preferred_element_type=F32)
    b = jnp.dot(ob_ref[...], wb_ref[...], preferred_element_type=F32)
    merged = (jax.nn.sigmoid(ga_ref[...].astype(F32)) * a
              + jax.nn.sigmoid(gb_ref[...].astype(F32)) * b)
    y = jnp.dot(merged.astype(BF16), wo_ref[...], preferred_element_type=F32)
    x1_ref[...] = x_ref[...] + gt_ref[...] * y


def _merge(x2, o_mla, o_ret, z, mod3, wa, wb, wo, S):
    T, D = x2.shape
    tm = ROW_TILE
    per_b = S // tm
    full = lambda shape: pl.BlockSpec(shape, lambda i: (0, 0))
    return pl.pallas_call(
        _merge_kernel,
        grid=(T // tm,),
        in_specs=[pl.BlockSpec((tm, D), lambda i: (i, 0)),
                  pl.BlockSpec((tm, o_mla.shape[1]), lambda i: (i, 0)),
                  pl.BlockSpec((tm, D), lambda i: (i, 0)),
                  pl.BlockSpec((tm, D), lambda i: (i, Z_GA // D)),
                  pl.BlockSpec((tm, D), lambda i: (i, Z_GB // D)),
                  _mod_spec(D, per_b, MOD_GT1),
                  full(wa.shape), full(wb.shape), full(wo.shape)],
        out_specs=pl.BlockSpec((tm, D), lambda i: (i, 0)),
        out_shape=jax.ShapeDtypeStruct((T, D), F32),
        compiler_params=_cparams(("arbitrary",)),
        name="merge_out",
    )(x2, o_mla, o_ret, z, z, mod3, wa, wb, wo)


def _router_kernel(x1_ref, sc_ref, sh_ref, g_ref, wr_ref, br_ref,
                   h2_ref, wts_ref, pp_ref, off_ref, pc_ref):
    TB = x1_ref.shape[0]
    D = x1_ref.shape[1]
    E, G = N_EXPERTS, N_GROUPS
    per = E // G
    h2 = _rms(x1_ref[...]) * g_ref[...] * (1.0 + sc_ref[...]) + sh_ref[...]
    _store_row_tiles(h2_ref, h2)
    logits = lax.dot_general(wr_ref[...], h2, (((1,), (1,)), ((), ())),
                             precision=lax.Precision.HIGHEST,
                             preferred_element_type=F32)
    s = jax.nn.sigmoid(logits)
    biased = s + br_ref[...]
    sub = lax.broadcasted_iota(I32, (per, TB), 0)
    neg = -jnp.inf

    def first_argmax(vals, m, idx, sentinel):
        return jnp.min(jnp.where(vals == m, idx, sentinel), axis=0, keepdims=True)

    bg = [biased[g * per:(g + 1) * per, :] for g in range(G)]
    sg = [s[g * per:(g + 1) * per, :] for g in range(G)]
    gscore = []
    for g in range(G):
        m1 = jnp.max(bg[g], axis=0, keepdims=True)
        i1 = first_argmax(bg[g], m1, sub, per)
        m2 = jnp.max(jnp.where(sub == i1, neg, bg[g]), axis=0, keepdims=True)
        gscore.append(m1 + m2)
    gs = jnp.concatenate(gscore, axis=0)
    gidx = lax.broadcasted_iota(I32, (G, TB), 0)
    gsel = jnp.zeros((G, TB), F32)
    for _ in range(TOPK_GROUPS):
        m = jnp.max(gs, axis=0, keepdims=True)
        i = first_argmax(gs, m, gidx, G)
        hit = gidx == i
        gsel = jnp.where(hit, 1.0, gsel)
        gs = jnp.where(hit, neg, gs)
    cand = [jnp.where(gsel[g:g + 1, :] > 0.0, bg[g], neg) for g in range(G)]
    eidx = [sub + g * per for g in range(G)]
    sel = [jnp.zeros((per, TB), F32) for _ in range(G)]
    top_i, top_w = [], []
    for _ in range(TOP_K):
        m = functools.reduce(jnp.maximum, [jnp.max(c, axis=0, keepdims=True) for c in cand])
        i = functools.reduce(jnp.minimum,
                             [first_argmax(cand[g], m, eidx[g], E) for g in range(G)])
        w = jnp.zeros((1, TB), F32)
        for g in range(G):
            hit = eidx[g] == i
            w = w + jnp.sum(jnp.where(hit, sg[g], 0.0), axis=0, keepdims=True)
            sel[g] = jnp.where(hit, 1.0, sel[g])
            cand[g] = jnp.where(hit, neg, cand[g])
        top_i.append(i)
        top_w.append(w)
    wsum = functools.reduce(lambda a, b: a + b, top_w)
    wts_ref[...] = jnp.concatenate([w / wsum * ROUTED_SCALE for w in top_w], axis=0)

    mask = jnp.concatenate(sel, axis=0)
    t_row = lax.broadcasted_iota(I32, (TB, TB), 0)
    t_col = lax.broadcasted_iota(I32, (TB, TB), 1)
    before = jnp.where(t_row < t_col, 1.0, 0.0).astype(BF16)
    rank = jnp.dot(mask.astype(BF16), before, preferred_element_type=F32)
    cnt = jnp.sum(mask, axis=1, keepdims=True)
    pc_rep = jnp.broadcast_to(cnt, (E, LANES))
    e_row = lax.broadcasted_iota(I32, (E, E), 0)
    e_col = lax.broadcasted_iota(I32, (E, E), 1)
    lower = jnp.where(e_col < e_row, 1.0, 0.0)
    off_rep = jnp.dot(lower, pc_rep, precision=lax.Precision.HIGHEST,
                      preferred_element_type=F32)
    off_ref[...] = off_rep.astype(I32)
    pc_ref[...] = pc_rep.astype(I32)
    posfull = off_rep[:, :1] + rank
    pos = []
    for kk in range(TOP_K):
        p = jnp.zeros((1, TB), F32)
        for g in range(G):
            p = p + jnp.sum(jnp.where(eidx[g] == top_i[kk], posfull[g * per:(g + 1) * per, :], 0.0),
                            axis=0, keepdims=True)
        pos.append(p.astype(I32))
    pp_ref[...] = jnp.concatenate(pos, axis=0) * (D // LANES)


def _router(x1, mod3, g_norm2, w_router, b_router, S):
    T, D = x1.shape
    TB = min(MOE_TB, T)
    nb = T // TB
    per_b = S // TB
    E = N_EXPERTS
    return pl.pallas_call(
        _router_kernel,
        grid=(nb,),
        in_specs=[pl.BlockSpec((TB, D), lambda i: (i, 0)),
                  _mod_spec(D, per_b, MOD_SC2), _mod_spec(D, per_b, MOD_SH2),
                  pl.BlockSpec((1, D), lambda i: (0, 0)),
                  pl.BlockSpec((E, D), lambda i: (0, 0)),
                  pl.BlockSpec((E, 1), lambda i: (0, 0))],
        out_specs=[pl.BlockSpec((TB * D // LANES, LANES), lambda i: (i, 0)),
                   pl.BlockSpec((TOP_K, TB), lambda i: (0, i)),
                   pl.BlockSpec((TOP_K, TB), lambda i: (0, i)),
                   pl.BlockSpec((E, LANES), lambda i: (i, 0)),
                   pl.BlockSpec((E, LANES), lambda i: (i, 0))],
        out_shape=[jax.ShapeDtypeStruct((T * D // LANES, LANES), F32),
                   jax.ShapeDtypeStruct((TOP_K, T), F32),
                   jax.ShapeDtypeStruct((TOP_K, T), I32),
                   jax.ShapeDtypeStruct((nb * E, LANES), I32),
                   jax.ShapeDtypeStruct((nb * E, LANES), I32)],
        compiler_params=_cparams(("arbitrary",)),
        name="moe_router",
    )(x1, mod3, mod3, g_norm2.reshape(1, D), w_router.T, b_router.reshape(E, 1))


def _moe_kernel(pp_sm, w_sm, off_sm, cnt_sm, h2_ref, wg_ref, wu_ref, wd_ref, *rest, TB):
    out_ref, xs_scr = rest[-2:]
    j = pl.program_id(0)
    step = pl.program_id(1)
    E = N_EXPERTS
    D = wg_ref.shape[1]
    RT = D // LANES
    CH = MOE_CH

    def tile(ref, first):
        return ref.at[pl.ds(pl.multiple_of(first, RT), RT), :]

    def positions(t):
        return [pp_sm[(j * TB + t) * TOP_K + k] for k in range(TOP_K)]

    @pl.when(step == 0)
    def _dispatch():
        xs_scr[TOP_K * TB * RT:, :] = jnp.zeros((2 * CH * RT, LANES), F32)

        def scatter(t, carry):
            row = tile(h2_ref, t * RT)[...]
            for p in positions(t):
                tile(xs_scr, p)[...] = row
            return carry

        lax.fori_loop(0, TB, scatter, 0, unroll=4)

    def run_rows(ee, r0, n_left, nrows):
        blk = xs_scr.at[pl.ds(pl.multiple_of(r0 * RT, RT), nrows * RT), :]
        xin = _load_row_tiles(blk, nrows)
        xb = xin.astype(BF16)
        g = jnp.dot(xb, wg_ref[ee], preferred_element_type=F32)
        u = jnp.dot(xb, wu_ref[ee], preferred_element_type=F32)
        y = jnp.dot((_silu(g) * u).astype(BF16), wd_ref[ee], preferred_element_type=F32)
        rows = lax.broadcasted_iota(I32, (nrows, D), 0)
        _store_row_tiles(blk, jnp.where(rows < n_left, y, xin))

    for ee in range(MOE_EPS):
        e = step * MOE_EPS + ee
        st = off_sm[j * E + e]
        n = cnt_sm[j * E + e]
        nbig = n // (2 * CH)

        def big_body(i, carry, ee=ee, st=st, n=n):
            run_rows(ee, st + i * 2 * CH, n - i * 2 * CH, 2 * CH)
            return carry

        lax.fori_loop(0, nbig, big_body, 0)
        r1 = st + nbig * 2 * CH
        rem = n - nbig * 2 * CH

        @pl.when(rem > CH)
        def _(ee=ee, r1=r1, rem=rem):
            run_rows(ee, r1, rem, 2 * CH)

        @pl.when((rem > 0) & (rem <= CH))
        def _(ee=ee, r1=r1, rem=rem):
            run_rows(ee, r1, rem, CH)

    @pl.when(step == pl.num_programs(1) - 1)
    def _combine():
        def gather(t, carry):
            acc = None
            for k, p in enumerate(positions(t)):
                term = w_sm[(j * TB + t) * TOP_K + k] * tile(xs_scr, p)[...]
                acc = term if acc is None else acc + term
            tile(out_ref, t * RT)[...] = acc
            return carry

        lax.fori_loop(0, TB, gather, 0, unroll=4)


def _moe(h2, wts, pp, off, pc, wg, wu, wd, n_split):
    D = wg.shape[1]
    RT = D // LANES
    T = h2.shape[0] // RT
    E = N_EXPERTS
    EPS = MOE_EPS
    Ts = T // n_split
    TB = min(MOE_TB, Ts)
    nb = Ts // TB
    rows = TOP_K * TB + 2 * MOE_CH
    routed = None
    for s in range(n_split):
        tok = slice(s * Ts, (s + 1) * Ts)
        blk = slice(s * nb * E, (s + 1) * nb * E)
        in_specs = [pl.BlockSpec((TB * RT, LANES), lambda j, e, *_, s=s: (s * nb + j, 0),
                                 pipeline_mode=pl.Buffered(1)),
                    pl.BlockSpec((EPS, D, D_EXPERT), lambda j, e, *_: (e, 0, 0)),
                    pl.BlockSpec((EPS, D, D_EXPERT), lambda j, e, *_: (e, 0, 0)),
                    pl.BlockSpec((EPS, D_EXPERT, D), lambda j, e, *_: (e, 0, 0))]
        args = [pp[:, tok].T.reshape(-1), wts[:, tok].T.reshape(-1), off[blk], pc[blk],
                h2, wg, wu, wd]
        aliases = {}
        if routed is not None:
            in_specs.append(pl.BlockSpec(memory_space=pl.ANY))
            args.append(routed)
            aliases = {len(args) - 1: 0}
        grid_spec = pltpu.PrefetchScalarGridSpec(
            num_scalar_prefetch=4,
            grid=(nb, E // EPS),
            in_specs=in_specs,
            out_specs=pl.BlockSpec((TB * RT, LANES), lambda j, e, *_, s=s: (s * nb + j, 0),
                                   pipeline_mode=pl.Buffered(1)),
            scratch_shapes=[pltpu.VMEM((rows * RT, LANES), F32)],
        )
        routed = pl.pallas_call(
            functools.partial(_moe_kernel, TB=TB),
            grid_spec=grid_spec,
            out_shape=jax.ShapeDtypeStruct((T * RT, LANES), F32),
            input_output_aliases=aliases,
            compiler_params=_cparams(("arbitrary", "arbitrary"), MOE_VMEM_LIMIT),
            name="moe_experts",
        )(*args)
    return routed


def _final_kernel(x1_ref, routed_ref, h2_ref, gt_ref, wsg_ref, wsu_ref, wsd_ref, gf_ref, o_ref):
    tm = x1_ref.shape[0]
    hb = _load_row_tiles(h2_ref, tm).astype(BF16)
    g = jnp.dot(hb, wsg_ref[...], preferred_element_type=F32)
    u = jnp.dot(hb, wsu_ref[...], preferred_element_type=F32)
    shared = jnp.dot((_silu(g) * u).astype(BF16), wsd_ref[...], preferred_element_type=F32)
    xo = x1_ref[...] + gt_ref[...] * (_load_row_tiles(routed_ref, tm) + shared)
    o_ref[...] = _rms(xo) * gf_ref[...]


def _final(x1, routed, h2, mod3, wsg, wsu, wsd, g_final, S):
    T, D = x1.shape
    tm = ROW_TILE
    per_b = S // tm
    full = lambda shape: pl.BlockSpec(shape, lambda i: (0, 0))
    return pl.pallas_call(
        _final_kernel,
        grid=(T // tm,),
        in_specs=[pl.BlockSpec((tm, D), lambda i: (i, 0)),
                  pl.BlockSpec((tm * D // LANES, LANES), lambda i: (i, 0)),
                  pl.BlockSpec((tm * D // LANES, LANES), lambda i: (i, 0)),
                  _mod_spec(D, per_b, MOD_GT2),
                  full(wsg.shape), full(wsu.shape), full(wsd.shape), full((1, D))],
        out_specs=pl.BlockSpec((tm, D), lambda i: (i, 0)),
        out_shape=jax.ShapeDtypeStruct((T, D), F32),
        compiler_params=_cparams(("arbitrary",)),
        name="final_out",
    )(x1, routed, h2, mod3, wsg, wsu, wsd, g_final.reshape(1, D))


def _pack_w_in_kernel(w_ref, o_ref):
    splits = [MLA_Q_RANK, MLA_KV_RANK, MLA_ROPE, RET_HEADS * RET_QK, RET_HEADS * RET_QK,
              RET_HEADS * RET_V, RET_HEADS * RET_V]
    gate_w = (w_ref.shape[1] - sum(splits)) // 2
    splits += [gate_w, gate_w]
    edges = [0] + [int(v) for v in np.cumsum(splits)]
    wcq, wckv, wkr, wrq, wrk, wrv, wrg, wga, wgb = [
        w_ref[:, edges[i]:edges[i + 1]] for i in range(len(splits))]
    rows = w_ref.shape[0]
    hm = MLA_ROPE // 2
    zl = jnp.zeros((rows, MLA_NOPE), F32)
    zr = jnp.zeros((rows, LANES - MLA_NOPE - MLA_ROPE), F32)
    pad = jnp.zeros((rows, Z_COLS - (Z_KR2 + LANES)), F32)
    w = jnp.concatenate([wrv, wrg, wga, wgb, wrq, wrk, wcq, wckv,
                         zl, wkr, zr, zl, wkr[:, hm:], wkr[:, :hm], zr, pad], axis=1)
    o_ref[...] = w.astype(BF16)


def _pack_w_in(w_in):
    D, n_in = w_in.shape
    tr = WPACK_ROWS
    return pl.pallas_call(
        _pack_w_in_kernel,
        grid=(D // tr,),
        in_specs=[pl.BlockSpec((tr, n_in), lambda i: (i, 0))],
        out_specs=pl.BlockSpec((tr, Z_COLS), lambda i: (i, 0)),
        out_shape=jax.ShapeDtypeStruct((D, Z_COLS), BF16),
        compiler_params=_cparams(("arbitrary",)),
        name="pack_w_in",
    )(w_in)


def _pack_mla_weights(w_uq, w_ukv):
    H = MLA_HEADS
    hm = MLA_ROPE // 2
    wq = w_uq.reshape(MLA_Q_RANK, H, MLA_NOPE + MLA_ROPE)
    nope, pe = wq[..., :MLA_NOPE], wq[..., MLA_NOPE:]
    zpad = jnp.zeros((MLA_Q_RANK, H, LANES - MLA_NOPE - MLA_ROPE), w_uq.dtype)
    wq1 = jnp.concatenate([nope, pe, zpad], axis=-1).reshape(MLA_Q_RANK, H * LANES)
    wq2 = jnp.concatenate([jnp.zeros_like(nope), pe[..., hm:], pe[..., :hm], zpad],
                          axis=-1).reshape(MLA_Q_RANK, H * LANES)
    wkv = w_ukv.reshape(MLA_KV_RANK, H, MLA_NOPE + MLA_V)
    kn, vv = wkv[..., :MLA_NOPE], wkv[..., MLA_NOPE:]
    wk = jnp.concatenate([kn, jnp.zeros((MLA_KV_RANK, H, LANES - MLA_NOPE), w_ukv.dtype)],
                         axis=-1).reshape(MLA_KV_RANK, H * LANES)
    wv = vv.reshape(MLA_KV_RANK, H * MLA_V).T
    return wq1.astype(BF16), wq2.astype(BF16), wk.astype(BF16), wv.astype(BF16)


def kernel(x, c, positions, w_ada, b_ada, g_norm1, w_in, g_cq, w_uq, g_ckv, w_ukv, g_ret,
           w_o_mla, w_o_ret, w_out, g_norm2, w_router, b_router, w_exp_gate, w_exp_up,
           w_exp_down, w_sh_gate, w_sh_up, w_sh_down, g_final):
    B, S, D = x.shape
    T = B * S
    x2 = x.reshape(T, D)

    mod = _ada(c, w_ada, b_ada)
    mod3 = mod.reshape(B * N_MOD, 1, D)
    cr, sr, cm, sm = _rope_tables(positions)

    z = _inproj(x2, mod3, g_norm1, _pack_w_in(w_in), S)
    wq1, wq2, wk, wv = _pack_mla_weights(w_uq, w_ukv)
    q, k, v = _mla_up(z, cm, sm, g_cq, g_ckv, wq1, wq2, wk, wv)
    o_mla = _attention(q, k, v, B, S)
    o_ret = _retention(z, cr, sr, g_ret, B, S)
    x1 = _merge(x2, o_mla, o_ret, z, mod3, w_o_mla.astype(BF16), w_o_ret.astype(BF16),
                w_out.astype(BF16), S)

    h2, wts, pp, off_rep, pc_rep = _router(x1, mod3, g_norm2, w_router, b_router, S)
    routed = _moe(h2, wts, pp, off_rep[:, 0], pc_rep[:, 0],
                  w_exp_gate.astype(BF16), w_exp_up.astype(BF16), w_exp_down.astype(BF16), n_split=B)
    out = _final(x1, routed, h2, mod3, w_sh_gate.astype(BF16), w_sh_up.astype(BF16),
                 w_sh_down.astype(BF16), g_final, S)
    return out.reshape(B, S, D)
```

```python
import functools

import numpy as np
import jax
import jax.numpy as jnp
from jax import lax
from jax.experimental import pallas as pl
from jax.experimental.pallas import tpu as pltpu

F32 = jnp.float32
BF16 = jnp.bfloat16
I32 = jnp.int32

MLA_HEADS = 8
MLA_Q_RANK = 384
MLA_KV_RANK = 256
MLA_NOPE = 64
MLA_ROPE = 32
MLA_V = 64
RET_HEADS = 4
RET_QK = 128
RET_V = 256
RET_BLOCK = 256
ROPE_THETA = 10000.0
N_EXPERTS = 64
TOP_K = 8
N_GROUPS = 8
TOPK_GROUPS = 4
D_EXPERT = 256
ROUTED_SCALE = 2.5
RMS_EPS = 1e-6
GN_EPS = 1e-5

LANES = 128
SUBLANES = 8
VMEM_LIMIT = 56 * 1024 * 1024

Z_RV, Z_RG, Z_GA, Z_GB = 0, 1024, 2048, 3072
Z_RQ, Z_RK = 4096, 4608
Z_CQKV = 5120
Z_KR1, Z_KR2 = 5760, 5888
Z_COLS = 6144

LOG2E = 1.4426950408889634

MOD_SH1, MOD_SC1, MOD_GT1, MOD_SH2, MOD_SC2, MOD_GT2 = range(6)
N_MOD = 6

ROW_TILE = 512
ROPE_TILE = 1024
INPROJ_TM, INPROJ_TN = 1024, 2048
WPACK_ROWS = 256

ATTN_BQ = 512
ATTN_BK = 512
ATTN_HEADS = 2
ATTN_LROWS = 16

MOE_TB = 1024
MOE_CH = 144
MOE_EPS = 4
MOE_VMEM_LIMIT = 58 * 1024 * 1024


def _cparams(sem, vmem_limit=VMEM_LIMIT):
    return pltpu.CompilerParams(dimension_semantics=sem, vmem_limit_bytes=vmem_limit)


def _mod_spec(D, per_b, which):
    return pl.BlockSpec((None, 1, D), lambda i, *_: ((i // per_b) * N_MOD + which, 0, 0))


def _rms(x):
    return x * lax.rsqrt(jnp.mean(x * x, axis=-1, keepdims=True) + RMS_EPS)


def _silu(x):
    return x * jax.nn.sigmoid(x)


def _load_row_tiles(ref, nrows):
    nchunk = ref.shape[0] // nrows
    return jnp.concatenate([ref[pl.ds(c, nrows, stride=nchunk), :] for c in range(nchunk)], axis=1)


def _store_row_tiles(ref, val):
    nrows, d = val.shape
    nchunk = d // LANES
    for c in range(nchunk):
        ref[pl.ds(c, nrows, stride=nchunk), :] = val[:, c * LANES:(c + 1) * LANES]


def _ada_kernel(c_ref, w_ref, b_ref, o_ref):
    c = c_ref[...]
    o_ref[...] = jnp.dot(_silu(c).astype(BF16), w_ref[...].astype(BF16),
                         preferred_element_type=F32) + b_ref[...]


def _ada(c, w_ada, b_ada):
    B, D = c.shape
    n_out = w_ada.shape[1]
    cp = jnp.zeros((SUBLANES, D), F32).at[:B].set(c)
    tn = D
    out = pl.pallas_call(
        _ada_kernel,
        grid=(n_out // tn,),
        in_specs=[pl.BlockSpec((SUBLANES, D), lambda j: (0, 0)),
                  pl.BlockSpec((D, tn), lambda j: (0, j)),
                  pl.BlockSpec((1, tn), lambda j: (0, j))],
        out_specs=pl.BlockSpec((SUBLANES, tn), lambda j: (0, j)),
        out_shape=jax.ShapeDtypeStruct((SUBLANES, n_out), F32),
        compiler_params=_cparams(("arbitrary",)),
        name="ada_mod",
    )(cp, w_ada, b_ada.reshape(1, n_out))
    return out[:B]


def _rope_kernel(pos_ref, inv_ref, cr_ref, sr_ref, cm_ref, sm_ref):
    ang = pos_ref[...].astype(F32) * inv_ref[...]
    c = jnp.cos(ang)
    s = jnp.sin(ang)
    lane = lax.broadcasted_iota(I32, c.shape, 1)
    half = RET_QK // 2
    cr_ref[...] = jnp.where(lane < half, c, pltpu.roll(c, half, 1))
    sr_ref[...] = jnp.where(lane < half, -s, pltpu.roll(s, half, 1))
    hm = MLA_ROPE // 2
    cm_ref[...] = jnp.where(lane < MLA_NOPE, 1.0,
                            jnp.where(lane < MLA_NOPE + hm, c,
                                      jnp.where(lane < MLA_NOPE + 2 * hm, pltpu.roll(c, hm, 1), 0.0)))
    sm_ref[...] = jnp.where(lane < MLA_NOPE, 0.0,
                            jnp.where(lane < MLA_NOPE + hm, -s,
                                      jnp.where(lane < MLA_NOPE + 2 * hm, pltpu.roll(s, hm, 1), 0.0)))


def _rope_tables(positions):
    T = positions.size
    tm = min(T, ROPE_TILE)
    inv_r = 1.0 / (ROPE_THETA ** (jnp.arange(0, RET_QK, 2, dtype=F32) / RET_QK))
    inv_m = 1.0 / (ROPE_THETA ** (jnp.arange(0, MLA_ROPE, 2, dtype=F32) / MLA_ROPE))
    inv = jnp.zeros((1, LANES), F32).at[0, :RET_QK // 2].set(inv_r)
    inv = inv.at[0, MLA_NOPE:MLA_NOPE + MLA_ROPE // 2].set(inv_m)
    tab = jax.ShapeDtypeStruct((T, LANES), F32)
    spec = pl.BlockSpec((tm, LANES), lambda i: (i, 0))
    return pl.pallas_call(
        _rope_kernel,
        grid=(T // tm,),
        in_specs=[pl.BlockSpec((tm, 1), lambda i: (i, 0)),
                  pl.BlockSpec((1, LANES), lambda i: (0, 0))],
        out_specs=[spec, spec, spec, spec],
        out_shape=[tab, tab, tab, tab],
        compiler_params=_cparams(("arbitrary",)),
        name="rope_tables",
    )(positions.reshape(T, 1), inv)


def _inproj_kernel(x_ref, sc_ref, sh_ref, g_ref, w_ref, z_ref, h_scr):
    @pl.when(pl.program_id(1) == 0)
    def _():
        h = _rms(x_ref[...]) * g_ref[...] * (1.0 + sc_ref[...]) + sh_ref[...]
        h_scr[...] = h.astype(BF16)

    z_ref[...] = jnp.dot(h_scr[...], w_ref[...], preferred_element_type=F32).astype(z_ref.dtype)


def _inproj(x2, mod3, g_norm1, w_pack, S):
    T, D = x2.shape
    N = w_pack.shape[1]
    tm, tn = INPROJ_TM, INPROJ_TN
    per_b = S // tm
    return pl.pallas_call(
        _inproj_kernel,
        grid=(T // tm, N // tn),
        in_specs=[pl.BlockSpec((tm, D), lambda i, j: (i, 0)),
                  _mod_spec(D, per_b, MOD_SC1), _mod_spec(D, per_b, MOD_SH1),
                  pl.BlockSpec((1, D), lambda i, j: (0, 0)),
                  pl.BlockSpec((D, tn), lambda i, j: (0, j))],
        out_specs=pl.BlockSpec((tm, tn), lambda i, j: (i, j)),
        out_shape=jax.ShapeDtypeStruct((T, N), BF16),
        scratch_shapes=[pltpu.VMEM((tm, D), BF16)],
        compiler_params=_cparams(("arbitrary", "arbitrary")),
        name="in_proj",
    )(x2, mod3, mod3, g_norm1.reshape(1, D), w_pack)


def _mla_up_kernel(zc_ref, kr1_ref, kr2_ref, cm_ref, sm_ref, gq_ref, gkv_ref,
                   wq1_ref, wq2_ref, wk_ref, wv_ref, q_ref, k_ref, v_ref):
    zc = zc_ref[...].astype(F32)
    cqn = (_rms(zc[:, :MLA_Q_RANK]) * gq_ref[...]).astype(BF16)
    ckvn = (_rms(zc[:, MLA_Q_RANK:]) * gkv_ref[...]).astype(BF16)
    cm = cm_ref[...]
    sm = sm_ref[...]
    nt = (((1,), (1,)), ((), ()))
    q1t = lax.dot_general(wq1_ref[...], cqn, nt, preferred_element_type=F32)
    q2t = lax.dot_general(wq2_ref[...], cqn, nt, preferred_element_type=F32)
    cmt, smt = cm.T, sm.T
    kn = jnp.dot(ckvn, wk_ref[...], preferred_element_type=F32)
    kpe = kr1_ref[...].astype(F32) * cm + kr2_ref[...].astype(F32) * sm
    qscale = (MLA_NOPE + MLA_ROPE) ** -0.5 * LOG2E
    for h in range(MLA_HEADS):
        sl = slice(h * LANES, (h + 1) * LANES)
        q_ref[sl, :] = ((q1t[sl, :] * cmt + q2t[sl, :] * smt) * qscale).astype(BF16)
        k_ref[:, sl] = (kn[:, sl] + kpe).astype(BF16)
    vt = lax.dot_general(wv_ref[...], ckvn, (((1,), (1,)), ((), ())),
                         preferred_element_type=F32).astype(BF16)
    vrows = MLA_V + ATTN_LROWS
    for h in range(MLA_HEADS):
        v_ref[h * vrows:h * vrows + MLA_V, :] = vt[h * MLA_V:(h + 1) * MLA_V, :]
        v_ref[h * vrows + MLA_V:(h + 1) * vrows, :] = jnp.ones((ATTN_LROWS, vt.shape[1]), BF16)


def _mla_up(z, cm, sm, g_cq, g_ckv, wq1, wq2, wk, wv):
    T = z.shape[0]
    tm = ROW_TILE
    HW = MLA_HEADS * LANES
    wc = MLA_Q_RANK + MLA_KV_RANK
    full = lambda shape: pl.BlockSpec(shape, lambda i: (0, 0))
    return pl.pallas_call(
        _mla_up_kernel,
        grid=(T // tm,),
        in_specs=[pl.BlockSpec((tm, wc), lambda i: (i, Z_CQKV // wc)),
                  pl.BlockSpec((tm, LANES), lambda i: (i, Z_KR1 // LANES)),
                  pl.BlockSpec((tm, LANES), lambda i: (i, Z_KR2 // LANES)),
                  pl.BlockSpec((tm, LANES), lambda i: (i, 0)),
                  pl.BlockSpec((tm, LANES), lambda i: (i, 0)),
                  full((1, MLA_Q_RANK)), full((1, MLA_KV_RANK)),
                  full(wq1.shape), full(wq2.shape), full(wk.shape), full(wv.shape)],
        out_specs=[pl.BlockSpec((HW, tm), lambda i: (0, i)),
                   pl.BlockSpec((tm, HW), lambda i: (i, 0)),
                   pl.BlockSpec((MLA_HEADS * (MLA_V + ATTN_LROWS), tm), lambda i: (0, i))],
        out_shape=[jax.ShapeDtypeStruct((HW, T), BF16),
                   jax.ShapeDtypeStruct((T, HW), BF16),
                   jax.ShapeDtypeStruct((MLA_HEADS * (MLA_V + ATTN_LROWS), T), BF16)],
        compiler_params=_cparams(("arbitrary",)),
        name="mla_up",
    )(z, z, z, cm, sm, g_cq.reshape(1, -1), g_ckv.reshape(1, -1), wq1, wq2, wk, wv)


def _attn_kernel(q_ref, k_ref, vt_ref, o_ref, *scr, bq, bk):
    nh = ATTN_HEADS
    slots = (scr[:nh], scr[nh:2 * nh])
    p_scrs, acc_scr, m_scr = scr[2 * nh:3 * nh], scr[3 * nh], scr[3 * nh + 1]
    mx_scrs = scr[3 * nh + 2:3 * nh + 4]
    qi = pl.program_id(2)
    vrows = MLA_V + ATTN_LROWS
    qts = [q_ref[h * LANES:(h + 1) * LANES, :] for h in range(nh)]
    acc_scr[...] = jnp.zeros(acc_scr.shape, F32)
    m_scr[...] = jnp.full(m_scr.shape, -jnp.inf, F32)
    sub8 = lax.broadcasted_iota(I32, (SUBLANES, bq), 0)
    lane8 = lax.broadcasted_iota(I32, (SUBLANES, bq), 1)
    pack = 2 * SUBLANES

    def scores(kb, slot):
        k0 = pl.multiple_of(kb * bk, bk)
        for h in range(nh):
            s = jnp.dot(k_ref[pl.ds(k0, bk), h * LANES:(h + 1) * LANES], qts[h],
                        preferred_element_type=F32)
            slots[slot][h][...] = s
            mx_scrs[slot][h * SUBLANES:(h + 1) * SUBLANES, :] = jnp.max(
                s.reshape(bk // SUBLANES, SUBLANES, bq), axis=0)

    def update(kb, slot, masked):
        k0 = pl.multiple_of(kb * bk, bk)
        for h in range(nh):
            st, pr = slots[slot][h], p_scrs[h]
            if masked:
                parts = [None, None]
                for r in range(bk // SUBLANES):
                    rs = slice(r * SUBLANES, (r + 1) * SUBLANES)
                    x = jnp.where(lane8 >= sub8 + r * SUBLANES, st[rs, :], -jnp.inf)
                    st[rs, :] = x
                    parts[r % 2] = x if parts[r % 2] is None else jnp.maximum(parts[r % 2], x)
                part = jnp.maximum(parts[0], parts[1])
            else:
                part = mx_scrs[slot][h * SUBLANES:(h + 1) * SUBLANES, :]
            m_cur = jnp.max(part, axis=0, keepdims=True)
            m_prev = m_scr[h:h + 1, :]
            m_new = jnp.maximum(m_prev, m_cur)
            m_scr[h:h + 1, :] = m_new
            alpha = jnp.exp2(m_prev - m_new)
            for r in range(bk // pack):
                rs = slice(r * pack, (r + 1) * pack)
                pr[rs, :] = jnp.exp2(st[rs, :] - m_new).astype(BF16)
            rows = slice(h * vrows, (h + 1) * vrows)
            acc_scr[rows, :] = alpha * acc_scr[rows, :] + jnp.dot(
                vt_ref[rows, pl.ds(k0, bk)], pr[...], preferred_element_type=F32)

    scores(0, 0)
    npairs = qi // 2

    def pair(i, carry):
        kb = 2 * i
        scores(kb + 1, 1)
        update(kb, 0, False)
        scores(kb + 2, 0)
        update(kb + 1, 1, False)
        return carry

    lax.fori_loop(0, npairs, pair, 0)

    @pl.when(qi % 2 == 0)
    def _():
        update(qi, 0, True)

    @pl.when(qi % 2 == 1)
    def _():
        scores(qi, 1)
        update(qi - 1, 0, False)
        update(qi, 1, True)

    ot = jnp.concatenate(
        [acc_scr[h * vrows:h * vrows + MLA_V, :]
         / acc_scr[h * vrows + MLA_V:h * vrows + MLA_V + 1, :] for h in range(nh)], axis=0)
    o_ref[...] = ot.T.astype(o_ref.dtype)


def _attention(qt, k, vt, B, S):
    T = k.shape[0]
    bq = min(ATTN_BQ, S)
    bk = min(ATTN_BK, S)
    nq = S // bq
    nh = ATTN_HEADS
    vrows = MLA_V + ATTN_LROWS
    assert bq == bk and nh <= SUBLANES
    kern = functools.partial(_attn_kernel, bq=bq, bk=bk)
    return pl.pallas_call(
        kern,
        grid=(B, MLA_HEADS // nh, nq),
        in_specs=[pl.BlockSpec((nh * LANES, bq), lambda b, p, i: (p, b * nq + i)),
                  pl.BlockSpec((S, nh * LANES), lambda b, p, i: (b, p)),
                  pl.BlockSpec((nh * vrows, S), lambda b, p, i: (p, b))],
        out_specs=pl.BlockSpec((bq, nh * MLA_V), lambda b, p, i: (b * nq + i, p)),
        out_shape=jax.ShapeDtypeStruct((T, MLA_HEADS * MLA_V), BF16),
        scratch_shapes=([pltpu.VMEM((bk, bq), F32)] * (2 * nh) + [pltpu.VMEM((bk, bq), BF16)] * nh
                        + [pltpu.VMEM((nh * vrows, bq), F32), pltpu.VMEM((SUBLANES, bq), F32)]
                        + [pltpu.VMEM((nh * SUBLANES, bq), F32)] * 2),
        compiler_params=_cparams(("arbitrary", "arbitrary", "arbitrary")),
        name="mla_attention",
    )(qt, k, vt)


def _ret_kernel(rq_ref, rk_ref, rv_ref, rg_ref, cr_ref, sr_ref, dec_ref, xi_ref, zeta_ref,
                g_ref, o_ref, state_scr):
    @pl.when(pl.program_id(1) == 0)
    def _():
        state_scr[...] = jnp.zeros(state_scr.shape, F32)

    C = rq_ref.shape[0]
    cr = cr_ref[...]
    sr = sr_ref[...]
    half = RET_QK // 2
    for h in range(RET_HEADS):
        qk = slice(h * RET_QK, (h + 1) * RET_QK)
        vv = slice(h * RET_V, (h + 1) * RET_V)
        rq = rq_ref[:, qk].astype(F32)
        rk = rk_ref[:, qk].astype(F32)
        q = rq * cr + pltpu.roll(rq, half, 1) * sr
        k = (rk * cr + pltpu.roll(rk, half, 1) * sr) * (RET_QK ** -0.5)
        v = rv_ref[:, vv]
        xi = xi_ref[h]
        state = state_scr[h]
        qb = q.astype(BF16)
        s = lax.dot_general(qb, k.astype(BF16), (((1,), (1,)), ((), ())),
                            preferred_element_type=F32) * dec_ref[h]
        inner = jnp.dot(s.astype(BF16), v, preferred_element_type=F32)
        cross = jnp.dot(qb, state.astype(BF16), preferred_element_type=F32) * xi
        kz = (k * zeta_ref[h]).astype(BF16)
        kv = lax.dot_general(kz, v, (((0,), (0,)), ((), ())), preferred_element_type=F32)
        state_scr[h] = xi[C - 1:C, :] * state + kv
        o = inner + cross
        mu = jnp.mean(o, axis=-1, keepdims=True)
        d = o - mu
        var = jnp.mean(d * d, axis=-1, keepdims=True)
        on = d * lax.rsqrt(var + GN_EPS) * g_ref[:, vv]
        o_ref[:, vv] = (_silu(rg_ref[:, vv].astype(F32)) * on).astype(o_ref.dtype)


def _retention_tables():
    C = RET_BLOCK
    h = np.arange(RET_HEADS, dtype=np.float64)
    log_g = np.log(1.0 - np.exp2(-5.0 - h))
    idx = np.arange(C, dtype=np.float64)
    diff = idx[:, None] - idx[None, :]
    decay = np.where(diff[None] >= 0, np.exp(np.maximum(diff, 0.0)[None] * log_g[:, None, None]), 0.0)
    zeta = np.exp((C - 1 - idx)[None, :] * log_g[:, None])
    xi = np.exp((idx + 1.0)[None, :] * log_g[:, None])
    zeta_rep = np.broadcast_to(zeta[:, :, None], (RET_HEADS, C, RET_QK))
    xi_rep = np.broadcast_to(xi[:, :, None], (RET_HEADS, C, RET_V))
    return (jnp.asarray(decay, F32), jnp.asarray(xi_rep, F32), jnp.asarray(zeta_rep, F32))


def _retention(z, cr, sr, g_ret, B, S):
    T = z.shape[0]
    C = RET_BLOCK
    N = S // C
    H = RET_HEADS
    WQ, WV = H * RET_QK, H * RET_V
    dec, xi, zeta = _retention_tables()
    row = lambda b, n: b * N + n
    const = lambda shape: pl.BlockSpec(shape, lambda b, n: (0,) * len(shape))
    return pl.pallas_call(
        _ret_kernel,
        grid=(B, N),
        in_specs=[pl.BlockSpec((C, WQ), lambda b, n: (row(b, n), Z_RQ // WQ)),
                  pl.BlockSpec((C, WQ), lambda b, n: (row(b, n), Z_RK // WQ)),
                  pl.BlockSpec((C, WV), lambda b, n: (row(b, n), Z_RV // WV)),
                  pl.BlockSpec((C, WV), lambda b, n: (row(b, n), Z_RG // WV)),
                  pl.BlockSpec((C, LANES), lambda b, n: (row(b, n), 0)),
                  pl.BlockSpec((C, LANES), lambda b, n: (row(b, n), 0)),
                  const((H, C, C)), const((H, C, RET_V)), const((H, C, RET_QK)), const((1, WV))],
        out_specs=pl.BlockSpec((C, WV), lambda b, n: (row(b, n), 0)),
        out_shape=jax.ShapeDtypeStruct((T, WV), BF16),
        scratch_shapes=[pltpu.VMEM((H, RET_QK, RET_V), F32)],
        compiler_params=_cparams(("arbitrary", "arbitrary")),
        name="retention",
    )(z, z, z, z, cr, sr, dec, xi, zeta, g_ret.reshape(1, -1))


def _merge_kernel(x_ref, oa_ref, ob_ref, ga_ref, gb_ref, gt_ref, wa_ref, wb_ref, wo_ref, x1_ref):
    a = jnp.dot(oa_ref[...], wa_ref[...], preferred_element_type=F32)
    b = jnp.dot(ob_ref[...], wb_ref[...], preferred_element_type=F32)
    merged = (jax.nn.sigmoid(ga_ref[...].astype(F32)) * a
              + jax.nn.sigmoid(gb_ref[...].astype(F32)) * b)
    y = jnp.dot(merged.astype(BF16), wo_ref[...], preferred_element_type=F32)
    x1_ref[...] = x_ref[...] + gt_ref[...] * y


def _merge(x2, o_mla, o_ret, z, mod3, wa, wb, wo, S):
    T, D = x2.shape
    tm = ROW_TILE
    per_b = S // tm
    full = lambda shape: pl.BlockSpec(shape, lambda i: (0, 0))
    return pl.pallas_call(
        _merge_kernel,
        grid=(T // tm,),
        in_specs=[pl.BlockSpec((tm, D), lambda i: (i, 0)),
                  pl.BlockSpec((tm, o_mla.shape[1]), lambda i: (i, 0)),
                  pl.BlockSpec((tm, D), lambda i: (i, 0)),
                  pl.BlockSpec((tm, D), lambda i: (i, Z_GA // D)),
                  pl.BlockSpec((tm, D), lambda i: (i, Z_GB // D)),
                  _mod_spec(D, per_b, MOD_GT1),
                  full(wa.shape), full(wb.shape), full(wo.shape)],
        out_specs=pl.BlockSpec((tm, D), lambda i: (i, 0)),
        out_shape=jax.ShapeDtypeStruct((T, D), F32),
        compiler_params=_cparams(("arbitrary",)),
        name="merge_out",
    )(x2, o_mla, o_ret, z, z, mod3, wa, wb, wo)


def _router_kernel(x1_ref, sc_ref, sh_ref, g_ref, wr_ref, br_ref,
                   h2_ref, wts_ref, pp_ref, off_ref, pc_ref):
    TB = x1_ref.shape[0]
    D = x1_ref.shape[1]
    E, G = N_EXPERTS, N_GROUPS
    per = E // G
    h2 = _rms(x1_ref[...]) * g_ref[...] * (1.0 + sc_ref[...]) + sh_ref[...]
    _store_row_tiles(h2_ref, h2)
    logits = lax.dot_general(wr_ref[...], h2, (((1,), (1,)), ((), ())),
                             precision=lax.Precision.HIGHEST,
                             preferred_element_type=F32)
    s = jax.nn.sigmoid(logits)
    biased = s + br_ref[...]
    sub = lax.broadcasted_iota(I32, (per, TB), 0)
    neg = -jnp.inf

    def first_argmax(vals, m, idx, sentinel):
        return jnp.min(jnp.where(vals == m, idx, sentinel), axis=0, keepdims=True)

    bg = [biased[g * per:(g + 1) * per, :] for g in range(G)]
    sg = [s[g * per:(g + 1) * per, :] for g in range(G)]
    gscore = []
    for g in range(G):
        m1 = jnp.max(bg[g], axis=0, keepdims=True)
        i1 = first_argmax(bg[g], m1, sub, per)
        m2 = jnp.max(jnp.where(sub == i1, neg, bg[g]), axis=0, keepdims=True)
        gscore.append(m1 + m2)
    gs = jnp.concatenate(gscore, axis=0)
    gidx = lax.broadcasted_iota(I32, (G, TB), 0)
    gsel = jnp.zeros((G, TB), F32)
    for _ in range(TOPK_GROUPS):
        m = jnp.max(gs, axis=0, keepdims=True)
        i = first_argmax(gs, m, gidx, G)
        hit = gidx == i
        gsel = jnp.where(hit, 1.0, gsel)
        gs = jnp.where(hit, neg, gs)
    cand = [jnp.where(gsel[g:g + 1, :] > 0.0, bg[g], neg) for g in range(G)]
    eidx = [sub + g * per for g in range(G)]
    sel = [jnp.zeros((per, TB), F32) for _ in range(G)]
    top_i, top_w = [], []
    for _ in range(TOP_K):
        m = functools.reduce(jnp.maximum, [jnp.max(c, axis=0, keepdims=True) for c in cand])
        i = functools.reduce(jnp.minimum,
                             [first_argmax(cand[g], m, eidx[g], E) for g in range(G)])
        w = jnp.zeros((1, TB), F32)
        for g in range(G):
            hit = eidx[g] == i
            w = w + jnp.sum(jnp.where(hit, sg[g], 0.0), axis=0, keepdims=True)
            sel[g] = jnp.where(hit, 1.0, sel[g])
            cand[g] = jnp.where(hit, neg, cand[g])
        top_i.append(i)
        top_w.append(w)
    wsum = functools.reduce(lambda a, b: a + b, top_w)
    wts_ref[...] = jnp.concatenate([w / wsum * ROUTED_SCALE for w in top_w], axis=0)

    mask = jnp.concatenate(sel, axis=0)
    t_row = lax.broadcasted_iota(I32, (TB, TB), 0)
    t_col = lax.broadcasted_iota(I32, (TB, TB), 1)
    before = jnp.where(t_row < t_col, 1.0, 0.0).astype(BF16)
    rank = jnp.dot(mask.astype(BF16), before, preferred_element_type=F32)
    cnt = jnp.sum(mask, axis=1, keepdims=True)
    pc_rep = jnp.broadcast_to(cnt, (E, LANES))
    e_row = lax.broadcasted_iota(I32, (E, E), 0)
    e_col = lax.broadcasted_iota(I32, (E, E), 1)
    lower = jnp.where(e_col < e_row, 1.0, 0.0)
    off_rep = jnp.dot(lower, pc_rep, precision=lax.Precision.HIGHEST,
                      preferred_element_type=F32)
    off_ref[...] = off_rep.astype(I32)
    pc_ref[...] = pc_rep.astype(I32)
    posfull = off_rep[:, :1] + rank
    pos = []
    for kk in range(TOP_K):
        p = jnp.zeros((1, TB), F32)
        for g in range(G):
            p = p + jnp.sum(jnp.where(eidx[g] == top_i[kk], posfull[g * per:(g + 1) * per, :], 0.0),
                            axis=0, keepdims=True)
        pos.append(p.astype(I32))
    pp_ref[...] = jnp.concatenate(pos, axis=0) * (D // LANES)


def _router(x1, mod3, g_norm2, w_router, b_router, S):
    T, D = x1.shape
    TB = min(MOE_TB, T)
    nb = T // TB
    per_b = S // TB
    E = N_EXPERTS
    return pl.pallas_call(
        _router_kernel,
        grid=(nb,),
        in_specs=[pl.BlockSpec((TB, D), lambda i: (i, 0)),
                  _mod_spec(D, per_b, MOD_SC2), _mod_spec(D, per_b, MOD_SH2),
                  pl.BlockSpec((1, D), lambda i: (0, 0)),
                  pl.BlockSpec((E, D), lambda i: (0, 0)),
                  pl.BlockSpec((E, 1), lambda i: (0, 0))],
        out_specs=[pl.BlockSpec((TB * D // LANES, LANES), lambda i: (i, 0)),
                   pl.BlockSpec((TOP_K, TB), lambda i: (0, i)),
                   pl.BlockSpec((TOP_K, TB), lambda i: (0, i)),
                   pl.BlockSpec((E, LANES), lambda i: (i, 0)),
                   pl.BlockSpec((E, LANES), lambda i: (i, 0))],
        out_shape=[jax.ShapeDtypeStruct((T * D // LANES, LANES), F32),
                   jax.ShapeDtypeStruct((TOP_K, T), F32),
                   jax.ShapeDtypeStruct((TOP_K, T), I32),
                   jax.ShapeDtypeStruct((nb * E, LANES), I32),
                   jax.ShapeDtypeStruct((nb * E, LANES), I32)],
        compiler_params=_cparams(("arbitrary",)),
        name="moe_router",
    )(x1, mod3, mod3, g_norm2.reshape(1, D), w_router.T, b_router.reshape(E, 1))


def _moe_kernel(pp_sm, w_sm, off_sm, cnt_sm, h2_ref, wg_ref, wu_ref, wd_ref, *rest, TB):
    out_ref, xs_scr = rest[-2:]
    j = pl.program_id(0)
    step = pl.program_id(1)
    E = N_EXPERTS
    D = wg_ref.shape[1]
    RT = D // LANES
    CH = MOE_CH

    def tile(ref, first):
        return ref.at[pl.ds(pl.multiple_of(first, RT), RT), :]

    def positions(t):
        return [pp_sm[(j * TB + t) * TOP_K + k] for k in range(TOP_K)]

    @pl.when(step == 0)
    def _dispatch():
        xs_scr[TOP_K * TB * RT:, :] = jnp.zeros((2 * CH * RT, LANES), F32)

        def scatter(t, carry):
            row = tile(h2_ref, t * RT)[...]
            for p in positions(t):
                tile(xs_scr, p)[...] = row
            return carry

        lax.fori_loop(0, TB, scatter, 0, unroll=4)

    def run_rows(ee, r0, n_left, nrows):
        blk = xs_scr.at[pl.ds(pl.multiple_of(r0 * RT, RT), nrows * RT), :]
        xin = _load_row_tiles(blk, nrows)
        xb = xin.astype(BF16)
        g = jnp.dot(xb, wg_ref[ee], preferred_element_type=F32)
        u = jnp.dot(xb, wu_ref[ee], preferred_element_type=F32)
        y = jnp.dot((_silu(g) * u).astype(BF16), wd_ref[ee], preferred_element_type=F32)
        rows = lax.broadcasted_iota(I32, (nrows, D), 0)
        _store_row_tiles(blk, jnp.where(rows < n_left, y, xin))

    for ee in range(MOE_EPS):
        e = step * MOE_EPS + ee
        st = off_sm[j * E + e]
        n = cnt_sm[j * E + e]
        nbig = n // (2 * CH)

        def big_body(i, carry, ee=ee, st=st, n=n):
            run_rows(ee, st + i * 2 * CH, n - i * 2 * CH, 2 * CH)
            return carry

        lax.fori_loop(0, nbig, big_body, 0)
        r1 = st + nbig * 2 * CH
        rem = n - nbig * 2 * CH

        @pl.when(rem > CH)
        def _(ee=ee, r1=r1, rem=rem):
            run_rows(ee, r1, rem, 2 * CH)

        @pl.when((rem > 0) & (rem <= CH))
        def _(ee=ee, r1=r1, rem=rem):
            run_rows(ee, r1, rem, CH)

    @pl.when(step == pl.num_programs(1) - 1)
    def _combine():
        def gather(t, carry):
            acc = None
            for k, p in enumerate(positions(t)):
                term = w_sm[(j * TB + t) * TOP_K + k] * tile(xs_scr, p)[...]
                acc = term if acc is None else acc + term
            tile(out_ref, t * RT)[...] = acc
            return carry

        lax.fori_loop(0, TB, gather, 0, unroll=4)


def _moe(h2, wts, pp, off, pc, wg, wu, wd, n_split):
    D = wg.shape[1]
    RT = D // LANES
    T = h2.shape[0] // RT
    E = N_EXPERTS
    EPS = MOE_EPS
    Ts = T // n_split
    TB = min(MOE_TB, Ts)
    nb = Ts // TB
    rows = TOP_K * TB + 2 * MOE_CH
    routed = None
    for s in range(n_split):
        tok = slice(s * Ts, (s + 1) * Ts)
        blk = slice(s * nb * E, (s + 1) * nb * E)
        in_specs = [pl.BlockSpec((TB * RT, LANES), lambda j, e, *_, s=s: (s * nb + j, 0),
                                 pipeline_mode=pl.Buffered(1)),
                    pl.BlockSpec((EPS, D, D_EXPERT), lambda j, e, *_: (e, 0, 0)),
                    pl.BlockSpec((EPS, D, D_EXPERT), lambda j, e, *_: (e, 0, 0)),
                    pl.BlockSpec((EPS, D_EXPERT, D), lambda j, e, *_: (e, 0, 0))]
        args = [pp[:, tok].T.reshape(-1), wts[:, tok].T.reshape(-1), off[blk], pc[blk],
                h2, wg, wu, wd]
        aliases = {}
        if routed is not None:
            in_specs.append(pl.BlockSpec(memory_space=pl.ANY))
            args.append(routed)
            aliases = {len(args) - 1: 0}
        grid_spec = pltpu.PrefetchScalarGridSpec(
            num_scalar_prefetch=4,
            grid=(nb, E // EPS),
            in_specs=in_specs,
            out_specs=pl.BlockSpec((TB * RT, LANES), lambda j, e, *_, s=s: (s * nb + j, 0),
                                   pipeline_mode=pl.Buffered(1)),
            scratch_shapes=[pltpu.VMEM((rows * RT, LANES), F32)],
        )
        routed = pl.pallas_call(
            functools.partial(_moe_kernel, TB=TB),
            grid_spec=grid_spec,
            out_shape=jax.ShapeDtypeStruct((T * RT, LANES), F32),
            input_output_aliases=aliases,
            compiler_params=_cparams(("arbitrary", "arbitrary"), MOE_VMEM_LIMIT),
            name="moe_experts",
        )(*args)
    return routed


def _final_kernel(x1_ref, routed_ref, h2_ref, gt_ref, wsg_ref, wsu_ref, wsd_ref, gf_ref, o_ref):
    tm = x1_ref.shape[0]
    hb = _load_row_tiles(h2_ref, tm).astype(BF16)
    g = jnp.dot(hb, wsg_ref[...], preferred_element_type=F32)
    u = jnp.dot(hb, wsu_ref[...], preferred_element_type=F32)
    shared = jnp.dot((_silu(g) * u).astype(BF16), wsd_ref[...], preferred_element_type=F32)
    xo = x1_ref[...] + gt_ref[...] * (_load_row_tiles(routed_ref, tm) + shared)
    o_ref[...] = _rms(xo) * gf_ref[...]


def _final(x1, routed, h2, mod3, wsg, wsu, wsd, g_final, S):
    T, D = x1.shape
    tm = ROW_TILE
    per_b = S // tm
    full = lambda shape: pl.BlockSpec(shape, lambda i: (0, 0))
    return pl.pallas_call(
        _final_kernel,
        grid=(T // tm,),
        in_specs=[pl.BlockSpec((tm, D), lambda i: (i, 0)),
                  pl.BlockSpec((tm * D // LANES, LANES), lambda i: (i, 0)),
                  pl.BlockSpec((tm * D // LANES, LANES), lambda i: (i, 0)),
                  _mod_spec(D, per_b, MOD_GT2),
                  full(wsg.shape), full(wsu.shape), full(wsd.shape), full((1, D))],
        out_specs=pl.BlockSpec((tm, D), lambda i: (i, 0)),
        out_shape=jax.ShapeDtypeStruct((T, D), F32),
        compiler_params=_cparams(("arbitrary",)),
        name="final_out",
    )(x1, routed, h2, mod3, wsg, wsu, wsd, g_final.reshape(1, D))


def _pack_w_in_kernel(w_ref, o_ref):
    splits = [MLA_Q_RANK, MLA_KV_RANK, MLA_ROPE, RET_HEADS * RET_QK, RET_HEADS * RET_QK,
              RET_HEADS * RET_V, RET_HEADS * RET_V]
    gate_w = (w_ref.shape[1] - sum(splits)) // 2
    splits += [gate_w, gate_w]
    edges = [0] + [int(v) for v in np.cumsum(splits)]
    wcq, wckv, wkr, wrq, wrk, wrv, wrg, wga, wgb = [
        w_ref[:, edges[i]:edges[i + 1]] for i in range(len(splits))]
    rows = w_ref.shape[0]
    hm = MLA_ROPE // 2
    zl = jnp.zeros((rows, MLA_NOPE), F32)
    zr = jnp.zeros((rows, LANES - MLA_NOPE - MLA_ROPE), F32)
    pad = jnp.zeros((rows, Z_COLS - (Z_KR2 + LANES)), F32)
    w = jnp.concatenate([wrv, wrg, wga, wgb, wrq, wrk, wcq, wckv,
                         zl, wkr, zr, zl, wkr[:, hm:], wkr[:, :hm], zr, pad], axis=1)
    o_ref[...] = w.astype(BF16)


def _pack_w_in(w_in):
    D, n_in = w_in.shape
    tr = WPACK_ROWS
    return pl.pallas_call(
        _pack_w_in_kernel,
        grid=(D // tr,),
        in_specs=[pl.BlockSpec((tr, n_in), lambda i: (i, 0))],
        out_specs=pl.BlockSpec((tr, Z_COLS), lambda i: (i, 0)),
        out_shape=jax.ShapeDtypeStruct((D, Z_COLS), BF16),
        compiler_params=_cparams(("arbitrary",)),
        name="pack_w_in",
    )(w_in)


def _pack_mla_weights(w_uq, w_ukv):
    H = MLA_HEADS
    hm = MLA_ROPE // 2
    wq = w_uq.reshape(MLA_Q_RANK, H, MLA_NOPE + MLA_ROPE)
    nope, pe = wq[..., :MLA_NOPE], wq[..., MLA_NOPE:]
    zpad = jnp.zeros((MLA_Q_RANK, H, LANES - MLA_NOPE - MLA_ROPE), w_uq.dtype)
    wq1 = jnp.concatenate([nope, pe, zpad], axis=-1).reshape(MLA_Q_RANK, H * LANES)
    wq2 = jnp.concatenate([jnp.zeros_like(nope), pe[..., hm:], pe[..., :hm], zpad],
                          axis=-1).reshape(MLA_Q_RANK, H * LANES)
    wkv = w_ukv.reshape(MLA_KV_RANK, H, MLA_NOPE + MLA_V)
    kn, vv = wkv[..., :MLA_NOPE], wkv[..., MLA_NOPE:]
    wk = jnp.concatenate([kn, jnp.zeros((MLA_KV_RANK, H, LANES - MLA_NOPE), w_ukv.dtype)],
                         axis=-1).reshape(MLA_KV_RANK, H * LANES)
    wv = vv.reshape(MLA_KV_RANK, H * MLA_V).T
    return wq1.T.astype(BF16), wq2.T.astype(BF16), wk.astype(BF16), wv.astype(BF16)


def kernel(x, c, positions, w_ada, b_ada, g_norm1, w_in, g_cq, w_uq, g_ckv, w_ukv, g_ret,
           w_o_mla, w_o_ret, w_out, g_norm2, w_router, b_router, w_exp_gate, w_exp_up,
           w_exp_down, w_sh_gate, w_sh_up, w_sh_down, g_final):
    B, S, D = x.shape
    T = B * S
    x2 = x.reshape(T, D)

    mod = _ada(c, w_ada, b_ada)
    mod3 = mod.reshape(B * N_MOD, 1, D)
    cr, sr, cm, sm = _rope_tables(positions)

    z = _inproj(x2, mod3, g_norm1, _pack_w_in(w_in), S)
    wq1, wq2, wk, wv = _pack_mla_weights(w_uq, w_ukv)
    q, k, v = _mla_up(z, cm, sm, g_cq, g_ckv, wq1, wq2, wk, wv)
    o_mla = _attention(q, k, v, B, S)
    o_ret = _retention(z, cr, sr, g_ret, B, S)
    x1 = _merge(x2, o_mla, o_ret, z, mod3, w_o_mla.astype(BF16), w_o_ret.astype(BF16),
                w_out.astype(BF16), S)

    h2, wts, pp, off_rep, pc_rep = _router(x1, mod3, g_norm2, w_router, b_router, S)
    routed = _moe(h2, wts, pp, off_rep[:, 0], pc_rep[:, 0],
                  w_exp_gate.astype(BF16), w_exp_up.astype(BF16), w_exp_down.astype(BF16), n_split=B)
    out = _final(x1, routed, h2, mod3, w_sh_gate.astype(BF16), w_sh_up.astype(BF16),
                 w_sh_down.astype(BF16), g_final, S)
    return out.reshape(B, S, D)
```

```python
import functools

import numpy as np
import jax
import jax.numpy as jnp
from jax import lax
from jax.experimental import pallas as pl
from jax.experimental.pallas import tpu as pltpu

F32 = jnp.float32
BF16 = jnp.bfloat16
I32 = jnp.int32

MLA_HEADS = 8
MLA_Q_RANK = 384
MLA_KV_RANK = 256
MLA_NOPE = 64
MLA_ROPE = 32
MLA_V = 64
RET_HEADS = 4
RET_QK = 128
RET_V = 256
RET_BLOCK = 256
ROPE_THETA = 10000.0
N_EXPERTS = 64
TOP_K = 8
N_GROUPS = 8
TOPK_GROUPS = 4
D_EXPERT = 256
ROUTED_SCALE = 2.5
RMS_EPS = 1e-6
GN_EPS = 1e-5

LANES = 128
SUBLANES = 8
VMEM_LIMIT = 56 * 1024 * 1024

Z_RV, Z_RG, Z_GA, Z_GB = 0, 1024, 2048, 3072
Z_RQ, Z_RK = 4096, 4608
Z_CQKV = 5120
Z_KR1, Z_KR2 = 5760, 5888
Z_COLS = 6144

LOG2E = 1.4426950408889634

MOD_SH1, MOD_SC1, MOD_GT1, MOD_SH2, MOD_SC2, MOD_GT2 = range(6)
N_MOD = 6

ROW_TILE = 512
ROPE_TILE = 1024
INPROJ_TM, INPROJ_TN = 1024, 2048
WPACK_ROWS = 256

ATTN_BQ = 512
ATTN_BK = 512
ATTN_HEADS = 2
ATTN_LROWS = 16

MOE_TB = 1024
MOE_CH = 144
MOE_EPS = 4
MOE_VMEM_LIMIT = 58 * 1024 * 1024


def _cparams(sem, vmem_limit=VMEM_LIMIT):
    return pltpu.CompilerParams(dimension_semantics=sem, vmem_limit_bytes=vmem_limit)


def _mod_spec(D, per_b, which):
    return pl.BlockSpec((None, 1, D), lambda i, *_: ((i // per_b) * N_MOD + which, 0, 0))


def _rms(x):
    return x * lax.rsqrt(jnp.mean(x * x, axis=-1, keepdims=True) + RMS_EPS)


def _silu(x):
    return x * jax.nn.sigmoid(x)


def _load_row_tiles(ref, nrows):
    nchunk = ref.shape[0] // nrows
    return jnp.concatenate([ref[pl.ds(c, nrows, stride=nchunk), :] for c in range(nchunk)], axis=1)


def _store_row_tiles(ref, val):
    nrows, d = val.shape
    nchunk = d // LANES
    for c in range(nchunk):
        ref[pl.ds(c, nrows, stride=nchunk), :] = val[:, c * LANES:(c + 1) * LANES]


def _ada_kernel(c_ref, w_ref, b_ref, o_ref):
    c = c_ref[...]
    o_ref[...] = jnp.dot(_silu(c).astype(BF16), w_ref[...].astype(BF16),
                         preferred_element_type=F32) + b_ref[...]


def _ada(c, w_ada, b_ada):
    B, D = c.shape
    n_out = w_ada.shape[1]
    cp = jnp.zeros((SUBLANES, D), F32).at[:B].set(c)
    tn = D
    out = pl.pallas_call(
        _ada_kernel,
        grid=(n_out // tn,),
        in_specs=[pl.BlockSpec((SUBLANES, D), lambda j: (0, 0)),
                  pl.BlockSpec((D, tn), lambda j: (0, j)),
                  pl.BlockSpec((1, tn), lambda j: (0, j))],
        out_specs=pl.BlockSpec((SUBLANES, tn), lambda j: (0, j)),
        out_shape=jax.ShapeDtypeStruct((SUBLANES, n_out), F32),
        compiler_params=_cparams(("arbitrary",)),
        name="ada_mod",
    )(cp, w_ada, b_ada.reshape(1, n_out))
    return out[:B]


def _rope_kernel(pos_ref, inv_ref, cr_ref, sr_ref, cm_ref, sm_ref):
    ang = pos_ref[...].astype(F32) * inv_ref[...]
    c = jnp.cos(ang)
    s = jnp.sin(ang)
    lane = lax.broadcasted_iota(I32, c.shape, 1)
    half = RET_QK // 2
    cr_ref[...] = jnp.where(lane < half, c, pltpu.roll(c, half, 1))
    sr_ref[...] = jnp.where(lane < half, -s, pltpu.roll(s, half, 1))
    hm = MLA_ROPE // 2
    cm_ref[...] = jnp.where(lane < MLA_NOPE, 1.0,
                            jnp.where(lane < MLA_NOPE + hm, c,
                                      jnp.where(lane < MLA_NOPE + 2 * hm, pltpu.roll(c, hm, 1), 0.0)))
    sm_ref[...] = jnp.where(lane < MLA_NOPE, 0.0,
                            jnp.where(lane < MLA_NOPE + hm, -s,
                                      jnp.where(lane < MLA_NOPE + 2 * hm, pltpu.roll(s, hm, 1), 0.0)))


def _rope_tables(positions):
    T = positions.size
    tm = min(T, ROPE_TILE)
    inv_r = 1.0 / (ROPE_THETA ** (jnp.arange(0, RET_QK, 2, dtype=F32) / RET_QK))
    inv_m = 1.0 / (ROPE_THETA ** (jnp.arange(0, MLA_ROPE, 2, dtype=F32) / MLA_ROPE))
    inv = jnp.zeros((1, LANES), F32).at[0, :RET_QK // 2].set(inv_r)
    inv = inv.at[0, MLA_NOPE:MLA_NOPE + MLA_ROPE // 2].set(inv_m)
    tab = jax.ShapeDtypeStruct((T, LANES), F32)
    spec = pl.BlockSpec((tm, LANES), lambda i: (i, 0))
    return pl.pallas_call(
        _rope_kernel,
        grid=(T // tm,),
        in_specs=[pl.BlockSpec((tm, 1), lambda i: (i, 0)),
                  pl.BlockSpec((1, LANES), lambda i: (0, 0))],
        out_specs=[spec, spec, spec, spec],
        out_shape=[tab, tab, tab, tab],
        compiler_params=_cparams(("arbitrary",)),
        name="rope_tables",
    )(positions.reshape(T, 1), inv)


def _inproj_kernel(x_ref, sc_ref, sh_ref, g_ref, w_ref, z_ref, h_scr):
    @pl.when(pl.program_id(1) == 0)
    def _():
        h = _rms(x_ref[...]) * g_ref[...] * (1.0 + sc_ref[...]) + sh_ref[...]
        h_scr[...] = h.astype(BF16)

    z_ref[...] = jnp.dot(h_scr[...], w_ref[...], preferred_element_type=F32).astype(z_ref.dtype)


def _inproj(x2, mod3, g_norm1, w_pack, S):
    T, D = x2.shape
    N = w_pack.shape[1]
    tm, tn = INPROJ_TM, INPROJ_TN
    per_b = S // tm
    return pl.pallas_call(
        _inproj_kernel,
        grid=(T // tm, N // tn),
        in_specs=[pl.BlockSpec((tm, D), lambda i, j: (i, 0)),
                  _mod_spec(D, per_b, MOD_SC1), _mod_spec(D, per_b, MOD_SH1),
                  pl.BlockSpec((1, D), lambda i, j: (0, 0)),
                  pl.BlockSpec((D, tn), lambda i, j: (0, j))],
        out_specs=pl.BlockSpec((tm, tn), lambda i, j: (i, j)),
        out_shape=jax.ShapeDtypeStruct((T, N), BF16),
        scratch_shapes=[pltpu.VMEM((tm, D), BF16)],
        compiler_params=_cparams(("arbitrary", "arbitrary")),
        name="in_proj",
    )(x2, mod3, mod3, g_norm1.reshape(1, D), w_pack)


def _mla_up_kernel(zc_ref, kr1_ref, kr2_ref, cm_ref, sm_ref, gq_ref, gkv_ref,
                   wq1_ref, wq2_ref, wk_ref, wv_ref, q_ref, k_ref, v_ref):
    zc = zc_ref[...].astype(F32)
    cqn = (_rms(zc[:, :MLA_Q_RANK]) * gq_ref[...]).astype(BF16)
    ckvn = (_rms(zc[:, MLA_Q_RANK:]) * gkv_ref[...]).astype(BF16)
    cm = cm_ref[...]
    sm = sm_ref[...]
    nt = (((1,), (1,)), ((), ()))
    q1t = lax.dot_general(wq1_ref[...], cqn, nt, preferred_element_type=F32)
    q2t = lax.dot_general(wq2_ref[...], cqn, nt, preferred_element_type=F32)
    cmt, smt = cm.T, sm.T
    kn = jnp.dot(ckvn, wk_ref[...], preferred_element_type=F32)
    kpe = kr1_ref[...].astype(F32) * cm + kr2_ref[...].astype(F32) * sm
    qscale = (MLA_NOPE + MLA_ROPE) ** -0.5 * LOG2E
    for h in range(MLA_HEADS):
        sl = slice(h * LANES, (h + 1) * LANES)
        q_ref[sl, :] = ((q1t[sl, :] * cmt + q2t[sl, :] * smt) * qscale).astype(BF16)
        k_ref[:, sl] = (kn[:, sl] + kpe).astype(BF16)
    vt = lax.dot_general(wv_ref[...], ckvn, (((1,), (1,)), ((), ())),
                         preferred_element_type=F32).astype(BF16)
    vrows = MLA_V + ATTN_LROWS
    for h in range(MLA_HEADS):
        v_ref[h * vrows:h * vrows + MLA_V, :] = vt[h * MLA_V:(h + 1) * MLA_V, :]
        v_ref[h * vrows + MLA_V:(h + 1) * vrows, :] = jnp.ones((ATTN_LROWS, vt.shape[1]), BF16)


def _mla_up(z, cm, sm, g_cq, g_ckv, wq1, wq2, wk, wv):
    T = z.shape[0]
    tm = ROW_TILE
    HW = MLA_HEADS * LANES
    wc = MLA_Q_RANK + MLA_KV_RANK
    full = lambda shape: pl.BlockSpec(shape, lambda i: (0, 0))
    return pl.pallas_call(
        _mla_up_kernel,
        grid=(T // tm,),
        in_specs=[pl.BlockSpec((tm, wc), lambda i: (i, Z_CQKV // wc)),
                  pl.BlockSpec((tm, LANES), lambda i: (i, Z_KR1 // LANES)),
                  pl.BlockSpec((tm, LANES), lambda i: (i, Z_KR2 // LANES)),
                  pl.BlockSpec((tm, LANES), lambda i: (i, 0)),
                  pl.BlockSpec((tm, LANES), lambda i: (i, 0)),
                  full((1, MLA_Q_RANK)), full((1, MLA_KV_RANK)),
                  full(wq1.shape), full(wq2.shape), full(wk.shape), full(wv.shape)],
        out_specs=[pl.BlockSpec((HW, tm), lambda i: (0, i)),
                   pl.BlockSpec((tm, HW), lambda i: (i, 0)),
                   pl.BlockSpec((MLA_HEADS * (MLA_V + ATTN_LROWS), tm), lambda i: (0, i))],
        out_shape=[jax.ShapeDtypeStruct((HW, T), BF16),
                   jax.ShapeDtypeStruct((T, HW), BF16),
                   jax.ShapeDtypeStruct((MLA_HEADS * (MLA_V + ATTN_LROWS), T), BF16)],
        compiler_params=_cparams(("arbitrary",)),
        name="mla_up",
    )(z, z, z, cm, sm, g_cq.reshape(1, -1), g_ckv.reshape(1, -1), wq1, wq2, wk, wv)


def _attn_kernel(q_ref, k_ref, vt_ref, o_ref, *scr, bq, bk):
    nh = ATTN_HEADS
    slots = (scr[:nh], scr[nh:2 * nh])
    p_scrs, acc_scr, m_scr = scr[2 * nh:3 * nh], scr[3 * nh], scr[3 * nh + 1]
    mx_scrs = scr[3 * nh + 2:3 * nh + 4]
    qi = pl.program_id(2)
    vrows = MLA_V + ATTN_LROWS
    qts = [q_ref[h * LANES:(h + 1) * LANES, :] for h in range(nh)]
    acc_scr[...] = jnp.zeros(acc_scr.shape, F32)
    m_scr[...] = jnp.full(m_scr.shape, -jnp.inf, F32)
    sub8 = lax.broadcasted_iota(I32, (SUBLANES, bq), 0)
    lane8 = lax.broadcasted_iota(I32, (SUBLANES, bq), 1)
    pack = 2 * SUBLANES

    def scores(kb, slot):
        k0 = pl.multiple_of(kb * bk, bk)
        for h in range(nh):
            s = jnp.dot(k_ref[pl.ds(k0, bk), h * LANES:(h + 1) * LANES], qts[h],
                        preferred_element_type=F32)
            slots[slot][h][...] = s
            mx_scrs[slot][h * SUBLANES:(h + 1) * SUBLANES, :] = jnp.max(
                s.reshape(bk // SUBLANES, SUBLANES, bq), axis=0)

    def update(kb, slot, masked):
        k0 = pl.multiple_of(kb * bk, bk)
        for h in range(nh):
            st, pr = slots[slot][h], p_scrs[h]
            if masked:
                parts = [None, None]
                for r in range(bk // SUBLANES):
                    rs = slice(r * SUBLANES, (r + 1) * SUBLANES)
                    x = jnp.where(lane8 >= sub8 + r * SUBLANES, st[rs, :], -jnp.inf)
                    st[rs, :] = x
                    parts[r % 2] = x if parts[r % 2] is None else jnp.maximum(parts[r % 2], x)
                part = jnp.maximum(parts[0], parts[1])
            else:
                part = mx_scrs[slot][h * SUBLANES:(h + 1) * SUBLANES, :]
            m_cur = jnp.max(part, axis=0, keepdims=True)
            m_prev = m_scr[h:h + 1, :]
            m_new = jnp.maximum(m_prev, m_cur)
            m_scr[h:h + 1, :] = m_new
            alpha = jnp.exp2(m_prev - m_new)
            for r in range(bk // pack):
                rs = slice(r * pack, (r + 1) * pack)
                pr[rs, :] = jnp.exp2(st[rs, :] - m_new).astype(BF16)
            rows = slice(h * vrows, (h + 1) * vrows)
            acc_scr[rows, :] = alpha * acc_scr[rows, :] + jnp.dot(
                vt_ref[rows, pl.ds(k0, bk)], pr[...], preferred_element_type=F32)

    scores(0, 0)
    npairs = qi // 2

    def pair(i, carry):
        kb = 2 * i
        scores(kb + 1, 1)
        update(kb, 0, False)
        scores(kb + 2, 0)
        update(kb + 1, 1, False)
        return carry

    lax.fori_loop(0, npairs, pair, 0)

    @pl.when(qi % 2 == 0)
    def _():
        update(qi, 0, True)

    @pl.when(qi % 2 == 1)
    def _():
        scores(qi, 1)
        update(qi - 1, 0, False)
        update(qi, 1, True)

    ot = jnp.concatenate(
        [acc_scr[h * vrows:h * vrows + MLA_V, :]
         / acc_scr[h * vrows + MLA_V:h * vrows + MLA_V + 1, :] for h in range(nh)], axis=0)
    o_ref[...] = ot.T.astype(o_ref.dtype)


def _attention(qt, k, vt, B, S):
    T = k.shape[0]
    bq = min(ATTN_BQ, S)
    bk = min(ATTN_BK, S)
    nq = S // bq
    nh = ATTN_HEADS
    vrows = MLA_V + ATTN_LROWS
    assert bq == bk and nh <= SUBLANES
    kern = functools.partial(_attn_kernel, bq=bq, bk=bk)
    return pl.pallas_call(
        kern,
        grid=(B, MLA_HEADS // nh, nq),
        in_specs=[pl.BlockSpec((nh * LANES, bq), lambda b, p, i: (p, b * nq + i)),
                  pl.BlockSpec((S, nh * LANES), lambda b, p, i: (b, p)),
                  pl.BlockSpec((nh * vrows, S), lambda b, p, i: (p, b))],
        out_specs=pl.BlockSpec((bq, nh * MLA_V), lambda b, p, i: (b * nq + i, p)),
        out_shape=jax.ShapeDtypeStruct((T, MLA_HEADS * MLA_V), BF16),
        scratch_shapes=([pltpu.VMEM((bk, bq), F32)] * (2 * nh) + [pltpu.VMEM((bk, bq), BF16)] * nh
                        + [pltpu.VMEM((nh * vrows, bq), F32), pltpu.VMEM((SUBLANES, bq), F32)]
                        + [pltpu.VMEM((nh * SUBLANES, bq), F32)] * 2),
        compiler_params=_cparams(("arbitrary", "arbitrary", "arbitrary")),
        name="mla_attention",
    )(qt, k, vt)


def _ret_kernel(rq_ref, rk_ref, rv_ref, rg_ref, cr_ref, sr_ref, dec_ref, xi_ref, zeta_ref,
                g_ref, o_ref, state_scr):
    @pl.when(pl.program_id(1) == 0)
    def _():
        state_scr[...] = jnp.zeros(state_scr.shape, F32)

    C = rq_ref.shape[0]
    cr = cr_ref[...]
    sr = sr_ref[...]
    half = RET_QK // 2
    for h in range(RET_HEADS):
        qk = slice(h * RET_QK, (h + 1) * RET_QK)
        vv = slice(h * RET_V, (h + 1) * RET_V)
        rq = rq_ref[:, qk].astype(F32)
        rk = rk_ref[:, qk].astype(F32)
        q = rq * cr + pltpu.roll(rq, half, 1) * sr
        k = (rk * cr + pltpu.roll(rk, half, 1) * sr) * (RET_QK ** -0.5)
        v = rv_ref[:, vv]
        xi = xi_ref[h]
        state = state_scr[h]
        qb = q.astype(BF16)
        s = lax.dot_general(qb, k.astype(BF16), (((1,), (1,)), ((), ())),
                            preferred_element_type=F32) * dec_ref[h]
        inner = jnp.dot(s.astype(BF16), v, preferred_element_type=F32)
        cross = jnp.dot(qb, state.astype(BF16), preferred_element_type=F32) * xi
        kz = (k * zeta_ref[h]).astype(BF16)
        kv = lax.dot_general(kz, v, (((0,), (0,)), ((), ())), preferred_element_type=F32)
        state_scr[h] = xi[C - 1:C, :] * state + kv
        o = inner + cross
        mu = jnp.mean(o, axis=-1, keepdims=True)
        d = o - mu
        var = jnp.mean(d * d, axis=-1, keepdims=True)
        on = d * lax.rsqrt(var + GN_EPS) * g_ref[:, vv]
        o_ref[:, vv] = (_silu(rg_ref[:, vv].astype(F32)) * on).astype(o_ref.dtype)


def _retention_tables():
    C = RET_BLOCK
    h = np.arange(RET_HEADS, dtype=np.float64)
    log_g = np.log(1.0 - np.exp2(-5.0 - h))
    idx = np.arange(C, dtype=np.float64)
    diff = idx[:, None] - idx[None, :]
    decay = np.where(diff[None] >= 0, np.exp(np.maximum(diff, 0.0)[None] * log_g[:, None, None]), 0.0)
    zeta = np.exp((C - 1 - idx)[None, :] * log_g[:, None])
    xi = np.exp((idx + 1.0)[None, :] * log_g[:, None])
    zeta_rep = np.broadcast_to(zeta[:, :, None], (RET_HEADS, C, RET_QK))
    xi_rep = np.broadcast_to(xi[:, :, None], (RET_HEADS, C, RET_V))
    return (jnp.asarray(decay, F32), jnp.asarray(xi_rep, F32), jnp.asarray(zeta_rep, F32))


def _retention(z, cr, sr, g_ret, B, S):
    T = z.shape[0]
    C = RET_BLOCK
    N = S // C
    H = RET_HEADS
    WQ, WV = H * RET_QK, H * RET_V
    dec, xi, zeta = _retention_tables()
    row = lambda b, n: b * N + n
    const = lambda shape: pl.BlockSpec(shape, lambda b, n: (0,) * len(shape))
    return pl.pallas_call(
        _ret_kernel,
        grid=(B, N),
        in_specs=[pl.BlockSpec((C, WQ), lambda b, n: (row(b, n), Z_RQ // WQ)),
                  pl.BlockSpec((C, WQ), lambda b, n: (row(b, n), Z_RK // WQ)),
                  pl.BlockSpec((C, WV), lambda b, n: (row(b, n), Z_RV // WV)),
                  pl.BlockSpec((C, WV), lambda b, n: (row(b, n), Z_RG // WV)),
                  pl.BlockSpec((C, LANES), lambda b, n: (row(b, n), 0)),
                  pl.BlockSpec((C, LANES), lambda b, n: (row(b, n), 0)),
                  const((H, C, C)), const((H, C, RET_V)), const((H, C, RET_QK)), const((1, WV))],
        out_specs=pl.BlockSpec((C, WV), lambda b, n: (row(b, n), 0)),
        out_shape=jax.ShapeDtypeStruct((T, WV), BF16),
        scratch_shapes=[pltpu.VMEM((H, RET_QK, RET_V), F32)],
        compiler_params=_cparams(("arbitrary", "arbitrary")),
        name="retention",
    )(z, z, z, z, cr, sr, dec, xi, zeta, g_ret.reshape(1, -1))


def _merge_kernel(x_ref, oa_ref, ob_ref, ga_ref, gb_ref, gt_ref, wa_ref, wb_ref, wo_ref, x1_ref):
    a = jnp.dot(oa_ref[...], wa_ref[...], preferred_element_type=F32)
    b = jnp.dot(ob_ref[...], wb_ref[...], preferred_element_type=F32)
    merged = (jax.nn.sigmoid(ga_ref[...].astype(F32)) * a
              + jax.nn.sigmoid(gb_ref[...].astype(F32)) * b)
    y = jnp.dot(merged.astype(BF16), wo_ref[...], preferred_element_type=F32)
    x1_ref[...] = x_ref[...] + gt_ref[...] * y


def _merge(x2, o_mla, o_ret, z, mod3, wa, wb, wo, S):
    T, D = x2.shape
    tm = ROW_TILE
    per_b = S // tm
    full = lambda shape: pl.BlockSpec(shape, lambda i: (0, 0))
    return pl.pallas_call(
        _merge_kernel,
        grid=(T // tm,),
        in_specs=[pl.BlockSpec((tm, D), lambda i: (i, 0)),
                  pl.BlockSpec((tm, o_mla.shape[1]), lambda i: (i, 0)),
                  pl.BlockSpec((tm, D), lambda i: (i, 0)),
                  pl.BlockSpec((tm, D), lambda i: (i, Z_GA // D)),
                  pl.BlockSpec((tm, D), lambda i: (i, Z_GB // D)),
                  _mod_spec(D, per_b, MOD_GT1),
                  full(wa.shape), full(wb.shape), full(wo.shape)],
        out_specs=pl.BlockSpec((tm, D), lambda i: (i, 0)),
        out_shape=jax.ShapeDtypeStruct((T, D), F32),
        compiler_params=_cparams(("arbitrary",)),
        name="merge_out",
    )(x2, o_mla, o_ret, z, z, mod3, wa, wb, wo)


def _router_kernel(x1_ref, sc_ref, sh_ref, g_ref, wr_ref, br_ref,
                   h2_ref, wts_ref, pp_ref, off_ref, pc_ref):
    TB = x1_ref.shape[0]
    D = x1_ref.shape[1]
    E, G = N_EXPERTS, N_GROUPS
    per = E // G
    h2 = _rms(x1_ref[...]) * g_ref[...] * (1.0 + sc_ref[...]) + sh_ref[...]
    _store_row_tiles(h2_ref, h2)
    logits = lax.dot_general(wr_ref[...], h2, (((1,), (1,)), ((), ())),
                             precision=lax.Precision.HIGHEST,
                             preferred_element_type=F32)
    s = jax.nn.sigmoid(logits)
    biased = s + br_ref[...]
    sub = lax.broadcasted_iota(I32, (per, TB), 0)
    neg = -jnp.inf

    def first_argmax(vals, m, idx, sentinel):
        return jnp.min(jnp.where(vals == m, idx, sentinel), axis=0, keepdims=True)

    bg = [biased[g * per:(g + 1) * per, :] for g in range(G)]
    sg = [s[g * per:(g + 1) * per, :] for g in range(G)]
    gscore = []
    for g in range(G):
        m1 = jnp.max(bg[g], axis=0, keepdims=True)
        i1 = first_argmax(bg[g], m1, sub, per)
        m2 = jnp.max(jnp.where(sub == i1, neg, bg[g]), axis=0, keepdims=True)
        gscore.append(m1 + m2)
    gs = jnp.concatenate(gscore, axis=0)
    gidx = lax.broadcasted_iota(I32, (G, TB), 0)
    gsel = jnp.zeros((G, TB), F32)
    for _ in range(TOPK_GROUPS):
        m = jnp.max(gs, axis=0, keepdims=True)
        i = first_argmax(gs, m, gidx, G)
        hit = gidx == i
        gsel = jnp.where(hit, 1.0, gsel)
        gs = jnp.where(hit, neg, gs)
    cand = [jnp.where(gsel[g:g + 1, :] > 0.0, bg[g], neg) for g in range(G)]
    eidx = [sub + g * per for g in range(G)]
    sel = [jnp.zeros((per, TB), F32) for _ in range(G)]
    top_i, top_w = [], []
    for _ in range(TOP_K):
        m = functools.reduce(jnp.maximum, [jnp.max(c, axis=0, keepdims=True) for c in cand])
        i = functools.reduce(jnp.minimum,
                             [first_argmax(cand[g], m, eidx[g], E) for g in range(G)])
        w = jnp.zeros((1, TB), F32)
        for g in range(G):
            hit = eidx[g] == i
            w = w + jnp.sum(jnp.where(hit, sg[g], 0.0), axis=0, keepdims=True)
            sel[g] = jnp.where(hit, 1.0, sel[g])
            cand[g] = jnp.where(hit, neg, cand[g])
        top_i.append(i)
        top_w.append(w)
    wsum = functools.reduce(lambda a, b: a + b, top_w)
    wts_ref[...] = jnp.concatenate([w / wsum * ROUTED_SCALE for w in top_w], axis=0)

    mask = jnp.concatenate(sel, axis=0)
    t_row = lax.broadcasted_iota(I32, (TB, TB), 0)
    t_col = lax.broadcasted_iota(I32, (TB, TB), 1)
    before = jnp.where(t_row < t_col, 1.0, 0.0).astype(BF16)
    rank = jnp.dot(mask.astype(BF16), before, preferred_element_type=F32)
    cnt = jnp.sum(mask, axis=1, keepdims=True)
    pc_rep = jnp.broadcast_to(cnt, (E, LANES))
    e_row = lax.broadcasted_iota(I32, (E, E), 0)
    e_col = lax.broadcasted_iota(I32, (E, E), 1)
    lower = jnp.where(e_col < e_row, 1.0, 0.0)
    off_rep = jnp.dot(lower, pc_rep, precision=lax.Precision.HIGHEST,
                      preferred_element_type=F32)
    off_ref[...] = off_rep.astype(I32)
    pc_ref[...] = pc_rep.astype(I32)
    posfull = off_rep[:, :1] + rank
    pos = []
    for kk in range(TOP_K):
        p = jnp.zeros((1, TB), F32)
        for g in range(G):
            p = p + jnp.sum(jnp.where(eidx[g] == top_i[kk], posfull[g * per:(g + 1) * per, :], 0.0),
                            axis=0, keepdims=True)
        pos.append(p.astype(I32))
    pp_ref[...] = jnp.concatenate(pos, axis=0) * (D // LANES)


def _router(x1, mod3, g_norm2, w_router, b_router, S):
    T, D = x1.shape
    TB = min(MOE_TB, T)
    nb = T // TB
    per_b = S // TB
    E = N_EXPERTS
    return pl.pallas_call(
        _router_kernel,
        grid=(nb,),
        in_specs=[pl.BlockSpec((TB, D), lambda i: (i, 0)),
                  _mod_spec(D, per_b, MOD_SC2), _mod_spec(D, per_b, MOD_SH2),
                  pl.BlockSpec((1, D), lambda i: (0, 0)),
                  pl.BlockSpec((E, D), lambda i: (0, 0)),
                  pl.BlockSpec((E, 1), lambda i: (0, 0))],
        out_specs=[pl.BlockSpec((TB * D // LANES, LANES), lambda i: (i, 0)),
                   pl.BlockSpec((TOP_K, TB), lambda i: (0, i)),
                   pl.BlockSpec((TOP_K, TB), lambda i: (0, i)),
                   pl.BlockSpec((E, LANES), lambda i: (i, 0)),
                   pl.BlockSpec((E, LANES), lambda i: (i, 0))],
        out_shape=[jax.ShapeDtypeStruct((T * D // LANES, LANES), F32),
                   jax.ShapeDtypeStruct((TOP_K, T), F32),
                   jax.ShapeDtypeStruct((TOP_K, T), I32),
                   jax.ShapeDtypeStruct((nb * E, LANES), I32),
                   jax.ShapeDtypeStruct((nb * E, LANES), I32)],
        compiler_params=_cparams(("arbitrary",)),
        name="moe_router",
    )(x1, mod3, mod3, g_norm2.reshape(1, D), w_router.T, b_router.reshape(E, 1))


def _moe_kernel(pp_sm, w_sm, off_sm, cnt_sm, h2_ref, wg_ref, wu_ref, wd_ref, *rest, TB):
    out_ref, xs_scr = rest[-2:]
    j = pl.program_id(0)
    step = pl.program_id(1)
    E = N_EXPERTS
    D = wg_ref.shape[1]
    RT = D // LANES
    CH = MOE_CH

    def tile(ref, first):
        return ref.at[pl.ds(pl.multiple_of(first, RT), RT), :]

    def positions(t):
        return [pp_sm[(j * TB + t) * TOP_K + k] for k in range(TOP_K)]

    @pl.when(step == 0)
    def _dispatch():
        xs_scr[TOP_K * TB * RT:, :] = jnp.zeros((2 * CH * RT, LANES), F32)

        def scatter(t, carry):
            row = tile(h2_ref, t * RT)[...]
            for p in positions(t):
                tile(xs_scr, p)[...] = row
            return carry

        lax.fori_loop(0, TB, scatter, 0, unroll=4)

    def run_rows(ee, r0, n_left, nrows):
        blk = xs_scr.at[pl.ds(pl.multiple_of(r0 * RT, RT), nrows * RT), :]
        xin = _load_row_tiles(blk, nrows)
        xb = xin.astype(BF16)
        g = jnp.dot(xb, wg_ref[ee], preferred_element_type=F32)
        u = jnp.dot(xb, wu_ref[ee], preferred_element_type=F32)
        y = jnp.dot((_silu(g) * u).astype(BF16), wd_ref[ee], preferred_element_type=F32)
        rows = lax.broadcasted_iota(I32, (nrows, D), 0)
        _store_row_tiles(blk, jnp.where(rows < n_left, y, xin))

    def expert(ee, carry):
        e = step * MOE_EPS + ee
        st = off_sm[j * E + e]
        n = cnt_sm[j * E + e]
        nbig = n // (2 * CH)

        def big_body(i, c):
            run_rows(ee, st + i * 2 * CH, n - i * 2 * CH, 2 * CH)
            return c

        lax.fori_loop(0, nbig, big_body, 0)
        r1 = st + nbig * 2 * CH
        rem = n - nbig * 2 * CH

        @pl.when(rem > CH)
        def _():
            run_rows(ee, r1, rem, 2 * CH)

        @pl.when((rem > 0) & (rem <= CH))
        def _():
            run_rows(ee, r1, rem, CH)

        return carry

    lax.fori_loop(0, MOE_EPS, expert, 0)

    @pl.when(step == pl.num_programs(1) - 1)
    def _combine():
        def gather(t, carry):
            acc = None
            for k, p in enumerate(positions(t)):
                term = w_sm[(j * TB + t) * TOP_K + k] * tile(xs_scr, p)[...]
                acc = term if acc is None else acc + term
            tile(out_ref, t * RT)[...] = acc
            return carry

        lax.fori_loop(0, TB, gather, 0, unroll=4)


def _moe(h2, wts, pp, off, pc, wg, wu, wd, n_split):
    D = wg.shape[1]
    RT = D // LANES
    T = h2.shape[0] // RT
    E = N_EXPERTS
    EPS = MOE_EPS
    Ts = T // n_split
    TB = min(MOE_TB, Ts)
    nb = Ts // TB
    rows = TOP_K * TB + 2 * MOE_CH
    routed = None
    for s in range(n_split):
        tok = slice(s * Ts, (s + 1) * Ts)
        blk = slice(s * nb * E, (s + 1) * nb * E)
        in_specs = [pl.BlockSpec((TB * RT, LANES), lambda j, e, *_, s=s: (s * nb + j, 0),
                                 pipeline_mode=pl.Buffered(1)),
                    pl.BlockSpec((EPS, D, D_EXPERT), lambda j, e, *_: (e, 0, 0)),
                    pl.BlockSpec((EPS, D, D_EXPERT), lambda j, e, *_: (e, 0, 0)),
                    pl.BlockSpec((EPS, D_EXPERT, D), lambda j, e, *_: (e, 0, 0))]
        args = [pp[:, tok].T.reshape(-1), wts[:, tok].T.reshape(-1), off[blk], pc[blk],
                h2, wg, wu, wd]
        aliases = {}
        if routed is not None:
            in_specs.append(pl.BlockSpec(memory_space=pl.ANY))
            args.append(routed)
            aliases = {len(args) - 1: 0}
        grid_spec = pltpu.PrefetchScalarGridSpec(
            num_scalar_prefetch=4,
            grid=(nb, E // EPS),
            in_specs=in_specs,
            out_specs=pl.BlockSpec((TB * RT, LANES), lambda j, e, *_, s=s: (s * nb + j, 0),
                                   pipeline_mode=pl.Buffered(1)),
            scratch_shapes=[pltpu.VMEM((rows * RT, LANES), F32)],
        )
        routed = pl.pallas_call(
            functools.partial(_moe_kernel, TB=TB),
            grid_spec=grid_spec,
            out_shape=jax.ShapeDtypeStruct((T * RT, LANES), F32),
            input_output_aliases=aliases,
            compiler_params=_cparams(("arbitrary", "arbitrary"), MOE_VMEM_LIMIT),
            name="moe_experts",
        )(*args)
    return routed


def _final_kernel(x1_ref, routed_ref, h2_ref, gt_ref, wsg_ref, wsu_ref, wsd_ref, gf_ref, o_ref):
    tm = x1_ref.shape[0]
    hb = _load_row_tiles(h2_ref, tm).astype(BF16)
    g = jnp.dot(hb, wsg_ref[...], preferred_element_type=F32)
    u = jnp.dot(hb, wsu_ref[...], preferred_element_type=F32)
    shared = jnp.dot((_silu(g) * u).astype(BF16), wsd_ref[...], preferred_element_type=F32)
    xo = x1_ref[...] + gt_ref[...] * (_load_row_tiles(routed_ref, tm) + shared)
    o_ref[...] = _rms(xo) * gf_ref[...]


def _final(x1, routed, h2, mod3, wsg, wsu, wsd, g_final, S):
    T, D = x1.shape
    tm = ROW_TILE
    per_b = S // tm
    full = lambda shape: pl.BlockSpec(shape, lambda i: (0, 0))
    return pl.pallas_call(
        _final_kernel,
        grid=(T // tm,),
        in_specs=[pl.BlockSpec((tm, D), lambda i: (i, 0)),
                  pl.BlockSpec((tm * D // LANES, LANES), lambda i: (i, 0)),
                  pl.BlockSpec((tm * D // LANES, LANES), lambda i: (i, 0)),
                  _mod_spec(D, per_b, MOD_GT2),
                  full(wsg.shape), full(wsu.shape), full(wsd.shape), full((1, D))],
        out_specs=pl.BlockSpec((tm, D), lambda i: (i, 0)),
        out_shape=jax.ShapeDtypeStruct((T, D), F32),
        compiler_params=_cparams(("arbitrary",)),
        name="final_out",
    )(x1, routed, h2, mod3, wsg, wsu, wsd, g_final.reshape(1, D))


def _pack_w_in_kernel(w_ref, o_ref):
    splits = [MLA_Q_RANK, MLA_KV_RANK, MLA_ROPE, RET_HEADS * RET_QK, RET_HEADS * RET_QK,
              RET_HEADS * RET_V, RET_HEADS * RET_V]
    gate_w = (w_ref.shape[1] - sum(splits)) // 2
    splits += [gate_w, gate_w]
    edges = [0] + [int(v) for v in np.cumsum(splits)]
    wcq, wckv, wkr, wrq, wrk, wrv, wrg, wga, wgb = [
        w_ref[:, edges[i]:edges[i + 1]] for i in range(len(splits))]
    rows = w_ref.shape[0]
    hm = MLA_ROPE // 2
    zl = jnp.zeros((rows, MLA_NOPE), F32)
    zr = jnp.zeros((rows, LANES - MLA_NOPE - MLA_ROPE), F32)
    pad = jnp.zeros((rows, Z_COLS - (Z_KR2 + LANES)), F32)
    w = jnp.concatenate([wrv, wrg, wga, wgb, wrq, wrk, wcq, wckv,
                         zl, wkr, zr, zl, wkr[:, hm:], wkr[:, :hm], zr, pad], axis=1)
    o_ref[...] = w.astype(BF16)


def _pack_w_in(w_in):
    D, n_in = w_in.shape
    tr = WPACK_ROWS
    return pl.pallas_call(
        _pack_w_in_kernel,
        grid=(D // tr,),
        in_specs=[pl.BlockSpec((tr, n_in), lambda i: (i, 0))],
        out_specs=pl.BlockSpec((tr, Z_COLS), lambda i: (i, 0)),
        out_shape=jax.ShapeDtypeStruct((D, Z_COLS), BF16),
        compiler_params=_cparams(("arbitrary",)),
        name="pack_w_in",
    )(w_in)


def _pack_mla_weights(w_uq, w_ukv):
    H = MLA_HEADS
    hm = MLA_ROPE // 2
    wq = w_uq.reshape(MLA_Q_RANK, H, MLA_NOPE + MLA_ROPE)
    nope, pe = wq[..., :MLA_NOPE], wq[..., MLA_NOPE:]
    zpad = jnp.zeros((MLA_Q_RANK, H, LANES - MLA_NOPE - MLA_ROPE), w_uq.dtype)
    wq1 = jnp.concatenate([nope, pe, zpad], axis=-1).reshape(MLA_Q_RANK, H * LANES)
    wq2 = jnp.concatenate([jnp.zeros_like(nope), pe[..., hm:], pe[..., :hm], zpad],
                          axis=-1).reshape(MLA_Q_RANK, H * LANES)
    wkv = w_ukv.reshape(MLA_KV_RANK, H, MLA_NOPE + MLA_V)
    kn, vv = wkv[..., :MLA_NOPE], wkv[..., MLA_NOPE:]
    wk = jnp.concatenate([kn, jnp.zeros((MLA_KV_RANK, H, LANES - MLA_NOPE), w_ukv.dtype)],
                         axis=-1).reshape(MLA_KV_RANK, H * LANES)
    wv = vv.reshape(MLA_KV_RANK, H * MLA_V).T
    return wq1.T.astype(BF16), wq2.T.astype(BF16), wk.astype(BF16), wv.astype(BF16)


def kernel(x, c, positions, w_ada, b_ada, g_norm1, w_in, g_cq, w_uq, g_ckv, w_ukv, g_ret,
           w_o_mla, w_o_ret, w_out, g_norm2, w_router, b_router, w_exp_gate, w_exp_up,
           w_exp_down, w_sh_gate, w_sh_up, w_sh_down, g_final):
    B, S, D = x.shape
    T = B * S
    x2 = x.reshape(T, D)

    mod = _ada(c, w_ada, b_ada)
    mod3 = mod.reshape(B * N_MOD, 1, D)
    cr, sr, cm, sm = _rope_tables(positions)

    z = _inproj(x2, mod3, g_norm1, _pack_w_in(w_in), S)
    wq1, wq2, wk, wv = _pack_mla_weights(w_uq, w_ukv)
    q, k, v = _mla_up(z, cm, sm, g_cq, g_ckv, wq1, wq2, wk, wv)
    o_mla = _attention(q, k, v, B, S)
    o_ret = _retention(z, cr, sr, g_ret, B, S)
    x1 = _merge(x2, o_mla, o_ret, z, mod3, w_o_mla.astype(BF16), w_o_ret.astype(BF16),
                w_out.astype(BF16), S)

    h2, wts, pp, off_rep, pc_rep = _router(x1, mod3, g_norm2, w_router, b_router, S)
    routed = _moe(h2, wts, pp, off_rep[:, 0], pc_rep[:, 0],
                  w_exp_gate.astype(BF16), w_exp_up.astype(BF16), w_exp_down.astype(BF16), n_split=B)
    out = _final(x1, routed, h2, mod3, w_sh_gate.astype(BF16), w_sh_up.astype(BF16),
                 w_sh_down.astype(BF16), g_final, S)
    return out.reshape(B, S, D)
```

```python
import functools

import numpy as np
import jax
import jax.numpy as jnp
from jax import lax
from jax.experimental import pallas as pl
from jax.experimental.pallas import tpu as pltpu

F32 = jnp.float32
BF16 = jnp.bfloat16
I32 = jnp.int32

MLA_HEADS = 8
MLA_Q_RANK = 384
MLA_KV_RANK = 256
MLA_NOPE = 64
MLA_ROPE = 32
MLA_V = 64
RET_HEADS = 4
RET_QK = 128
RET_V = 256
RET_BLOCK = 256
ROPE_THETA = 10000.0
N_EXPERTS = 64
TOP_K = 8
N_GROUPS = 8
TOPK_GROUPS = 4
D_EXPERT = 256
ROUTED_SCALE = 2.5
RMS_EPS = 1e-6
GN_EPS = 1e-5

LANES = 128
SUBLANES = 8
VMEM_LIMIT = 56 * 1024 * 1024

Z_RV, Z_RG, Z_GA, Z_GB = 0, 1024, 2048, 3072
Z_RQ, Z_RK = 4096, 4608
Z_CQKV = 5120
Z_KR1, Z_KR2 = 5760, 5888
Z_COLS = 6144

LOG2E = 1.4426950408889634

MOD_SH1, MOD_SC1, MOD_GT1, MOD_SH2, MOD_SC2, MOD_GT2 = range(6)
N_MOD = 6

ROW_TILE = 512
ROPE_TILE = 1024
INPROJ_TM, INPROJ_TN = 1024, 2048
WPACK_ROWS = 256

ATTN_BQ = 512
ATTN_BK = 512
ATTN_HEADS = 2
ATTN_LROWS = 16

MOE_TB = 1024
MOE_CH = 144
MOE_EPS = 4
MOE_VMEM_LIMIT = 58 * 1024 * 1024


def _cparams(sem, vmem_limit=VMEM_LIMIT):
    return pltpu.CompilerParams(dimension_semantics=sem, vmem_limit_bytes=vmem_limit)


def _mod_spec(D, per_b, which):
    return pl.BlockSpec((None, 1, D), lambda i, *_: ((i // per_b) * N_MOD + which, 0, 0))


def _rms(x):
    return x * lax.rsqrt(jnp.mean(x * x, axis=-1, keepdims=True) + RMS_EPS)


def _silu(x):
    return x * jax.nn.sigmoid(x)


def _load_row_tiles(ref, nrows):
    nchunk = ref.shape[0] // nrows
    return jnp.concatenate([ref[pl.ds(c, nrows, stride=nchunk), :] for c in range(nchunk)], axis=1)


def _store_row_tiles(ref, val):
    nrows, d = val.shape
    nchunk = d // LANES
    for c in range(nchunk):
        ref[pl.ds(c, nrows, stride=nchunk), :] = val[:, c * LANES:(c + 1) * LANES]


def _ada_kernel(c_ref, w_ref, b_ref, o_ref):
    c = c_ref[...]
    o_ref[...] = jnp.dot(_silu(c).astype(BF16), w_ref[...].astype(BF16),
                         preferred_element_type=F32) + b_ref[...]


def _ada(c, w_ada, b_ada):
    B, D = c.shape
    n_out = w_ada.shape[1]
    cp = jnp.zeros((SUBLANES, D), F32).at[:B].set(c)
    tn = D
    out = pl.pallas_call(
        _ada_kernel,
        grid=(n_out // tn,),
        in_specs=[pl.BlockSpec((SUBLANES, D), lambda j: (0, 0)),
                  pl.BlockSpec((D, tn), lambda j: (0, j)),
                  pl.BlockSpec((1, tn), lambda j: (0, j))],
        out_specs=pl.BlockSpec((SUBLANES, tn), lambda j: (0, j)),
        out_shape=jax.ShapeDtypeStruct((SUBLANES, n_out), F32),
        compiler_params=_cparams(("arbitrary",)),
        name="ada_mod",
    )(cp, w_ada, b_ada.reshape(1, n_out))
    return out[:B]


def _rope_kernel(pos_ref, inv_ref, cr_ref, sr_ref, cm_ref, sm_ref):
    ang = pos_ref[...].astype(F32) * inv_ref[...]
    c = jnp.cos(ang)
    s = jnp.sin(ang)
    lane = lax.broadcasted_iota(I32, c.shape, 1)
    half = RET_QK // 2
    cr_ref[...] = jnp.where(lane < half, c, pltpu.roll(c, half, 1))
    sr_ref[...] = jnp.where(lane < half, -s, pltpu.roll(s, half, 1))
    hm = MLA_ROPE // 2
    cm_ref[...] = jnp.where(lane < MLA_NOPE, 1.0,
                            jnp.where(lane < MLA_NOPE + hm, c,
                                      jnp.where(lane < MLA_NOPE + 2 * hm, pltpu.roll(c, hm, 1), 0.0)))
    sm_ref[...] = jnp.where(lane < MLA_NOPE, 0.0,
                            jnp.where(lane < MLA_NOPE + hm, -s,
                                      jnp.where(lane < MLA_NOPE + 2 * hm, pltpu.roll(s, hm, 1), 0.0)))


def _rope_tables(positions):
    T = positions.size
    tm = min(T, ROPE_TILE)
    inv_r = 1.0 / (ROPE_THETA ** (jnp.arange(0, RET_QK, 2, dtype=F32) / RET_QK))
    inv_m = 1.0 / (ROPE_THETA ** (jnp.arange(0, MLA_ROPE, 2, dtype=F32) / MLA_ROPE))
    inv = jnp.zeros((1, LANES), F32).at[0, :RET_QK // 2].set(inv_r)
    inv = inv.at[0, MLA_NOPE:MLA_NOPE + MLA_ROPE // 2].set(inv_m)
    tab = jax.ShapeDtypeStruct((T, LANES), F32)
    spec = pl.BlockSpec((tm, LANES), lambda i: (i, 0))
    return pl.pallas_call(
        _rope_kernel,
        grid=(T // tm,),
        in_specs=[pl.BlockSpec((tm, 1), lambda i: (i, 0)),
                  pl.BlockSpec((1, LANES), lambda i: (0, 0))],
        out_specs=[spec, spec, spec, spec],
        out_shape=[tab, tab, tab, tab],
        compiler_params=_cparams(("arbitrary",)),
        name="rope_tables",
    )(positions.reshape(T, 1), inv)


def _inproj_kernel(x_ref, sc_ref, sh_ref, g_ref, w_ref, z_ref, h_scr):
    @pl.when(pl.program_id(1) == 0)
    def _():
        h = _rms(x_ref[...]) * g_ref[...] * (1.0 + sc_ref[...]) + sh_ref[...]
        h_scr[...] = h.astype(BF16)

    z_ref[...] = jnp.dot(h_scr[...], w_ref[...], preferred_element_type=F32).astype(z_ref.dtype)


def _inproj(x2, mod3, g_norm1, w_pack, S):
    T, D = x2.shape
    N = w_pack.shape[1]
    tm, tn = INPROJ_TM, INPROJ_TN
    per_b = S // tm
    return pl.pallas_call(
        _inproj_kernel,
        grid=(T // tm, N // tn),
        in_specs=[pl.BlockSpec((tm, D), lambda i, j: (i, 0)),
                  _mod_spec(D, per_b, MOD_SC1), _mod_spec(D, per_b, MOD_SH1),
                  pl.BlockSpec((1, D), lambda i, j: (0, 0)),
                  pl.BlockSpec((D, tn), lambda i, j: (0, j))],
        out_specs=pl.BlockSpec((tm, tn), lambda i, j: (i, j)),
        out_shape=jax.ShapeDtypeStruct((T, N), BF16),
        scratch_shapes=[pltpu.VMEM((tm, D), BF16)],
        compiler_params=_cparams(("arbitrary", "arbitrary")),
        name="in_proj",
    )(x2, mod3, mod3, g_norm1.reshape(1, D), w_pack)


def _mla_up_kernel(zc_ref, kr1_ref, kr2_ref, cm_ref, sm_ref, gq_ref, gkv_ref,
                   wq1_ref, wq2_ref, wk_ref, wv_ref, q_ref, k_ref, v_ref):
    zc = zc_ref[...].astype(F32)
    cqn = (_rms(zc[:, :MLA_Q_RANK]) * gq_ref[...]).astype(BF16)
    ckvn = (_rms(zc[:, MLA_Q_RANK:]) * gkv_ref[...]).astype(BF16)
    cm = cm_ref[...]
    sm = sm_ref[...]
    nt = (((1,), (1,)), ((), ()))
    q1t = lax.dot_general(wq1_ref[...], cqn, nt, preferred_element_type=F32)
    q2t = lax.dot_general(wq2_ref[...], cqn, nt, preferred_element_type=F32)
    cmt, smt = cm.T, sm.T
    kn = jnp.dot(ckvn, wk_ref[...], preferred_element_type=F32)
    kpe = kr1_ref[...].astype(F32) * cm + kr2_ref[...].astype(F32) * sm
    qscale = (MLA_NOPE + MLA_ROPE) ** -0.5 * LOG2E
    for h in range(MLA_HEADS):
        sl = slice(h * LANES, (h + 1) * LANES)
        q_ref[sl, :] = ((q1t[sl, :] * cmt + q2t[sl, :] * smt) * qscale).astype(BF16)
        k_ref[:, sl] = (kn[:, sl] + kpe).astype(BF16)
    vt = lax.dot_general(wv_ref[...], ckvn, (((1,), (1,)), ((), ())),
                         preferred_element_type=F32).astype(BF16)
    vrows = MLA_V + ATTN_LROWS
    for h in range(MLA_HEADS):
        v_ref[h * vrows:h * vrows + MLA_V, :] = vt[h * MLA_V:(h + 1) * MLA_V, :]
        v_ref[h * vrows + MLA_V:(h + 1) * vrows, :] = jnp.ones((ATTN_LROWS, vt.shape[1]), BF16)


def _mla_up(z, cm, sm, g_cq, g_ckv, wq1, wq2, wk, wv):
    T = z.shape[0]
    tm = ROW_TILE
    HW = MLA_HEADS * LANES
    wc = MLA_Q_RANK + MLA_KV_RANK
    full = lambda shape: pl.BlockSpec(shape, lambda i: (0, 0))
    return pl.pallas_call(
        _mla_up_kernel,
        grid=(T // tm,),
        in_specs=[pl.BlockSpec((tm, wc), lambda i: (i, Z_CQKV // wc)),
                  pl.BlockSpec((tm, LANES), lambda i: (i, Z_KR1 // LANES)),
                  pl.BlockSpec((tm, LANES), lambda i: (i, Z_KR2 // LANES)),
                  pl.BlockSpec((tm, LANES), lambda i: (i, 0)),
                  pl.BlockSpec((tm, LANES), lambda i: (i, 0)),
                  full((1, MLA_Q_RANK)), full((1, MLA_KV_RANK)),
                  full(wq1.shape), full(wq2.shape), full(wk.shape), full(wv.shape)],
        out_specs=[pl.BlockSpec((HW, tm), lambda i: (0, i)),
                   pl.BlockSpec((tm, HW), lambda i: (i, 0)),
                   pl.BlockSpec((MLA_HEADS * (MLA_V + ATTN_LROWS), tm), lambda i: (0, i))],
        out_shape=[jax.ShapeDtypeStruct((HW, T), BF16),
                   jax.ShapeDtypeStruct((T, HW), BF16),
                   jax.ShapeDtypeStruct((MLA_HEADS * (MLA_V + ATTN_LROWS), T), BF16)],
        compiler_params=_cparams(("arbitrary",)),
        name="mla_up",
    )(z, z, z, cm, sm, g_cq.reshape(1, -1), g_ckv.reshape(1, -1), wq1, wq2, wk, wv)


def _attn_kernel(q_ref, k_ref, vt_ref, o_ref, *scr, bq, bk):
    nh = ATTN_HEADS
    slots = (scr[:nh], scr[nh:2 * nh])
    p_scrs, acc_scr, m_scr = scr[2 * nh:3 * nh], scr[3 * nh], scr[3 * nh + 1]
    mx_scrs = scr[3 * nh + 2:3 * nh + 4]
    qi = pl.program_id(2)
    vrows = MLA_V + ATTN_LROWS
    qts = [q_ref[h * LANES:(h + 1) * LANES, :] for h in range(nh)]
    acc_scr[...] = jnp.zeros(acc_scr.shape, F32)
    m_scr[...] = jnp.full(m_scr.shape, -jnp.inf, F32)
    sub8 = lax.broadcasted_iota(I32, (SUBLANES, bq), 0)
    lane8 = lax.broadcasted_iota(I32, (SUBLANES, bq), 1)
    pack = 2 * SUBLANES

    def scores(kb, slot):
        k0 = pl.multiple_of(kb * bk, bk)
        for h in range(nh):
            s = jnp.dot(k_ref[pl.ds(k0, bk), h * LANES:(h + 1) * LANES], qts[h],
                        preferred_element_type=F32)
            slots[slot][h][...] = s
            mx_scrs[slot][h * SUBLANES:(h + 1) * SUBLANES, :] = jnp.max(
                s.reshape(bk // SUBLANES, SUBLANES, bq), axis=0)

    def update(kb, slot, masked):
        k0 = pl.multiple_of(kb * bk, bk)
        for h in range(nh):
            st, pr = slots[slot][h], p_scrs[h]
            if masked:
                parts = [None, None]
                for r in range(bk // SUBLANES):
                    rs = slice(r * SUBLANES, (r + 1) * SUBLANES)
                    x = jnp.where(lane8 >= sub8 + r * SUBLANES, st[rs, :], -jnp.inf)
                    st[rs, :] = x
                    parts[r % 2] = x if parts[r % 2] is None else jnp.maximum(parts[r % 2], x)
                part = jnp.maximum(parts[0], parts[1])
            else:
                part = mx_scrs[slot][h * SUBLANES:(h + 1) * SUBLANES, :]
            m_cur = jnp.max(part, axis=0, keepdims=True)
            m_prev = m_scr[h:h + 1, :]
            m_new = jnp.maximum(m_prev, m_cur)
            m_scr[h:h + 1, :] = m_new
            alpha = jnp.exp2(m_prev - m_new)
            for r in range(bk // pack):
                rs = slice(r * pack, (r + 1) * pack)
                pr[rs, :] = jnp.exp2(st[rs, :] - m_new).astype(BF16)
            rows = slice(h * vrows, (h + 1) * vrows)
            acc_scr[rows, :] = alpha * acc_scr[rows, :] + jnp.dot(
                vt_ref[rows, pl.ds(k0, bk)], pr[...], preferred_element_type=F32)

    scores(qi, 0)

    @pl.when(qi == 0)
    def _():
        update(qi, 0, True)

    @pl.when(qi > 0)
    def _():
        scores(0, 1)
        update(qi, 0, True)
        npairs = (qi - 1) // 2

        def pair(i, carry):
            u = 2 * i
            scores(u + 1, 0)
            update(u, 1, False)
            scores(u + 2, 1)
            update(u + 1, 0, False)
            return carry

        lax.fori_loop(0, npairs, pair, 0)
        u = 2 * npairs
        rem = qi - u

        @pl.when(rem == 1)
        def _():
            update(u, 1, False)

        @pl.when(rem == 2)
        def _():
            scores(u + 1, 0)
            update(u, 1, False)
            update(u + 1, 0, False)

    ot = jnp.concatenate(
        [acc_scr[h * vrows:h * vrows + MLA_V, :]
         / acc_scr[h * vrows + MLA_V:h * vrows + MLA_V + 1, :] for h in range(nh)], axis=0)
    o_ref[...] = ot.T.astype(o_ref.dtype)


def _attention(qt, k, vt, B, S):
    T = k.shape[0]
    bq = min(ATTN_BQ, S)
    bk = min(ATTN_BK, S)
    nq = S // bq
    nh = ATTN_HEADS
    vrows = MLA_V + ATTN_LROWS
    assert bq == bk and nh <= SUBLANES
    kern = functools.partial(_attn_kernel, bq=bq, bk=bk)
    return pl.pallas_call(
        kern,
        grid=(B, MLA_HEADS // nh, nq),
        in_specs=[pl.BlockSpec((nh * LANES, bq), lambda b, p, i: (p, b * nq + i)),
                  pl.BlockSpec((S, nh * LANES), lambda b, p, i: (b, p)),
                  pl.BlockSpec((nh * vrows, S), lambda b, p, i: (p, b))],
        out_specs=pl.BlockSpec((bq, nh * MLA_V), lambda b, p, i: (b * nq + i, p)),
        out_shape=jax.ShapeDtypeStruct((T, MLA_HEADS * MLA_V), BF16),
        scratch_shapes=([pltpu.VMEM((bk, bq), F32)] * (2 * nh) + [pltpu.VMEM((bk, bq), BF16)] * nh
                        + [pltpu.VMEM((nh * vrows, bq), F32), pltpu.VMEM((SUBLANES, bq), F32)]
                        + [pltpu.VMEM((nh * SUBLANES, bq), F32)] * 2),
        compiler_params=_cparams(("arbitrary", "arbitrary", "arbitrary")),
        name="mla_attention",
    )(qt, k, vt)


def _ret_kernel(rq_ref, rk_ref, rv_ref, rg_ref, cr_ref, sr_ref, dec_ref, xi_ref, zeta_ref,
                g_ref, o_ref, state_scr):
    @pl.when(pl.program_id(1) == 0)
    def _():
        state_scr[...] = jnp.zeros(state_scr.shape, F32)

    C = rq_ref.shape[0]
    cr = cr_ref[...]
    sr = sr_ref[...]
    half = RET_QK // 2
    for h in range(RET_HEADS):
        qk = slice(h * RET_QK, (h + 1) * RET_QK)
        vv = slice(h * RET_V, (h + 1) * RET_V)
        rq = rq_ref[:, qk].astype(F32)
        rk = rk_ref[:, qk].astype(F32)
        q = rq * cr + pltpu.roll(rq, half, 1) * sr
        k = (rk * cr + pltpu.roll(rk, half, 1) * sr) * (RET_QK ** -0.5)
        v = rv_ref[:, vv]
        xi = xi_ref[h]
        state = state_scr[h]
        qb = q.astype(BF16)
        s = lax.dot_general(qb, k.astype(BF16), (((1,), (1,)), ((), ())),
                            preferred_element_type=F32) * dec_ref[h]
        inner = jnp.dot(s.astype(BF16), v, preferred_element_type=F32)
        cross = jnp.dot(qb, state.astype(BF16), preferred_element_type=F32) * xi
        kz = (k * zeta_ref[h]).astype(BF16)
        kv = lax.dot_general(kz, v, (((0,), (0,)), ((), ())), preferred_element_type=F32)
        state_scr[h] = xi[C - 1:C, :] * state + kv
        o = inner + cross
        mu = jnp.mean(o, axis=-1, keepdims=True)
        d = o - mu
        var = jnp.mean(d * d, axis=-1, keepdims=True)
        on = d * lax.rsqrt(var + GN_EPS) * g_ref[:, vv]
        o_ref[:, vv] = (_silu(rg_ref[:, vv].astype(F32)) * on).astype(o_ref.dtype)


def _retention_tables():
    C = RET_BLOCK
    h = np.arange(RET_HEADS, dtype=np.float64)
    log_g = np.log(1.0 - np.exp2(-5.0 - h))
    idx = np.arange(C, dtype=np.float64)
    diff = idx[:, None] - idx[None, :]
    decay = np.where(diff[None] >= 0, np.exp(np.maximum(diff, 0.0)[None] * log_g[:, None, None]), 0.0)
    zeta = np.exp((C - 1 - idx)[None, :] * log_g[:, None])
    xi = np.exp((idx + 1.0)[None, :] * log_g[:, None])
    zeta_rep = np.broadcast_to(zeta[:, :, None], (RET_HEADS, C, RET_QK))
    xi_rep = np.broadcast_to(xi[:, :, None], (RET_HEADS, C, RET_V))
    return (jnp.asarray(decay, F32), jnp.asarray(xi_rep, F32), jnp.asarray(zeta_rep, F32))


def _retention(z, cr, sr, g_ret, B, S):
    T = z.shape[0]
    C = RET_BLOCK
    N = S // C
    H = RET_HEADS
    WQ, WV = H * RET_QK, H * RET_V
    dec, xi, zeta = _retention_tables()
    row = lambda b, n: b * N + n
    const = lambda shape: pl.BlockSpec(shape, lambda b, n: (0,) * len(shape))
    return pl.pallas_call(
        _ret_kernel,
        grid=(B, N),
        in_specs=[pl.BlockSpec((C, WQ), lambda b, n: (row(b, n), Z_RQ // WQ)),
                  pl.BlockSpec((C, WQ), lambda b, n: (row(b, n), Z_RK // WQ)),
                  pl.BlockSpec((C, WV), lambda b, n: (row(b, n), Z_RV // WV)),
                  pl.BlockSpec((C, WV), lambda b, n: (row(b, n), Z_RG // WV)),
                  pl.BlockSpec((C, LANES), lambda b, n: (row(b, n), 0)),
                  pl.BlockSpec((C, LANES), lambda b, n: (row(b, n), 0)),
                  const((H, C, C)), const((H, C, RET_V)), const((H, C, RET_QK)), const((1, WV))],
        out_specs=pl.BlockSpec((C, WV), lambda b, n: (row(b, n), 0)),
        out_shape=jax.ShapeDtypeStruct((T, WV), BF16),
        scratch_shapes=[pltpu.VMEM((H, RET_QK, RET_V), F32)],
        compiler_params=_cparams(("arbitrary", "arbitrary")),
        name="retention",
    )(z, z, z, z, cr, sr, dec, xi, zeta, g_ret.reshape(1, -1))


def _merge_kernel(x_ref, oa_ref, ob_ref, ga_ref, gb_ref, gt_ref, wa_ref, wb_ref, wo_ref, x1_ref):
    a = jnp.dot(oa_ref[...], wa_ref[...], preferred_element_type=F32)
    b = jnp.dot(ob_ref[...], wb_ref[...], preferred_element_type=F32)
    merged = (jax.nn.sigmoid(ga_ref[...].astype(F32)) * a
              + jax.nn.sigmoid(gb_ref[...].astype(F32)) * b)
    y = jnp.dot(merged.astype(BF16), wo_ref[...], preferred_element_type=F32)
    x1_ref[...] = x_ref[...] + gt_ref[...] * y


def _merge(x2, o_mla, o_ret, z, mod3, wa, wb, wo, S):
    T, D = x2.shape
    tm = ROW_TILE
    per_b = S // tm
    full = lambda shape: pl.BlockSpec(shape, lambda i: (0, 0))
    return pl.pallas_call(
        _merge_kernel,
        grid=(T // tm,),
        in_specs=[pl.BlockSpec((tm, D), lambda i: (i, 0)),
                  pl.BlockSpec((tm, o_mla.shape[1]), lambda i: (i, 0)),
                  pl.BlockSpec((tm, D), lambda i: (i, 0)),
                  pl.BlockSpec((tm, D), lambda i: (i, Z_GA // D)),
                  pl.BlockSpec((tm, D), lambda i: (i, Z_GB // D)),
                  _mod_spec(D, per_b, MOD_GT1),
                  full(wa.shape), full(wb.shape), full(wo.shape)],
        out_specs=pl.BlockSpec((tm, D), lambda i: (i, 0)),
        out_shape=jax.ShapeDtypeStruct((T, D), F32),
        compiler_params=_cparams(("arbitrary",)),
        name="merge_out",
    )(x2, o_mla, o_ret, z, z, mod3, wa, wb, wo)


def _router_kernel(x1_ref, sc_ref, sh_ref, g_ref, wr_ref, br_ref,
                   h2_ref, wts_ref, pp_ref, off_ref, pc_ref):
    TB = x1_ref.shape[0]
    D = x1_ref.shape[1]
    E, G = N_EXPERTS, N_GROUPS
    per = E // G
    h2 = _rms(x1_ref[...]) * g_ref[...] * (1.0 + sc_ref[...]) + sh_ref[...]
    _store_row_tiles(h2_ref, h2)
    logits = lax.dot_general(wr_ref[...], h2, (((1,), (1,)), ((), ())),
                             precision=lax.Precision.HIGHEST,
                             preferred_element_type=F32)
    s = jax.nn.sigmoid(logits)
    biased = s + br_ref[...]
    sub = lax.broadcasted_iota(I32, (per, TB), 0)
    neg = -jnp.inf

    def first_argmax(vals, m, idx, sentinel):
        return jnp.min(jnp.where(vals == m, idx, sentinel), axis=0, keepdims=True)

    bg = [biased[g * per:(g + 1) * per, :] for g in range(G)]
    sg = [s[g * per:(g + 1) * per, :] for g in range(G)]
    gscore = []
    for g in range(G):
        m1 = jnp.max(bg[g], axis=0, keepdims=True)
        i1 = first_argmax(bg[g], m1, sub, per)
        m2 = jnp.max(jnp.where(sub == i1, neg, bg[g]), axis=0, keepdims=True)
        gscore.append(m1 + m2)
    gs = jnp.concatenate(gscore, axis=0)
    gidx = lax.broadcasted_iota(I32, (G, TB), 0)
    gsel = jnp.zeros((G, TB), F32)
    for _ in range(TOPK_GROUPS):
        m = jnp.max(gs, axis=0, keepdims=True)
        i = first_argmax(gs, m, gidx, G)
        hit = gidx == i
        gsel = jnp.where(hit, 1.0, gsel)
        gs = jnp.where(hit, neg, gs)
    cand = [jnp.where(gsel[g:g + 1, :] > 0.0, bg[g], neg) for g in range(G)]
    eidx = [sub + g * per for g in range(G)]
    sel = [jnp.zeros((per, TB), F32) for _ in range(G)]
    top_i, top_w = [], []
    for _ in range(TOP_K):
        m = functools.reduce(jnp.maximum, [jnp.max(c, axis=0, keepdims=True) for c in cand])
        i = functools.reduce(jnp.minimum,
                             [first_argmax(cand[g], m, eidx[g], E) for g in range(G)])
        w = jnp.zeros((1, TB), F32)
        for g in range(G):
            hit = eidx[g] == i
            w = w + jnp.sum(jnp.where(hit, sg[g], 0.0), axis=0, keepdims=True)
            sel[g] = jnp.where(hit, 1.0, sel[g])
            cand[g] = jnp.where(hit, neg, cand[g])
        top_i.append(i)
        top_w.append(w)
    wsum = functools.reduce(lambda a, b: a + b, top_w)
    wts_ref[...] = jnp.concatenate([w / wsum * ROUTED_SCALE for w in top_w], axis=0)

    mask = jnp.concatenate(sel, axis=0)
    t_row = lax.broadcasted_iota(I32, (TB, TB), 0)
    t_col = lax.broadcasted_iota(I32, (TB, TB), 1)
    before = jnp.where(t_row < t_col, 1.0, 0.0).astype(BF16)
    rank = jnp.dot(mask.astype(BF16), before, preferred_element_type=F32)
    cnt = jnp.sum(mask, axis=1, keepdims=True)
    pc_rep = jnp.broadcast_to(cnt, (E, LANES))
    e_row = lax.broadcasted_iota(I32, (E, E), 0)
    e_col = lax.broadcasted_iota(I32, (E, E), 1)
    lower = jnp.where(e_col < e_row, 1.0, 0.0)
    off_rep = jnp.dot(lower, pc_rep, precision=lax.Precision.HIGHEST,
                      preferred_element_type=F32)
    off_ref[...] = off_rep.astype(I32)
    pc_ref[...] = pc_rep.astype(I32)
    posfull = off_rep[:, :1] + rank
    pos = []
    for kk in range(TOP_K):
        p = jnp.zeros((1, TB), F32)
        for g in range(G):
            p = p + jnp.sum(jnp.where(eidx[g] == top_i[kk], posfull[g * per:(g + 1) * per, :], 0.0),
                            axis=0, keepdims=True)
        pos.append(p.astype(I32))
    pp_ref[...] = jnp.concatenate(pos, axis=0) * (D // LANES)


def _router(x1, mod3, g_norm2, w_router, b_router, S):
    T, D = x1.shape
    TB = min(MOE_TB, T)
    nb = T // TB
    per_b = S // TB
    E = N_EXPERTS
    return pl.pallas_call(
        _router_kernel,
        grid=(nb,),
        in_specs=[pl.BlockSpec((TB, D), lambda i: (i, 0)),
                  _mod_spec(D, per_b, MOD_SC2), _mod_spec(D, per_b, MOD_SH2),
                  pl.BlockSpec((1, D), lambda i: (0, 0)),
                  pl.BlockSpec((E, D), lambda i: (0, 0)),
                  pl.BlockSpec((E, 1), lambda i: (0, 0))],
        out_specs=[pl.BlockSpec((TB * D // LANES, LANES), lambda i: (i, 0)),
                   pl.BlockSpec((TOP_K, TB), lambda i: (0, i)),
                   pl.BlockSpec((TOP_K, TB), lambda i: (0, i)),
                   pl.BlockSpec((E, LANES), lambda i: (i, 0)),
                   pl.BlockSpec((E, LANES), lambda i: (i, 0))],
        out_shape=[jax.ShapeDtypeStruct((T * D // LANES, LANES), F32),
                   jax.ShapeDtypeStruct((TOP_K, T), F32),
                   jax.ShapeDtypeStruct((TOP_K, T), I32),
                   jax.ShapeDtypeStruct((nb * E, LANES), I32),
                   jax.ShapeDtypeStruct((nb * E, LANES), I32)],
        compiler_params=_cparams(("arbitrary",)),
        name="moe_router",
    )(x1, mod3, mod3, g_norm2.reshape(1, D), w_router.T, b_router.reshape(E, 1))


def _moe_kernel(pp_sm, w_sm, off_sm, cnt_sm, h2_ref, wg_ref, wu_ref, wd_ref, *rest, TB):
    out_ref, xs_scr = rest[-2:]
    j = pl.program_id(0)
    step = pl.program_id(1)
    E = N_EXPERTS
    D = wg_ref.shape[1]
    RT = D // LANES
    CH = MOE_CH

    def tile(ref, first):
        return ref.at[pl.ds(pl.multiple_of(first, RT), RT), :]

    def positions(t):
        return [pp_sm[(j * TB + t) * TOP_K + k] for k in range(TOP_K)]

    @pl.when(step == 0)
    def _dispatch():
        xs_scr[TOP_K * TB * RT:, :] = jnp.zeros((2 * CH * RT, LANES), F32)

        def scatter(t, carry):
            row = tile(h2_ref, t * RT)[...]
            for p in positions(t):
                tile(xs_scr, p)[...] = row
            return carry

        lax.fori_loop(0, TB, scatter, 0, unroll=4)

    def run_rows(ee, r0, n_left, nrows):
        blk = xs_scr.at[pl.ds(pl.multiple_of(r0 * RT, RT), nrows * RT), :]
        xin = _load_row_tiles(blk, nrows)
        xb = xin.astype(BF16)
        g = jnp.dot(xb, wg_ref[ee], preferred_element_type=F32)
        u = jnp.dot(xb, wu_ref[ee], preferred_element_type=F32)
        y = jnp.dot((_silu(g) * u).astype(BF16), wd_ref[ee], preferred_element_type=F32)
        rows = lax.broadcasted_iota(I32, (nrows, D), 0)
        _store_row_tiles(blk, jnp.where(rows < n_left, y, xin))

    for ee in range(MOE_EPS):
        e = step * MOE_EPS + ee
        st = off_sm[j * E + e]
        n = cnt_sm[j * E + e]
        nbig = n // (2 * CH)

        def big_body(i, carry, ee=ee, st=st, n=n):
            run_rows(ee, st + i * 2 * CH, n - i * 2 * CH, 2 * CH)
            return carry

        lax.fori_loop(0, nbig, big_body, 0)
        r1 = st + nbig * 2 * CH
        rem = n - nbig * 2 * CH

        @pl.when(rem > CH)
        def _(ee=ee, r1=r1, rem=rem):
            run_rows(ee, r1, rem, 2 * CH)

        @pl.when((rem > 0) & (rem <= CH))
        def _(ee=ee, r1=r1, rem=rem):
            run_rows(ee, r1, rem, CH)

    @pl.when(step == pl.num_programs(1) - 1)
    def _combine():
        def gather(t, carry):
            acc = None
            for k, p in enumerate(positions(t)):
                term = w_sm[(j * TB + t) * TOP_K + k] * tile(xs_scr, p)[...]
                acc = term if acc is None else acc + term
            tile(out_ref, t * RT)[...] = acc
            return carry

        lax.fori_loop(0, TB, gather, 0, unroll=4)


def _moe(h2, wts, pp, off, pc, wg, wu, wd, n_split):
    D = wg.shape[1]
    RT = D // LANES
    T = h2.shape[0] // RT
    E = N_EXPERTS
    EPS = MOE_EPS
    Ts = T // n_split
    TB = min(MOE_TB, Ts)
    nb = Ts // TB
    rows = TOP_K * TB + 2 * MOE_CH
    routed = None
    for s in range(n_split):
        tok = slice(s * Ts, (s + 1) * Ts)
        blk = slice(s * nb * E, (s + 1) * nb * E)
        in_specs = [pl.BlockSpec((TB * RT, LANES), lambda j, e, *_, s=s: (s * nb + j, 0),
                                 pipeline_mode=pl.Buffered(1)),
                    pl.BlockSpec((EPS, D, D_EXPERT), lambda j, e, *_: (e, 0, 0)),
                    pl.BlockSpec((EPS, D, D_EXPERT), lambda j, e, *_: (e, 0, 0)),
                    pl.BlockSpec((EPS, D_EXPERT, D), lambda j, e, *_: (e, 0, 0))]
        args = [pp[:, tok].T.reshape(-1), wts[:, tok].T.reshape(-1), off[blk], pc[blk],
                h2, wg, wu, wd]
        aliases = {}
        if routed is not None:
            in_specs.append(pl.BlockSpec(memory_space=pl.ANY))
            args.append(routed)
            aliases = {len(args) - 1: 0}
        grid_spec = pltpu.PrefetchScalarGridSpec(
            num_scalar_prefetch=4,
            grid=(nb, E // EPS),
            in_specs=in_specs,
            out_specs=pl.BlockSpec((TB * RT, LANES), lambda j, e, *_, s=s: (s * nb + j, 0),
                                   pipeline_mode=pl.Buffered(1)),
            scratch_shapes=[pltpu.VMEM((rows * RT, LANES), F32)],
        )
        routed = pl.pallas_call(
            functools.partial(_moe_kernel, TB=TB),
            grid_spec=grid_spec,
            out_shape=jax.ShapeDtypeStruct((T * RT, LANES), F32),
            input_output_aliases=aliases,
            compiler_params=_cparams(("arbitrary", "arbitrary"), MOE_VMEM_LIMIT),
            name="moe_experts",
        )(*args)
    return routed


def _final_kernel(x1_ref, routed_ref, h2_ref, gt_ref, wsg_ref, wsu_ref, wsd_ref, gf_ref, o_ref):
    tm = x1_ref.shape[0]
    hb = _load_row_tiles(h2_ref, tm).astype(BF16)
    g = jnp.dot(hb, wsg_ref[...], preferred_element_type=F32)
    u = jnp.dot(hb, wsu_ref[...], preferred_element_type=F32)
    shared = jnp.dot((_silu(g) * u).astype(BF16), wsd_ref[...], preferred_element_type=F32)
    xo = x1_ref[...] + gt_ref[...] * (_load_row_tiles(routed_ref, tm) + shared)
    o_ref[...] = _rms(xo) * gf_ref[...]


def _final(x1, routed, h2, mod3, wsg, wsu, wsd, g_final, S):
    T, D = x1.shape
    tm = ROW_TILE
    per_b = S // tm
    full = lambda shape: pl.BlockSpec(shape, lambda i: (0, 0))
    return pl.pallas_call(
        _final_kernel,
        grid=(T // tm,),
        in_specs=[pl.BlockSpec((tm, D), lambda i: (i, 0)),
                  pl.BlockSpec((tm * D // LANES, LANES), lambda i: (i, 0)),
                  pl.BlockSpec((tm * D // LANES, LANES), lambda i: (i, 0)),
                  _mod_spec(D, per_b, MOD_GT2),
                  full(wsg.shape), full(wsu.shape), full(wsd.shape), full((1, D))],
        out_specs=pl.BlockSpec((tm, D), lambda i: (i, 0)),
        out_shape=jax.ShapeDtypeStruct((T, D), F32),
        compiler_params=_cparams(("arbitrary",)),
        name="final_out",
    )(x1, routed, h2, mod3, wsg, wsu, wsd, g_final.reshape(1, D))


def _pack_w_in_kernel(w_ref, o_ref):
    splits = [MLA_Q_RANK, MLA_KV_RANK, MLA_ROPE, RET_HEADS * RET_QK, RET_HEADS * RET_QK,
              RET_HEADS * RET_V, RET_HEADS * RET_V]
    gate_w = (w_ref.shape[1] - sum(splits)) // 2
    splits += [gate_w, gate_w]
    edges = [0] + [int(v) for v in np.cumsum(splits)]
    wcq, wckv, wkr, wrq, wrk, wrv, wrg, wga, wgb = [
        w_ref[:, edges[i]:edges[i + 1]] for i in range(len(splits))]
    rows = w_ref.shape[0]
    hm = MLA_ROPE // 2
    zl = jnp.zeros((rows, MLA_NOPE), F32)
    zr = jnp.zeros((rows, LANES - MLA_NOPE - MLA_ROPE), F32)
    pad = jnp.zeros((rows, Z_COLS - (Z_KR2 + LANES)), F32)
    w = jnp.concatenate([wrv, wrg, wga, wgb, wrq, wrk, wcq, wckv,
                         zl, wkr, zr, zl, wkr[:, hm:], wkr[:, :hm], zr, pad], axis=1)
    o_ref[...] = w.astype(BF16)


def _pack_w_in(w_in):
    D, n_in = w_in.shape
    tr = WPACK_ROWS
    return pl.pallas_call(
        _pack_w_in_kernel,
        grid=(D // tr,),
        in_specs=[pl.BlockSpec((tr, n_in), lambda i: (i, 0))],
        out_specs=pl.BlockSpec((tr, Z_COLS), lambda i: (i, 0)),
        out_shape=jax.ShapeDtypeStruct((D, Z_COLS), BF16),
        compiler_params=_cparams(("arbitrary",)),
        name="pack_w_in",
    )(w_in)


def _pack_mla_weights(w_uq, w_ukv):
    H = MLA_HEADS
    hm = MLA_ROPE // 2
    wq = w_uq.reshape(MLA_Q_RANK, H, MLA_NOPE + MLA_ROPE)
    nope, pe = wq[..., :MLA_NOPE], wq[..., MLA_NOPE:]
    zpad = jnp.zeros((MLA_Q_RANK, H, LANES - MLA_NOPE - MLA_ROPE), w_uq.dtype)
    wq1 = jnp.concatenate([nope, pe, zpad], axis=-1).reshape(MLA_Q_RANK, H * LANES)
    wq2 = jnp.concatenate([jnp.zeros_like(nope), pe[..., hm:], pe[..., :hm], zpad],
                          axis=-1).reshape(MLA_Q_RANK, H * LANES)
    wkv = w_ukv.reshape(MLA_KV_RANK, H, MLA_NOPE + MLA_V)
    kn, vv = wkv[..., :MLA_NOPE], wkv[..., MLA_NOPE:]
    wk = jnp.concatenate([kn, jnp.zeros((MLA_KV_RANK, H, LANES - MLA_NOPE), w_ukv.dtype)],
                         axis=-1).reshape(MLA_KV_RANK, H * LANES)
    wv = vv.reshape(MLA_KV_RANK, H * MLA_V).T
    return wq1.T.astype(BF16), wq2.T.astype(BF16), wk.astype(BF16), wv.astype(BF16)


def kernel(x, c, positions, w_ada, b_ada, g_norm1, w_in, g_cq, w_uq, g_ckv, w_ukv, g_ret,
           w_o_mla, w_o_ret, w_out, g_norm2, w_router, b_router, w_exp_gate, w_exp_up,
           w_exp_down, w_sh_gate, w_sh_up, w_sh_down, g_final):
    B, S, D = x.shape
    T = B * S
    x2 = x.reshape(T, D)

    mod = _ada(c, w_ada, b_ada)
    mod3 = mod.reshape(B * N_MOD, 1, D)
    cr, sr, cm, sm = _rope_tables(positions)

    z = _inproj(x2, mod3, g_norm1, _pack_w_in(w_in), S)
    wq1, wq2, wk, wv = _pack_mla_weights(w_uq, w_ukv)
    q, k, v = _mla_up(z, cm, sm, g_cq, g_ckv, wq1, wq2, wk, wv)
    o_mla = _attention(q, k, v, B, S)
    o_ret = _retention(z, cr, sr, g_ret, B, S)
    x1 = _merge(x2, o_mla, o_ret, z, mod3, w_o_mla.astype(BF16), w_o_ret.astype(BF16),
                w_out.astype(BF16), S)

    h2, wts, pp, off_rep, pc_rep = _router(x1, mod3, g_norm2, w_router, b_router, S)
    routed = _moe(h2, wts, pp, off_rep[:, 0], pc_rep[:, 0],
                  w_exp_gate.astype(BF16), w_exp_up.astype(BF16), w_exp_down.astype(BF16), n_split=B)
    out = _final(x1, routed, h2, mod3, w_sh_gate.astype(BF16), w_sh_up.astype(BF16),
                 w_sh_down.astype(BF16), g_final, S)
    return out.reshape(B, S, D)
```

```python
import functools

import numpy as np
import jax
import jax.numpy as jnp
from jax import lax
from jax.experimental import pallas as pl
from jax.experimental.pallas import tpu as pltpu

F32 = jnp.float32
BF16 = jnp.bfloat16
I32 = jnp.int32

MLA_HEADS = 8
MLA_Q_RANK = 384
MLA_KV_RANK = 256
MLA_NOPE = 64
MLA_ROPE = 32
MLA_V = 64
RET_HEADS = 4
RET_QK = 128
RET_V = 256
RET_BLOCK = 256
ROPE_THETA = 10000.0
N_EXPERTS = 64
TOP_K = 8
N_GROUPS = 8
TOPK_GROUPS = 4
D_EXPERT = 256
ROUTED_SCALE = 2.5
RMS_EPS = 1e-6
GN_EPS = 1e-5

LANES = 128
SUBLANES = 8
VMEM_LIMIT = 56 * 1024 * 1024

Z_RV, Z_RG, Z_GA, Z_GB = 0, 1024, 2048, 3072
Z_RQ, Z_RK = 4096, 4608
Z_CQKV = 5120
Z_KR1, Z_KR2 = 5760, 5888
Z_COLS = 6144

LOG2E = 1.4426950408889634

MOD_SH1, MOD_SC1, MOD_GT1, MOD_SH2, MOD_SC2, MOD_GT2 = range(6)
N_MOD = 6

ROW_TILE = 512
ROPE_TILE = 1024
INPROJ_TM, INPROJ_TN = 1024, 2048
WPACK_ROWS = 256

ATTN_BQ = 512
ATTN_BK = 512
ATTN_HEADS = 2
ATTN_LROWS = 16

MOE_TB = 1024
MOE_CH = 144
MOE_EPS = 4
MOE_VMEM_LIMIT = 58 * 1024 * 1024


def _cparams(sem, vmem_limit=VMEM_LIMIT):
    return pltpu.CompilerParams(dimension_semantics=sem, vmem_limit_bytes=vmem_limit)


def _mod_spec(D, per_b, which):
    return pl.BlockSpec((None, 1, D), lambda i, *_: ((i // per_b) * N_MOD + which, 0, 0))


def _rms(x):
    return x * lax.rsqrt(jnp.mean(x * x, axis=-1, keepdims=True) + RMS_EPS)


def _silu(x):
    return x * jax.nn.sigmoid(x)


def _load_row_tiles(ref, nrows):
    nchunk = ref.shape[0] // nrows
    return jnp.concatenate([ref[pl.ds(c, nrows, stride=nchunk), :] for c in range(nchunk)], axis=1)


def _store_row_tiles(ref, val):
    nrows, d = val.shape
    nchunk = d // LANES
    for c in range(nchunk):
        ref[pl.ds(c, nrows, stride=nchunk), :] = val[:, c * LANES:(c + 1) * LANES]


def _ada_kernel(c_ref, w_ref, b_ref, o_ref):
    c = c_ref[...]
    o_ref[...] = jnp.dot(_silu(c).astype(BF16), w_ref[...].astype(BF16),
                         preferred_element_type=F32) + b_ref[...]


def _ada(c, w_ada, b_ada):
    B, D = c.shape
    n_out = w_ada.shape[1]
    cp = jnp.zeros((SUBLANES, D), F32).at[:B].set(c)
    tn = D
    out = pl.pallas_call(
        _ada_kernel,
        grid=(n_out // tn,),
        in_specs=[pl.BlockSpec((SUBLANES, D), lambda j: (0, 0)),
                  pl.BlockSpec((D, tn), lambda j: (0, j)),
                  pl.BlockSpec((1, tn), lambda j: (0, j))],
        out_specs=pl.BlockSpec((SUBLANES, tn), lambda j: (0, j)),
        out_shape=jax.ShapeDtypeStruct((SUBLANES, n_out), F32),
        compiler_params=_cparams(("arbitrary",)),
        name="ada_mod",
    )(cp, w_ada, b_ada.reshape(1, n_out))
    return out[:B]


def _rope_kernel(pos_ref, inv_ref, cr_ref, sr_ref, cm_ref, sm_ref):
    ang = pos_ref[...].astype(F32) * inv_ref[...]
    c = jnp.cos(ang)
    s = jnp.sin(ang)
    lane = lax.broadcasted_iota(I32, c.shape, 1)
    half = RET_QK // 2
    cr_ref[...] = jnp.where(lane < half, c, pltpu.roll(c, half, 1))
    sr_ref[...] = jnp.where(lane < half, -s, pltpu.roll(s, half, 1))
    hm = MLA_ROPE // 2
    cm_ref[...] = jnp.where(lane < MLA_NOPE, 1.0,
                            jnp.where(lane < MLA_NOPE + hm, c,
                                      jnp.where(lane < MLA_NOPE + 2 * hm, pltpu.roll(c, hm, 1), 0.0)))
    sm_ref[...] = jnp.where(lane < MLA_NOPE, 0.0,
                            jnp.where(lane < MLA_NOPE + hm, -s,
                                      jnp.where(lane < MLA_NOPE + 2 * hm, pltpu.roll(s, hm, 1), 0.0)))


def _rope_tables(positions):
    T = positions.size
    tm = min(T, ROPE_TILE)
    inv_r = 1.0 / (ROPE_THETA ** (jnp.arange(0, RET_QK, 2, dtype=F32) / RET_QK))
    inv_m = 1.0 / (ROPE_THETA ** (jnp.arange(0, MLA_ROPE, 2, dtype=F32) / MLA_ROPE))
    inv = jnp.zeros((1, LANES), F32).at[0, :RET_QK // 2].set(inv_r)
    inv = inv.at[0, MLA_NOPE:MLA_NOPE + MLA_ROPE // 2].set(inv_m)
    tab = jax.ShapeDtypeStruct((T, LANES), F32)
    spec = pl.BlockSpec((tm, LANES), lambda i: (i, 0))
    return pl.pallas_call(
        _rope_kernel,
        grid=(T // tm,),
        in_specs=[pl.BlockSpec((tm, 1), lambda i: (i, 0)),
                  pl.BlockSpec((1, LANES), lambda i: (0, 0))],
        out_specs=[spec, spec, spec, spec],
        out_shape=[tab, tab, tab, tab],
        compiler_params=_cparams(("arbitrary",)),
        name="rope_tables",
    )(positions.reshape(T, 1), inv)


def _inproj_kernel(x_ref, sc_ref, sh_ref, g_ref, w_ref, z_ref, h_scr):
    @pl.when(pl.program_id(1) == 0)
    def _():
        h = _rms(x_ref[...]) * g_ref[...] * (1.0 + sc_ref[...]) + sh_ref[...]
        h_scr[...] = h.astype(BF16)

    z_ref[...] = jnp.dot(h_scr[...], w_ref[...], preferred_element_type=F32).astype(z_ref.dtype)


def _inproj(x2, mod3, g_norm1, w_pack, S):
    T, D = x2.shape
    N = w_pack.shape[1]
    tm, tn = INPROJ_TM, INPROJ_TN
    per_b = S // tm
    return pl.pallas_call(
        _inproj_kernel,
        grid=(T // tm, N // tn),
        in_specs=[pl.BlockSpec((tm, D), lambda i, j: (i, 0)),
                  _mod_spec(D, per_b, MOD_SC1), _mod_spec(D, per_b, MOD_SH1),
                  pl.BlockSpec((1, D), lambda i, j: (0, 0)),
                  pl.BlockSpec((D, tn), lambda i, j: (0, j))],
        out_specs=pl.BlockSpec((tm, tn), lambda i, j: (i, j)),
        out_shape=jax.ShapeDtypeStruct((T, N), BF16),
        scratch_shapes=[pltpu.VMEM((tm, D), BF16)],
        compiler_params=_cparams(("arbitrary", "arbitrary")),
        name="in_proj",
    )(x2, mod3, mod3, g_norm1.reshape(1, D), w_pack)


def _mla_up_kernel(zc_ref, kr1_ref, kr2_ref, cm_ref, sm_ref, gq_ref, gkv_ref,
                   wq1_ref, wq2_ref, wk_ref, wv_ref, q_ref, k_ref, v_ref):
    zc = zc_ref[...].astype(F32)
    cqn = (_rms(zc[:, :MLA_Q_RANK]) * gq_ref[...]).astype(BF16)
    ckvn = (_rms(zc[:, MLA_Q_RANK:]) * gkv_ref[...]).astype(BF16)
    cm = cm_ref[...]
    sm = sm_ref[...]
    nt = (((1,), (1,)), ((), ()))
    q1t = lax.dot_general(wq1_ref[...], cqn, nt, preferred_element_type=F32)
    q2t = lax.dot_general(wq2_ref[...], cqn, nt, preferred_element_type=F32)
    cmt, smt = cm.T, sm.T
    kn = jnp.dot(ckvn, wk_ref[...], preferred_element_type=F32)
    kpe = kr1_ref[...].astype(F32) * cm + kr2_ref[...].astype(F32) * sm
    qscale = (MLA_NOPE + MLA_ROPE) ** -0.5 * LOG2E
    for h in range(MLA_HEADS):
        sl = slice(h * LANES, (h + 1) * LANES)
        q_ref[sl, :] = ((q1t[sl, :] * cmt + q2t[sl, :] * smt) * qscale).astype(BF16)
        k_ref[:, sl] = (kn[:, sl] + kpe).astype(BF16)
    vt = lax.dot_general(wv_ref[...], ckvn, (((1,), (1,)), ((), ())),
                         preferred_element_type=F32).astype(BF16)
    vrows = MLA_V + ATTN_LROWS
    for h in range(MLA_HEADS):
        v_ref[h * vrows:h * vrows + MLA_V, :] = vt[h * MLA_V:(h + 1) * MLA_V, :]
        v_ref[h * vrows + MLA_V:(h + 1) * vrows, :] = jnp.ones((ATTN_LROWS, vt.shape[1]), BF16)


def _mla_up(z, cm, sm, g_cq, g_ckv, wq1, wq2, wk, wv):
    T = z.shape[0]
    tm = ROW_TILE
    HW = MLA_HEADS * LANES
    wc = MLA_Q_RANK + MLA_KV_RANK
    full = lambda shape: pl.BlockSpec(shape, lambda i: (0, 0))
    return pl.pallas_call(
        _mla_up_kernel,
        grid=(T // tm,),
        in_specs=[pl.BlockSpec((tm, wc), lambda i: (i, Z_CQKV // wc)),
                  pl.BlockSpec((tm, LANES), lambda i: (i, Z_KR1 // LANES)),
                  pl.BlockSpec((tm, LANES), lambda i: (i, Z_KR2 // LANES)),
                  pl.BlockSpec((tm, LANES), lambda i: (i, 0)),
                  pl.BlockSpec((tm, LANES), lambda i: (i, 0)),
                  full((1, MLA_Q_RANK)), full((1, MLA_KV_RANK)),
                  full(wq1.shape), full(wq2.shape), full(wk.shape), full(wv.shape)],
        out_specs=[pl.BlockSpec((HW, tm), lambda i: (0, i)),
                   pl.BlockSpec((tm, HW), lambda i: (i, 0)),
                   pl.BlockSpec((MLA_HEADS * (MLA_V + ATTN_LROWS), tm), lambda i: (0, i))],
        out_shape=[jax.ShapeDtypeStruct((HW, T), BF16),
                   jax.ShapeDtypeStruct((T, HW), BF16),
                   jax.ShapeDtypeStruct((MLA_HEADS * (MLA_V + ATTN_LROWS), T), BF16)],
        compiler_params=_cparams(("arbitrary",)),
        name="mla_up",
    )(z, z, z, cm, sm, g_cq.reshape(1, -1), g_ckv.reshape(1, -1), wq1, wq2, wk, wv)


def _attn_kernel(q_ref, k_ref, vt_ref, o_ref, *scr, bq, bk):
    nh = ATTN_HEADS
    slots = (scr[:nh], scr[nh:2 * nh])
    p_scrs, acc_scr, m_scr = scr[2 * nh:3 * nh], scr[3 * nh], scr[3 * nh + 1]
    mx_scrs = scr[3 * nh + 2:3 * nh + 4]
    qi = pl.program_id(2)
    vrows = MLA_V + ATTN_LROWS
    qts = [q_ref[h * LANES:(h + 1) * LANES, :] for h in range(nh)]
    acc_scr[...] = jnp.zeros(acc_scr.shape, F32)
    m_scr[...] = jnp.full(m_scr.shape, -jnp.inf, F32)
    sub8 = lax.broadcasted_iota(I32, (SUBLANES, bq), 0)
    lane8 = lax.broadcasted_iota(I32, (SUBLANES, bq), 1)
    pack = 2 * SUBLANES

    def scores(kb, slot):
        k0 = pl.multiple_of(kb * bk, bk)
        for h in range(nh):
            s = jnp.dot(k_ref[pl.ds(k0, bk), h * LANES:(h + 1) * LANES], qts[h],
                        preferred_element_type=F32)
            slots[slot][h][...] = s
            mx_scrs[slot][h * SUBLANES:(h + 1) * SUBLANES, :] = jnp.max(
                s.reshape(bk // SUBLANES, SUBLANES, bq), axis=0)

    def update(kb, slot, masked):
        k0 = pl.multiple_of(kb * bk, bk)
        for h in range(nh):
            st, pr = slots[slot][h], p_scrs[h]
            if masked:
                parts = [None, None]
                for r in range(bk // SUBLANES):
                    rs = slice(r * SUBLANES, (r + 1) * SUBLANES)
                    x = jnp.where(lane8 >= sub8 + r * SUBLANES, st[rs, :], -jnp.inf)
                    st[rs, :] = x
                    parts[r % 2] = x if parts[r % 2] is None else jnp.maximum(parts[r % 2], x)
                part = jnp.maximum(parts[0], parts[1])
            else:
                part = mx_scrs[slot][h * SUBLANES:(h + 1) * SUBLANES, :]
            m_cur = jnp.max(part, axis=0, keepdims=True)
            m_prev = m_scr[h:h + 1, :]
            m_new = jnp.maximum(m_prev, m_cur)
            m_scr[h:h + 1, :] = m_new
            alpha = jnp.exp2(m_prev - m_new)
            for r in range(bk // pack):
                rs = slice(r * pack, (r + 1) * pack)
                pr[rs, :] = jnp.exp2(st[rs, :] - m_new).astype(BF16)
            rows = slice(h * vrows, (h + 1) * vrows)
            acc_scr[rows, :] = alpha * acc_scr[rows, :] + jnp.dot(
                vt_ref[rows, pl.ds(k0, bk)], pr[...], preferred_element_type=F32)

    scores(0, 0)
    npairs = qi // 2

    def pair(i, carry):
        kb = 2 * i
        scores(kb + 1, 1)
        update(kb, 0, False)
        scores(kb + 2, 0)
        update(kb + 1, 1, False)
        return carry

    lax.fori_loop(0, npairs, pair, 0)

    @pl.when(qi % 2 == 0)
    def _():
        update(qi, 0, True)

    @pl.when(qi % 2 == 1)
    def _():
        scores(qi, 1)
        update(qi - 1, 0, False)
        update(qi, 1, True)

    ot = jnp.concatenate(
        [acc_scr[h * vrows:h * vrows + MLA_V, :]
         / acc_scr[h * vrows + MLA_V:h * vrows + MLA_V + 1, :] for h in range(nh)], axis=0)
    o_ref[...] = ot.T.astype(o_ref.dtype)


def _attention(qt, k, vt, B, S):
    T = k.shape[0]
    bq = min(ATTN_BQ, S)
    bk = min(ATTN_BK, S)
    nq = S // bq
    nh = ATTN_HEADS
    vrows = MLA_V + ATTN_LROWS
    assert bq == bk and nh <= SUBLANES
    kern = functools.partial(_attn_kernel, bq=bq, bk=bk)
    return pl.pallas_call(
        kern,
        grid=(B, MLA_HEADS // nh, nq),
        in_specs=[pl.BlockSpec((nh * LANES, bq), lambda b, p, i: (p, b * nq + i)),
                  pl.BlockSpec((S, nh * LANES), lambda b, p, i: (b, p)),
                  pl.BlockSpec((nh * vrows, S), lambda b, p, i: (p, b))],
        out_specs=pl.BlockSpec((bq, nh * MLA_V), lambda b, p, i: (b * nq + i, p)),
        out_shape=jax.ShapeDtypeStruct((T, MLA_HEADS * MLA_V), BF16),
        scratch_shapes=([pltpu.VMEM((bk, bq), F32)] * (2 * nh) + [pltpu.VMEM((bk, bq), BF16)] * nh
                        + [pltpu.VMEM((nh * vrows, bq), F32), pltpu.VMEM((SUBLANES, bq), F32)]
                        + [pltpu.VMEM((nh * SUBLANES, bq), F32)] * 2),
        compiler_params=_cparams(("arbitrary", "arbitrary", "arbitrary")),
        name="mla_attention",
    )(qt, k, vt)


def _ret_kernel(rq_ref, rk_ref, rv_ref, rg_ref, cr_ref, sr_ref, dec_ref, xi_ref, zeta_ref,
                g_ref, o_ref, state_scr):
    @pl.when(pl.program_id(1) == 0)
    def _():
        state_scr[...] = jnp.zeros(state_scr.shape, F32)

    C = rq_ref.shape[0]
    cr = cr_ref[...]
    sr = sr_ref[...]
    half = RET_QK // 2
    for h in range(RET_HEADS):
        qk = slice(h * RET_QK, (h + 1) * RET_QK)
        vv = slice(h * RET_V, (h + 1) * RET_V)
        rq = rq_ref[:, qk].astype(F32)
        rk = rk_ref[:, qk].astype(F32)
        q = rq * cr + pltpu.roll(rq, half, 1) * sr
        k = (rk * cr + pltpu.roll(rk, half, 1) * sr) * (RET_QK ** -0.5)
        v = rv_ref[:, vv]
        xi = xi_ref[h]
        state = state_scr[h]
        qb = q.astype(BF16)
        s = lax.dot_general(qb, k.astype(BF16), (((1,), (1,)), ((), ())),
                            preferred_element_type=F32) * dec_ref[h]
        inner = jnp.dot(s.astype(BF16), v, preferred_element_type=F32)
        cross = jnp.dot(qb, state.astype(BF16), preferred_element_type=F32) * xi
        kz = (k * zeta_ref[h]).astype(BF16)
        kv = lax.dot_general(kz, v, (((0,), (0,)), ((), ())), preferred_element_type=F32)
        state_scr[h] = xi[C - 1:C, :] * state + kv
        o = inner + cross
        mu = jnp.mean(o, axis=-1, keepdims=True)
        d = o - mu
        var = jnp.mean(d * d, axis=-1, keepdims=True)
        on = d * lax.rsqrt(var + GN_EPS) * g_ref[:, vv]
        o_ref[:, vv] = (_silu(rg_ref[:, vv].astype(F32)) * on).astype(o_ref.dtype)


def _retention_tables():
    C = RET_BLOCK
    h = np.arange(RET_HEADS, dtype=np.float64)
    log_g = np.log(1.0 - np.exp2(-5.0 - h))
    idx = np.arange(C, dtype=np.float64)
    diff = idx[:, None] - idx[None, :]
    decay = np.where(diff[None] >= 0, np.exp(np.maximum(diff, 0.0)[None] * log_g[:, None, None]), 0.0)
    zeta = np.exp((C - 1 - idx)[None, :] * log_g[:, None])
    xi = np.exp((idx + 1.0)[None, :] * log_g[:, None])
    zeta_rep = np.broadcast_to(zeta[:, :, None], (RET_HEADS, C, RET_QK))
    xi_rep = np.broadcast_to(xi[:, :, None], (RET_HEADS, C, RET_V))
    return (jnp.asarray(decay, F32), jnp.asarray(xi_rep, F32), jnp.asarray(zeta_rep, F32))


def _retention(z, cr, sr, g_ret, B, S):
    T = z.shape[0]
    C = RET_BLOCK
    N = S // C
    H = RET_HEADS
    WQ, WV = H * RET_QK, H * RET_V
    dec, xi, zeta = _retention_tables()
    row = lambda b, n: b * N + n
    const = lambda shape: pl.BlockSpec(shape, lambda b, n: (0,) * len(shape))
    return pl.pallas_call(
        _ret_kernel,
        grid=(B, N),
        in_specs=[pl.BlockSpec((C, WQ), lambda b, n: (row(b, n), Z_RQ // WQ)),
                  pl.BlockSpec((C, WQ), lambda b, n: (row(b, n), Z_RK // WQ)),
                  pl.BlockSpec((C, WV), lambda b, n: (row(b, n), Z_RV // WV)),
                  pl.BlockSpec((C, WV), lambda b, n: (row(b, n), Z_RG // WV)),
                  pl.BlockSpec((C, LANES), lambda b, n: (row(b, n), 0)),
                  pl.BlockSpec((C, LANES), lambda b, n: (row(b, n), 0)),
                  const((H, C, C)), const((H, C, RET_V)), const((H, C, RET_QK)), const((1, WV))],
        out_specs=pl.BlockSpec((C, WV), lambda b, n: (row(b, n), 0)),
        out_shape=jax.ShapeDtypeStruct((T, WV), BF16),
        scratch_shapes=[pltpu.VMEM((H, RET_QK, RET_V), F32)],
        compiler_params=_cparams(("arbitrary", "arbitrary")),
        name="retention",
    )(z, z, z, z, cr, sr, dec, xi, zeta, g_ret.reshape(1, -1))


def _merge_kernel(x_ref, oa_ref, ob_ref, ga_ref, gb_ref, gt_ref, wa_ref, wb_ref, wo_ref, x1_ref):
    a = jnp.dot(oa_ref[...], wa_ref[...], preferred_element_type=F32)
    b = jnp.dot(ob_ref[...], wb_ref[...], preferred_element_type=F32)
    merged = (jax.nn.sigmoid(ga_ref[...].astype(F32)) * a
              + jax.nn.sigmoid(gb_ref[...].astype(F32)) * b)
    y = jnp.dot(merged.astype(BF16), wo_ref[...], preferred_element_type=F32)
    x1_ref[...] = x_ref[...] + gt_ref[...] * y


def _merge(x2, o_mla, o_ret, z, mod3, wa, wb, wo, S):
    T, D = x2.shape
    tm = ROW_TILE
    per_b = S // tm
    full = lambda shape: pl.BlockSpec(shape, lambda i: (0, 0))
    return pl.pallas_call(
        _merge_kernel,
        grid=(T // tm,),
        in_specs=[pl.BlockSpec((tm, D), lambda i: (i, 0)),
                  pl.BlockSpec((tm, o_mla.shape[1]), lambda i: (i, 0)),
                  pl.BlockSpec((tm, D), lambda i: (i, 0)),
                  pl.BlockSpec((tm, D), lambda i: (i, Z_GA // D)),
                  pl.BlockSpec((tm, D), lambda i: (i, Z_GB // D)),
                  _mod_spec(D, per_b, MOD_GT1),
                  full(wa.shape), full(wb.shape), full(wo.shape)],
        out_specs=pl.BlockSpec((tm, D), lambda i: (i, 0)),
        out_shape=jax.ShapeDtypeStruct((T, D), F32),
        compiler_params=_cparams(("arbitrary",)),
        name="merge_out",
    )(x2, o_mla, o_ret, z, z, mod3, wa, wb, wo)


def _router_kernel(x1_ref, sc_ref, sh_ref, g_ref, wr_ref, br_ref,
                   h2_ref, wts_ref, pp_ref, off_ref, pc_ref):
    TB = x1_ref.shape[0]
    D = x1_ref.shape[1]
    E, G = N_EXPERTS, N_GROUPS
    per = E // G
    h2 = _rms(x1_ref[...]) * g_ref[...] * (1.0 + sc_ref[...]) + sh_ref[...]
    _store_row_tiles(h2_ref, h2)
    logits = lax.dot_general(wr_ref[...], h2, (((1,), (1,)), ((), ())),
                             precision=lax.Precision.HIGHEST,
                             preferred_element_type=F32)
    s = jax.nn.sigmoid(logits)
    biased = s + br_ref[...]
    sub = lax.broadcasted_iota(I32, (per, TB), 0)
    neg = -jnp.inf

    def first_argmax(vals, m, idx, sentinel):
        return jnp.min(jnp.where(vals == m, idx, sentinel), axis=0, keepdims=True)

    bg = [biased[g * per:(g + 1) * per, :] for g in range(G)]
    sg = [s[g * per:(g + 1) * per, :] for g in range(G)]
    gscore = []
    for g in range(G):
        m1 = jnp.max(bg[g], axis=0, keepdims=True)
        i1 = first_argmax(bg[g], m1, sub, per)
        m2 = jnp.max(jnp.where(sub == i1, neg, bg[g]), axis=0, keepdims=True)
        gscore.append(m1 + m2)
    gs = jnp.concatenate(gscore, axis=0)
    gidx = lax.broadcasted_iota(I32, (G, TB), 0)
    gsel = jnp.zeros((G, TB), F32)
    for _ in range(TOPK_GROUPS):
        m = jnp.max(gs, axis=0, keepdims=True)
        i = first_argmax(gs, m, gidx, G)
        hit = gidx == i
        gsel = jnp.where(hit, 1.0, gsel)
        gs = jnp.where(hit, neg, gs)
    cand = [jnp.where(gsel[g:g + 1, :] > 0.0, bg[g], neg) for g in range(G)]
    eidx = [sub + g * per for g in range(G)]
    sel = [jnp.zeros((per, TB), F32) for _ in range(G)]
    top_i, top_w = [], []
    for _ in range(TOP_K):
        m = functools.reduce(jnp.maximum, [jnp.max(c, axis=0, keepdims=True) for c in cand])
        i = functools.reduce(jnp.minimum,
                             [first_argmax(cand[g], m, eidx[g], E) for g in range(G)])
        w = jnp.zeros((1, TB), F32)
        for g in range(G):
            hit = eidx[g] == i
            w = w + jnp.sum(jnp.where(hit, sg[g], 0.0), axis=0, keepdims=True)
            sel[g] = jnp.where(hit, 1.0, sel[g])
            cand[g] = jnp.where(hit, neg, cand[g])
        top_i.append(i)
        top_w.append(w)
    wsum = functools.reduce(lambda a, b: a + b, top_w)
    wts_ref[...] = jnp.concatenate([w / wsum * ROUTED_SCALE for w in top_w], axis=0)

    mask = jnp.concatenate(sel, axis=0)
    t_row = lax.broadcasted_iota(I32, (TB, TB), 0)
    t_col = lax.broadcasted_iota(I32, (TB, TB), 1)
    before = jnp.where(t_row < t_col, 1.0, 0.0).astype(BF16)
    rank = jnp.dot(mask.astype(BF16), before, preferred_element_type=F32)
    cnt = jnp.sum(mask, axis=1, keepdims=True)
    pc_rep = jnp.broadcast_to(cnt, (E, LANES))
    e_row = lax.broadcasted_iota(I32, (E, E), 0)
    e_col = lax.broadcasted_iota(I32, (E, E), 1)
    lower = jnp.where(e_col < e_row, 1.0, 0.0)
    off_rep = jnp.dot(lower, pc_rep, precision=lax.Precision.HIGHEST,
                      preferred_element_type=F32)
    off_ref[...] = off_rep.astype(I32)
    pc_ref[...] = pc_rep.astype(I32)
    posfull = off_rep[:, :1] + rank
    pos = []
    for kk in range(TOP_K):
        p = jnp.zeros((1, TB), F32)
        for g in range(G):
            p = p + jnp.sum(jnp.where(eidx[g] == top_i[kk], posfull[g * per:(g + 1) * per, :], 0.0),
                            axis=0, keepdims=True)
        pos.append(p.astype(I32))
    pp_ref[...] = jnp.concatenate(pos, axis=0) * (D // LANES)


def _router(x1, mod3, g_norm2, w_router, b_router, S):
    T, D = x1.shape
    TB = min(MOE_TB, T)
    nb = T // TB
    per_b = S // TB
    E = N_EXPERTS
    return pl.pallas_call(
        _router_kernel,
        grid=(nb,),
        in_specs=[pl.BlockSpec((TB, D), lambda i: (i, 0)),
                  _mod_spec(D, per_b, MOD_SC2), _mod_spec(D, per_b, MOD_SH2),
                  pl.BlockSpec((1, D), lambda i: (0, 0)),
                  pl.BlockSpec((E, D), lambda i: (0, 0)),
                  pl.BlockSpec((E, 1), lambda i: (0, 0))],
        out_specs=[pl.BlockSpec((TB * D // LANES, LANES), lambda i: (i, 0)),
                   pl.BlockSpec((TOP_K, TB), lambda i: (0, i)),
                   pl.BlockSpec((TOP_K, TB), lambda i: (0, i)),
                   pl.BlockSpec((E, LANES), lambda i: (i, 0)),
                   pl.BlockSpec((E, LANES), lambda i: (i, 0))],
        out_shape=[jax.ShapeDtypeStruct((T * D // LANES, LANES), F32),
                   jax.ShapeDtypeStruct((TOP_K, T), F32),
                   jax.ShapeDtypeStruct((TOP_K, T), I32),
                   jax.ShapeDtypeStruct((nb * E, LANES), I32),
                   jax.ShapeDtypeStruct((nb * E, LANES), I32)],
        compiler_params=_cparams(("arbitrary",)),
        name="moe_router",
    )(x1, mod3, mod3, g_norm2.reshape(1, D), w_router.T, b_router.reshape(E, 1))


def _moe_kernel(pp_sm, w_sm, off_sm, cnt_sm, h2_ref, wg_ref, wu_ref, wd_ref, *rest, TB):
    out_ref, xs_scr = rest[-2:]
    j = pl.program_id(0)
    step = pl.program_id(1)
    E = N_EXPERTS
    D = wg_ref.shape[1]
    RT = D // LANES
    CH = MOE_CH

    def tile(ref, first):
        return ref.at[pl.ds(pl.multiple_of(first, RT), RT), :]

    def positions(t):
        return [pp_sm[(j * TB + t) * TOP_K + k] for k in range(TOP_K)]

    @pl.when(step == 0)
    def _dispatch():
        xs_scr[TOP_K * TB * RT:, :] = jnp.zeros((2 * CH * RT, LANES), F32)

        def scatter(t, carry):
            row = tile(h2_ref, t * RT)[...]
            for p in positions(t):
                tile(xs_scr, p)[...] = row
            return carry

        lax.fori_loop(0, TB, scatter, 0, unroll=8)

    def run_rows(ee, r0, n_left, nrows):
        blk = xs_scr.at[pl.ds(pl.multiple_of(r0 * RT, RT), nrows * RT), :]
        xin = _load_row_tiles(blk, nrows)
        xb = xin.astype(BF16)
        g = jnp.dot(xb, wg_ref[ee], preferred_element_type=F32)
        u = jnp.dot(xb, wu_ref[ee], preferred_element_type=F32)
        y = jnp.dot((_silu(g) * u).astype(BF16), wd_ref[ee], preferred_element_type=F32)
        rows = lax.broadcasted_iota(I32, (nrows, D), 0)
        _store_row_tiles(blk, jnp.where(rows < n_left, y, xin))

    for ee in range(MOE_EPS):
        e = step * MOE_EPS + ee
        st = off_sm[j * E + e]
        n = cnt_sm[j * E + e]
        nbig = n // (2 * CH)

        def big_body(i, carry, ee=ee, st=st, n=n):
            run_rows(ee, st + i * 2 * CH, n - i * 2 * CH, 2 * CH)
            return carry

        lax.fori_loop(0, nbig, big_body, 0)
        r1 = st + nbig * 2 * CH
        rem = n - nbig * 2 * CH

        @pl.when(rem > CH)
        def _(ee=ee, r1=r1, rem=rem):
            run_rows(ee, r1, rem, 2 * CH)

        @pl.when((rem > 0) & (rem <= CH))
        def _(ee=ee, r1=r1, rem=rem):
            run_rows(ee, r1, rem, CH)

    @pl.when(step == pl.num_programs(1) - 1)
    def _combine():
        def gather(t, carry):
            acc = None
            for k, p in enumerate(positions(t)):
                term = w_sm[(j * TB + t) * TOP_K + k] * tile(xs_scr, p)[...]
                acc = term if acc is None else acc + term
            tile(out_ref, t * RT)[...] = acc
            return carry

        lax.fori_loop(0, TB, gather, 0, unroll=8)


def _moe(h2, wts, pp, off, pc, wg, wu, wd, n_split):
    D = wg.shape[1]
    RT = D // LANES
    T = h2.shape[0] // RT
    E = N_EXPERTS
    EPS = MOE_EPS
    Ts = T // n_split
    TB = min(MOE_TB, Ts)
    nb = Ts // TB
    rows = TOP_K * TB + 2 * MOE_CH
    routed = None
    for s in range(n_split):
        tok = slice(s * Ts, (s + 1) * Ts)
        blk = slice(s * nb * E, (s + 1) * nb * E)
        in_specs = [pl.BlockSpec((TB * RT, LANES), lambda j, e, *_, s=s: (s * nb + j, 0),
                                 pipeline_mode=pl.Buffered(1)),
                    pl.BlockSpec((EPS, D, D_EXPERT), lambda j, e, *_: (e, 0, 0)),
                    pl.BlockSpec((EPS, D, D_EXPERT), lambda j, e, *_: (e, 0, 0)),
                    pl.BlockSpec((EPS, D_EXPERT, D), lambda j, e, *_: (e, 0, 0))]
        args = [pp[:, tok].T.reshape(-1), wts[:, tok].T.reshape(-1), off[blk], pc[blk],
                h2, wg, wu, wd]
        aliases = {}
        if routed is not None:
            in_specs.append(pl.BlockSpec(memory_space=pl.ANY))
            args.append(routed)
            aliases = {len(args) - 1: 0}
        grid_spec = pltpu.PrefetchScalarGridSpec(
            num_scalar_prefetch=4,
            grid=(nb, E // EPS),
            in_specs=in_specs,
            out_specs=pl.BlockSpec((TB * RT, LANES), lambda j, e, *_, s=s: (s * nb + j, 0),
                                   pipeline_mode=pl.Buffered(1)),
            scratch_shapes=[pltpu.VMEM((rows * RT, LANES), F32)],
        )
        routed = pl.pallas_call(
            functools.partial(_moe_kernel, TB=TB),
            grid_spec=grid_spec,
            out_shape=jax.ShapeDtypeStruct((T * RT, LANES), F32),
            input_output_aliases=aliases,
            compiler_params=_cparams(("arbitrary", "arbitrary"), MOE_VMEM_LIMIT),
            name="moe_experts",
        )(*args)
    return routed


def _final_kernel(x1_ref, routed_ref, h2_ref, gt_ref, wsg_ref, wsu_ref, wsd_ref, gf_ref, o_ref):
    tm = x1_ref.shape[0]
    hb = _load_row_tiles(h2_ref, tm).astype(BF16)
    g = jnp.dot(hb, wsg_ref[...], preferred_element_type=F32)
    u = jnp.dot(hb, wsu_ref[...], preferred_element_type=F32)
    shared = jnp.dot((_silu(g) * u).astype(BF16), wsd_ref[...], preferred_element_type=F32)
    xo = x1_ref[...] + gt_ref[...] * (_load_row_tiles(routed_ref, tm) + shared)
    o_ref[...] = _rms(xo) * gf_ref[...]


def _final(x1, routed, h2, mod3, wsg, wsu, wsd, g_final, S):
    T, D = x1.shape
    tm = ROW_TILE
    per_b = S // tm
    full = lambda shape: pl.BlockSpec(shape, lambda i: (0, 0))
    return pl.pallas_call(
        _final_kernel,
        grid=(T // tm,),
        in_specs=[pl.BlockSpec((tm, D), lambda i: (i, 0)),
                  pl.BlockSpec((tm * D // LANES, LANES), lambda i: (i, 0)),
                  pl.BlockSpec((tm * D // LANES, LANES), lambda i: (i, 0)),
                  _mod_spec(D, per_b, MOD_GT2),
                  full(wsg.shape), full(wsu.shape), full(wsd.shape), full((1, D))],
        out_specs=pl.BlockSpec((tm, D), lambda i: (i, 0)),
        out_shape=jax.ShapeDtypeStruct((T, D), F32),
        compiler_params=_cparams(("arbitrary",)),
        name="final_out",
    )(x1, routed, h2, mod3, wsg, wsu, wsd, g_final.reshape(1, D))


def _pack_w_in_kernel(w_ref, o_ref):
    splits = [MLA_Q_RANK, MLA_KV_RANK, MLA_ROPE, RET_HEADS * RET_QK, RET_HEADS * RET_QK,
              RET_HEADS * RET_V, RET_HEADS * RET_V]
    gate_w = (w_ref.shape[1] - sum(splits)) // 2
    splits += [gate_w, gate_w]
    edges = [0] + [int(v) for v in np.cumsum(splits)]
    wcq, wckv, wkr, wrq, wrk, wrv, wrg, wga, wgb = [
        w_ref[:, edges[i]:edges[i + 1]] for i in range(len(splits))]
    rows = w_ref.shape[0]
    hm = MLA_ROPE // 2
    zl = jnp.zeros((rows, MLA_NOPE), F32)
    zr = jnp.zeros((rows, LANES - MLA_NOPE - MLA_ROPE), F32)
    pad = jnp.zeros((rows, Z_COLS - (Z_KR2 + LANES)), F32)
    w = jnp.concatenate([wrv, wrg, wga, wgb, wrq, wrk, wcq, wckv,
                         zl, wkr, zr, zl, wkr[:, hm:], wkr[:, :hm], zr, pad], axis=1)
    o_ref[...] = w.astype(BF16)


def _pack_w_in(w_in):
    D, n_in = w_in.shape
    tr = WPACK_ROWS
    return pl.pallas_call(
        _pack_w_in_kernel,
        grid=(D // tr,),
        in_specs=[pl.BlockSpec((tr, n_in), lambda i: (i, 0))],
        out_specs=pl.BlockSpec((tr, Z_COLS), lambda i: (i, 0)),
        out_shape=jax.ShapeDtypeStruct((D, Z_COLS), BF16),
        compiler_params=_cparams(("arbitrary",)),
        name="pack_w_in",
    )(w_in)


def _pack_mla_weights(w_uq, w_ukv):
    H = MLA_HEADS
    hm = MLA_ROPE // 2
    wq = w_uq.reshape(MLA_Q_RANK, H, MLA_NOPE + MLA_ROPE)
    nope, pe = wq[..., :MLA_NOPE], wq[..., MLA_NOPE:]
    zpad = jnp.zeros((MLA_Q_RANK, H, LANES - MLA_NOPE - MLA_ROPE), w_uq.dtype)
    wq1 = jnp.concatenate([nope, pe, zpad], axis=-1).reshape(MLA_Q_RANK, H * LANES)
    wq2 = jnp.concatenate([jnp.zeros_like(nope), pe[..., hm:], pe[..., :hm], zpad],
                          axis=-1).reshape(MLA_Q_RANK, H * LANES)
    wkv = w_ukv.reshape(MLA_KV_RANK, H, MLA_NOPE + MLA_V)
    kn, vv = wkv[..., :MLA_NOPE], wkv[..., MLA_NOPE:]
    wk = jnp.concatenate([kn, jnp.zeros((MLA_KV_RANK, H, LANES - MLA_NOPE), w_ukv.dtype)],
                         axis=-1).reshape(MLA_KV_RANK, H * LANES)
    wv = vv.reshape(MLA_KV_RANK, H * MLA_V).T
    return wq1.T.astype(BF16), wq2.T.astype(BF16), wk.astype(BF16), wv.astype(BF16)


def kernel(x, c, positions, w_ada, b_ada, g_norm1, w_in, g_cq, w_uq, g_ckv, w_ukv, g_ret,
           w_o_mla, w_o_ret, w_out, g_norm2, w_router, b_router, w_exp_gate, w_exp_up,
           w_exp_down, w_sh_gate, w_sh_up, w_sh_down, g_final):
    B, S, D = x.shape
    T = B * S
    x2 = x.reshape(T, D)

    mod = _ada(c, w_ada, b_ada)
    mod3 = mod.reshape(B * N_MOD, 1, D)
    cr, sr, cm, sm = _rope_tables(positions)

    z = _inproj(x2, mod3, g_norm1, _pack_w_in(w_in), S)
    wq1, wq2, wk, wv = _pack_mla_weights(w_uq, w_ukv)
    q, k, v = _mla_up(z, cm, sm, g_cq, g_ckv, wq1, wq2, wk, wv)
    o_mla = _attention(q, k, v, B, S)
    o_ret = _retention(z, cr, sr, g_ret, B, S)
    x1 = _merge(x2, o_mla, o_ret, z, mod3, w_o_mla.astype(BF16), w_o_ret.astype(BF16),
                w_out.astype(BF16), S)

    h2, wts, pp, off_rep, pc_rep = _router(x1, mod3, g_norm2, w_router, b_router, S)
    routed = _moe(h2, wts, pp, off_rep[:, 0], pc_rep[:, 0],
                  w_exp_gate.astype(BF16), w_exp_up.astype(BF16), w_exp_down.astype(BF16), n_split=B)
    out = _final(x1, routed, h2, mod3, w_sh_gate.astype(BF16), w_sh_up.astype(BF16),
                 w_sh_down.astype(BF16), g_final, S)
    return out.reshape(B, S, D)
```

```python
import functools

import numpy as np
import jax
import jax.numpy as jnp
from jax import lax
from jax.experimental import pallas as pl
from jax.experimental.pallas import tpu as pltpu

F32 = jnp.float32
BF16 = jnp.bfloat16
I32 = jnp.int32

MLA_HEADS = 8
MLA_Q_RANK = 384
MLA_KV_RANK = 256
MLA_NOPE = 64
MLA_ROPE = 32
MLA_V = 64
RET_HEADS = 4
RET_QK = 128
RET_V = 256
RET_BLOCK = 256
ROPE_THETA = 10000.0
N_EXPERTS = 64
TOP_K = 8
N_GROUPS = 8
TOPK_GROUPS = 4
D_EXPERT = 256
ROUTED_SCALE = 2.5
RMS_EPS = 1e-6
GN_EPS = 1e-5

LANES = 128
SUBLANES = 8
VMEM_LIMIT = 56 * 1024 * 1024

Z_RV, Z_RG, Z_GA, Z_GB = 0, 1024, 2048, 3072
Z_RQ, Z_RK = 4096, 4608
Z_CQKV = 5120
Z_KR1, Z_KR2 = 5760, 5888
Z_COLS = 6144

LOG2E = 1.4426950408889634

MOD_SH1, MOD_SC1, MOD_GT1, MOD_SH2, MOD_SC2, MOD_GT2 = range(6)
N_MOD = 6

ROW_TILE = 512
FINAL_TILE = 1024
ROPE_TILE = 1024
INPROJ_TM, INPROJ_TN = 1024, 2048
WPACK_ROWS = 256

ATTN_BQ = 512
ATTN_BK = 512
ATTN_HEADS = 2
ATTN_LROWS = 16

MOE_TB = 1024
MOE_CH = 144
MOE_EPS = 4
MOE_VMEM_LIMIT = 58 * 1024 * 1024


def _cparams(sem, vmem_limit=VMEM_LIMIT):
    return pltpu.CompilerParams(dimension_semantics=sem, vmem_limit_bytes=vmem_limit)


def _mod_spec(D, per_b, which):
    return pl.BlockSpec((None, 1, D), lambda i, *_: ((i // per_b) * N_MOD + which, 0, 0))


def _rms(x):
    return x * lax.rsqrt(jnp.mean(x * x, axis=-1, keepdims=True) + RMS_EPS)


def _silu(x):
    return x * jax.nn.sigmoid(x)


def _load_row_tiles(ref, nrows):
    nchunk = ref.shape[0] // nrows
    return jnp.concatenate([ref[pl.ds(c, nrows, stride=nchunk), :] for c in range(nchunk)], axis=1)


def _store_row_tiles(ref, val):
    nrows, d = val.shape
    nchunk = d // LANES
    for c in range(nchunk):
        ref[pl.ds(c, nrows, stride=nchunk), :] = val[:, c * LANES:(c + 1) * LANES]


def _ada_kernel(c_ref, w_ref, b_ref, o_ref):
    c = c_ref[...]
    o_ref[...] = jnp.dot(_silu(c).astype(BF16), w_ref[...].astype(BF16),
                         preferred_element_type=F32) + b_ref[...]


def _ada(c, w_ada, b_ada):
    B, D = c.shape
    n_out = w_ada.shape[1]
    cp = jnp.zeros((SUBLANES, D), F32).at[:B].set(c)
    tn = D
    out = pl.pallas_call(
        _ada_kernel,
        grid=(n_out // tn,),
        in_specs=[pl.BlockSpec((SUBLANES, D), lambda j: (0, 0)),
                  pl.BlockSpec((D, tn), lambda j: (0, j)),
                  pl.BlockSpec((1, tn), lambda j: (0, j))],
        out_specs=pl.BlockSpec((SUBLANES, tn), lambda j: (0, j)),
        out_shape=jax.ShapeDtypeStruct((SUBLANES, n_out), F32),
        compiler_params=_cparams(("arbitrary",)),
        name="ada_mod",
    )(cp, w_ada, b_ada.reshape(1, n_out))
    return out[:B]


def _rope_kernel(pos_ref, inv_ref, cr_ref, sr_ref, cm_ref, sm_ref):
    ang = pos_ref[...].astype(F32) * inv_ref[...]
    c = jnp.cos(ang)
    s = jnp.sin(ang)
    lane = lax.broadcasted_iota(I32, c.shape, 1)
    half = RET_QK // 2
    cr_ref[...] = jnp.where(lane < half, c, pltpu.roll(c, half, 1))
    sr_ref[...] = jnp.where(lane < half, -s, pltpu.roll(s, half, 1))
    hm = MLA_ROPE // 2
    cm_ref[...] = jnp.where(lane < MLA_NOPE, 1.0,
                            jnp.where(lane < MLA_NOPE + hm, c,
                                      jnp.where(lane < MLA_NOPE + 2 * hm, pltpu.roll(c, hm, 1), 0.0)))
    sm_ref[...] = jnp.where(lane < MLA_NOPE, 0.0,
                            jnp.where(lane < MLA_NOPE + hm, -s,
                                      jnp.where(lane < MLA_NOPE + 2 * hm, pltpu.roll(s, hm, 1), 0.0)))


def _rope_tables(positions):
    T = positions.size
    tm = min(T, ROPE_TILE)
    inv_r = 1.0 / (ROPE_THETA ** (jnp.arange(0, RET_QK, 2, dtype=F32) / RET_QK))
    inv_m = 1.0 / (ROPE_THETA ** (jnp.arange(0, MLA_ROPE, 2, dtype=F32) / MLA_ROPE))
    inv = jnp.zeros((1, LANES), F32).at[0, :RET_QK // 2].set(inv_r)
    inv = inv.at[0, MLA_NOPE:MLA_NOPE + MLA_ROPE // 2].set(inv_m)
    tab = jax.ShapeDtypeStruct((T, LANES), F32)
    spec = pl.BlockSpec((tm, LANES), lambda i: (i, 0))
    return pl.pallas_call(
        _rope_kernel,
        grid=(T // tm,),
        in_specs=[pl.BlockSpec((tm, 1), lambda i: (i, 0)),
                  pl.BlockSpec((1, LANES), lambda i: (0, 0))],
        out_specs=[spec, spec, spec, spec],
        out_shape=[tab, tab, tab, tab],
        compiler_params=_cparams(("arbitrary",)),
        name="rope_tables",
    )(positions.reshape(T, 1), inv)


def _inproj_kernel(x_ref, sc_ref, sh_ref, g_ref, w_ref, z_ref, h_scr):
    @pl.when(pl.program_id(1) == 0)
    def _():
        h = _rms(x_ref[...]) * g_ref[...] * (1.0 + sc_ref[...]) + sh_ref[...]
        h_scr[...] = h.astype(BF16)

    z_ref[...] = jnp.dot(h_scr[...], w_ref[...], preferred_element_type=F32).astype(z_ref.dtype)


def _inproj(x2, mod3, g_norm1, w_pack, S):
    T, D = x2.shape
    N = w_pack.shape[1]
    tm, tn = INPROJ_TM, INPROJ_TN
    per_b = S // tm
    return pl.pallas_call(
        _inproj_kernel,
        grid=(T // tm, N // tn),
        in_specs=[pl.BlockSpec((tm, D), lambda i, j: (i, 0)),
                  _mod_spec(D, per_b, MOD_SC1), _mod_spec(D, per_b, MOD_SH1),
                  pl.BlockSpec((1, D), lambda i, j: (0, 0)),
                  pl.BlockSpec((D, tn), lambda i, j: (0, j))],
        out_specs=pl.BlockSpec((tm, tn), lambda i, j: (i, j)),
        out_shape=jax.ShapeDtypeStruct((T, N), BF16),
        scratch_shapes=[pltpu.VMEM((tm, D), BF16)],
        compiler_params=_cparams(("arbitrary", "arbitrary")),
        name="in_proj",
    )(x2, mod3, mod3, g_norm1.reshape(1, D), w_pack)


def _mla_up_kernel(zc_ref, kr1_ref, kr2_ref, cm_ref, sm_ref, gq_ref, gkv_ref,
                   wq1_ref, wq2_ref, wk_ref, wv_ref, q_ref, k_ref, v_ref):
    zc = zc_ref[...].astype(F32)
    cqn = (_rms(zc[:, :MLA_Q_RANK]) * gq_ref[...]).astype(BF16)
    ckvn = (_rms(zc[:, MLA_Q_RANK:]) * gkv_ref[...]).astype(BF16)
    cm = cm_ref[...]
    sm = sm_ref[...]
    nt = (((1,), (1,)), ((), ()))
    q1t = lax.dot_general(wq1_ref[...], cqn, nt, preferred_element_type=F32)
    q2t = lax.dot_general(wq2_ref[...], cqn, nt, preferred_element_type=F32)
    cmt, smt = cm.T, sm.T
    kn = jnp.dot(ckvn, wk_ref[...], preferred_element_type=F32)
    kpe = kr1_ref[...].astype(F32) * cm + kr2_ref[...].astype(F32) * sm
    qscale = (MLA_NOPE + MLA_ROPE) ** -0.5 * LOG2E
    for h in range(MLA_HEADS):
        sl = slice(h * LANES, (h + 1) * LANES)
        q_ref[sl, :] = ((q1t[sl, :] * cmt + q2t[sl, :] * smt) * qscale).astype(BF16)
        k_ref[:, sl] = (kn[:, sl] + kpe).astype(BF16)
    vt = lax.dot_general(wv_ref[...], ckvn, (((1,), (1,)), ((), ())),
                         preferred_element_type=F32).astype(BF16)
    vrows = MLA_V + ATTN_LROWS
    for h in range(MLA_HEADS):
        v_ref[h * vrows:h * vrows + MLA_V, :] = vt[h * MLA_V:(h + 1) * MLA_V, :]
        v_ref[h * vrows + MLA_V:(h + 1) * vrows, :] = jnp.ones((ATTN_LROWS, vt.shape[1]), BF16)


def _mla_up(z, cm, sm, g_cq, g_ckv, wq1, wq2, wk, wv):
    T = z.shape[0]
    tm = ROW_TILE
    HW = MLA_HEADS * LANES
    wc = MLA_Q_RANK + MLA_KV_RANK
    full = lambda shape: pl.BlockSpec(shape, lambda i: (0, 0))
    return pl.pallas_call(
        _mla_up_kernel,
        grid=(T // tm,),
        in_specs=[pl.BlockSpec((tm, wc), lambda i: (i, Z_CQKV // wc)),
                  pl.BlockSpec((tm, LANES), lambda i: (i, Z_KR1 // LANES)),
                  pl.BlockSpec((tm, LANES), lambda i: (i, Z_KR2 // LANES)),
                  pl.BlockSpec((tm, LANES), lambda i: (i, 0)),
                  pl.BlockSpec((tm, LANES), lambda i: (i, 0)),
                  full((1, MLA_Q_RANK)), full((1, MLA_KV_RANK)),
                  full(wq1.shape), full(wq2.shape), full(wk.shape), full(wv.shape)],
        out_specs=[pl.BlockSpec((HW, tm), lambda i: (0, i)),
                   pl.BlockSpec((tm, HW), lambda i: (i, 0)),
                   pl.BlockSpec((MLA_HEADS * (MLA_V + ATTN_LROWS), tm), lambda i: (0, i))],
        out_shape=[jax.ShapeDtypeStruct((HW, T), BF16),
                   jax.ShapeDtypeStruct((T, HW), BF16),
                   jax.ShapeDtypeStruct((MLA_HEADS * (MLA_V + ATTN_LROWS), T), BF16)],
        compiler_params=_cparams(("arbitrary",)),
        name="mla_up",
    )(z, z, z, cm, sm, g_cq.reshape(1, -1), g_ckv.reshape(1, -1), wq1, wq2, wk, wv)


def _attn_kernel(q_ref, k_ref, vt_ref, o_ref, *scr, bq, bk):
    nh = ATTN_HEADS
    slots = (scr[:nh], scr[nh:2 * nh])
    p_scrs, acc_scr, m_scr = scr[2 * nh:3 * nh], scr[3 * nh], scr[3 * nh + 1]
    mx_scrs = scr[3 * nh + 2:3 * nh + 4]
    qi = pl.program_id(2)
    vrows = MLA_V + ATTN_LROWS
    qts = [q_ref[h * LANES:(h + 1) * LANES, :] for h in range(nh)]
    acc_scr[...] = jnp.zeros(acc_scr.shape, F32)
    m_scr[...] = jnp.full(m_scr.shape, -jnp.inf, F32)
    sub8 = lax.broadcasted_iota(I32, (SUBLANES, bq), 0)
    lane8 = lax.broadcasted_iota(I32, (SUBLANES, bq), 1)
    pack = 2 * SUBLANES

    def scores(kb, slot):
        k0 = pl.multiple_of(kb * bk, bk)
        for h in range(nh):
            s = jnp.dot(k_ref[pl.ds(k0, bk), h * LANES:(h + 1) * LANES], qts[h],
                        preferred_element_type=F32)
            slots[slot][h][...] = s
            mx_scrs[slot][h * SUBLANES:(h + 1) * SUBLANES, :] = jnp.max(
                s.reshape(bk // SUBLANES, SUBLANES, bq), axis=0)

    def update(kb, slot, masked):
        k0 = pl.multiple_of(kb * bk, bk)
        for h in range(nh):
            st, pr = slots[slot][h], p_scrs[h]
            if masked:
                parts = [None, None]
                for r in range(bk // SUBLANES):
                    rs = slice(r * SUBLANES, (r + 1) * SUBLANES)
                    x = jnp.where(lane8 >= sub8 + r * SUBLANES, st[rs, :], -jnp.inf)
                    st[rs, :] = x
                    parts[r % 2] = x if parts[r % 2] is None else jnp.maximum(parts[r % 2], x)
                part = jnp.maximum(parts[0], parts[1])
            else:
                part = mx_scrs[slot][h * SUBLANES:(h + 1) * SUBLANES, :]
            m_cur = jnp.max(part, axis=0, keepdims=True)
            m_prev = m_scr[h:h + 1, :]
            m_new = jnp.maximum(m_prev, m_cur)
            m_scr[h:h + 1, :] = m_new
            alpha = jnp.exp2(m_prev - m_new)
            for r in range(bk // pack):
                rs = slice(r * pack, (r + 1) * pack)
                pr[rs, :] = jnp.exp2(st[rs, :] - m_new).astype(BF16)
            rows = slice(h * vrows, (h + 1) * vrows)
            acc_scr[rows, :] = alpha * acc_scr[rows, :] + jnp.dot(
                vt_ref[rows, pl.ds(k0, bk)], pr[...], preferred_element_type=F32)

    scores(0, 0)
    npairs = qi // 2

    def pair(i, carry):
        kb = 2 * i
        scores(kb + 1, 1)
        update(kb, 0, False)
        scores(kb + 2, 0)
        update(kb + 1, 1, False)
        return carry

    lax.fori_loop(0, npairs, pair, 0)

    @pl.when(qi % 2 == 0)
    def _():
        update(qi, 0, True)

    @pl.when(qi % 2 == 1)
    def _():
        scores(qi, 1)
        update(qi - 1, 0, False)
        update(qi, 1, True)

    ot = jnp.concatenate(
        [acc_scr[h * vrows:h * vrows + MLA_V, :]
         / acc_scr[h * vrows + MLA_V:h * vrows + MLA_V + 1, :] for h in range(nh)], axis=0)
    o_ref[...] = ot.T.astype(o_ref.dtype)


def _attention(qt, k, vt, B, S):
    T = k.shape[0]
    bq = min(ATTN_BQ, S)
    bk = min(ATTN_BK, S)
    nq = S // bq
    nh = ATTN_HEADS
    vrows = MLA_V + ATTN_LROWS
    assert bq == bk and nh <= SUBLANES
    kern = functools.partial(_attn_kernel, bq=bq, bk=bk)
    return pl.pallas_call(
        kern,
        grid=(B, MLA_HEADS // nh, nq),
        in_specs=[pl.BlockSpec((nh * LANES, bq), lambda b, p, i: (p, b * nq + i)),
                  pl.BlockSpec((S, nh * LANES), lambda b, p, i: (b, p)),
                  pl.BlockSpec((nh * vrows, S), lambda b, p, i: (p, b))],
        out_specs=pl.BlockSpec((bq, nh * MLA_V), lambda b, p, i: (b * nq + i, p)),
        out_shape=jax.ShapeDtypeStruct((T, MLA_HEADS * MLA_V), BF16),
        scratch_shapes=([pltpu.VMEM((bk, bq), F32)] * (2 * nh) + [pltpu.VMEM((bk, bq), BF16)] * nh
                        + [pltpu.VMEM((nh * vrows, bq), F32), pltpu.VMEM((SUBLANES, bq), F32)]
                        + [pltpu.VMEM((nh * SUBLANES, bq), F32)] * 2),
        compiler_params=_cparams(("arbitrary", "arbitrary", "arbitrary")),
        name="mla_attention",
    )(qt, k, vt)


def _ret_kernel(rq_ref, rk_ref, rv_ref, rg_ref, cr_ref, sr_ref, dec_ref, xi_ref, zeta_ref,
                g_ref, o_ref, state_scr):
    @pl.when(pl.program_id(1) == 0)
    def _():
        state_scr[...] = jnp.zeros(state_scr.shape, F32)

    C = rq_ref.shape[0]
    cr = cr_ref[...]
    sr = sr_ref[...]
    half = RET_QK // 2
    for h in range(RET_HEADS):
        qk = slice(h * RET_QK, (h + 1) * RET_QK)
        vv = slice(h * RET_V, (h + 1) * RET_V)
        rq = rq_ref[:, qk].astype(F32)
        rk = rk_ref[:, qk].astype(F32)
        q = rq * cr + pltpu.roll(rq, half, 1) * sr
        k = (rk * cr + pltpu.roll(rk, half, 1) * sr) * (RET_QK ** -0.5)
        v = rv_ref[:, vv]
        xi = xi_ref[h]
        state = state_scr[h]
        qb = q.astype(BF16)
        s = lax.dot_general(qb, k.astype(BF16), (((1,), (1,)), ((), ())),
                            preferred_element_type=F32) * dec_ref[h]
        inner = jnp.dot(s.astype(BF16), v, preferred_element_type=F32)
        cross = jnp.dot(qb, state.astype(BF16), preferred_element_type=F32) * xi
        kz = (k * zeta_ref[h]).astype(BF16)
        kv = lax.dot_general(kz, v, (((0,), (0,)), ((), ())), preferred_element_type=F32)
        state_scr[h] = xi[C - 1:C, :] * state + kv
        o = inner + cross
        mu = jnp.mean(o, axis=-1, keepdims=True)
        d = o - mu
        var = jnp.mean(d * d, axis=-1, keepdims=True)
        on = d * lax.rsqrt(var + GN_EPS) * g_ref[:, vv]
        o_ref[:, vv] = (_silu(rg_ref[:, vv].astype(F32)) * on).astype(o_ref.dtype)


def _retention_tables():
    C = RET_BLOCK
    h = np.arange(RET_HEADS, dtype=np.float64)
    log_g = np.log(1.0 - np.exp2(-5.0 - h))
    idx = np.arange(C, dtype=np.float64)
    diff = idx[:, None] - idx[None, :]
    decay = np.where(diff[None] >= 0, np.exp(np.maximum(diff, 0.0)[None] * log_g[:, None, None]), 0.0)
    zeta = np.exp((C - 1 - idx)[None, :] * log_g[:, None])
    xi = np.exp((idx + 1.0)[None, :] * log_g[:, None])
    zeta_rep = np.broadcast_to(zeta[:, :, None], (RET_HEADS, C, RET_QK))
    xi_rep = np.broadcast_to(xi[:, :, None], (RET_HEADS, C, RET_V))
    return (jnp.asarray(decay, F32), jnp.asarray(xi_rep, F32), jnp.asarray(zeta_rep, F32))


def _retention(z, cr, sr, g_ret, B, S):
    T = z.shape[0]
    C = RET_BLOCK
    N = S // C
    H = RET_HEADS
    WQ, WV = H * RET_QK, H * RET_V
    dec, xi, zeta = _retention_tables()
    row = lambda b, n: b * N + n
    const = lambda shape: pl.BlockSpec(shape, lambda b, n: (0,) * len(shape))
    return pl.pallas_call(
        _ret_kernel,
        grid=(B, N),
        in_specs=[pl.BlockSpec((C, WQ), lambda b, n: (row(b, n), Z_RQ // WQ)),
                  pl.BlockSpec((C, WQ), lambda b, n: (row(b, n), Z_RK // WQ)),
                  pl.BlockSpec((C, WV), lambda b, n: (row(b, n), Z_RV // WV)),
                  pl.BlockSpec((C, WV), lambda b, n: (row(b, n), Z_RG // WV)),
                  pl.BlockSpec((C, LANES), lambda b, n: (row(b, n), 0)),
                  pl.BlockSpec((C, LANES), lambda b, n: (row(b, n), 0)),
                  const((H, C, C)), const((H, C, RET_V)), const((H, C, RET_QK)), const((1, WV))],
        out_specs=pl.BlockSpec((C, WV), lambda b, n: (row(b, n), 0)),
        out_shape=jax.ShapeDtypeStruct((T, WV), BF16),
        scratch_shapes=[pltpu.VMEM((H, RET_QK, RET_V), F32)],
        compiler_params=_cparams(("arbitrary", "arbitrary")),
        name="retention",
    )(z, z, z, z, cr, sr, dec, xi, zeta, g_ret.reshape(1, -1))


def _merge_kernel(x_ref, oa_ref, ob_ref, ga_ref, gb_ref, gt_ref, wa_ref, wb_ref, wo_ref, x1_ref):
    a = jnp.dot(oa_ref[...], wa_ref[...], preferred_element_type=F32)
    b = jnp.dot(ob_ref[...], wb_ref[...], preferred_element_type=F32)
    merged = (jax.nn.sigmoid(ga_ref[...].astype(F32)) * a
              + jax.nn.sigmoid(gb_ref[...].astype(F32)) * b)
    y = jnp.dot(merged.astype(BF16), wo_ref[...], preferred_element_type=F32)
    x1_ref[...] = x_ref[...] + gt_ref[...] * y


def _merge(x2, o_mla, o_ret, z, mod3, wa, wb, wo, S):
    T, D = x2.shape
    tm = ROW_TILE
    per_b = S // tm
    full = lambda shape: pl.BlockSpec(shape, lambda i: (0, 0))
    return pl.pallas_call(
        _merge_kernel,
        grid=(T // tm,),
        in_specs=[pl.BlockSpec((tm, D), lambda i: (i, 0)),
                  pl.BlockSpec((tm, o_mla.shape[1]), lambda i: (i, 0)),
                  pl.BlockSpec((tm, D), lambda i: (i, 0)),
                  pl.BlockSpec((tm, D), lambda i: (i, Z_GA // D)),
                  pl.BlockSpec((tm, D), lambda i: (i, Z_GB // D)),
                  _mod_spec(D, per_b, MOD_GT1),
                  full(wa.shape), full(wb.shape), full(wo.shape)],
        out_specs=pl.BlockSpec((tm, D), lambda i: (i, 0)),
        out_shape=jax.ShapeDtypeStruct((T, D), F32),
        compiler_params=_cparams(("arbitrary",)),
        name="merge_out",
    )(x2, o_mla, o_ret, z, z, mod3, wa, wb, wo)


def _router_kernel(x1_ref, sc_ref, sh_ref, g_ref, wr_ref, br_ref,
                   h2_ref, wts_ref, pp_ref, off_ref, pc_ref):
    TB = x1_ref.shape[0]
    D = x1_ref.shape[1]
    E, G = N_EXPERTS, N_GROUPS
    per = E // G
    h2 = _rms(x1_ref[...]) * g_ref[...] * (1.0 + sc_ref[...]) + sh_ref[...]
    _store_row_tiles(h2_ref, h2)
    logits = lax.dot_general(wr_ref[...], h2, (((1,), (1,)), ((), ())),
                             precision=lax.Precision.HIGHEST,
                             preferred_element_type=F32)
    s = jax.nn.sigmoid(logits)
    biased = s + br_ref[...]
    sub = lax.broadcasted_iota(I32, (per, TB), 0)
    neg = -jnp.inf

    def first_argmax(vals, m, idx, sentinel):
        return jnp.min(jnp.where(vals == m, idx, sentinel), axis=0, keepdims=True)

    bg = [biased[g * per:(g + 1) * per, :] for g in range(G)]
    sg = [s[g * per:(g + 1) * per, :] for g in range(G)]
    gscore = []
    for g in range(G):
        m1 = jnp.max(bg[g], axis=0, keepdims=True)
        i1 = first_argmax(bg[g], m1, sub, per)
        m2 = jnp.max(jnp.where(sub == i1, neg, bg[g]), axis=0, keepdims=True)
        gscore.append(m1 + m2)
    gs = jnp.concatenate(gscore, axis=0)
    gidx = lax.broadcasted_iota(I32, (G, TB), 0)
    gsel = jnp.zeros((G, TB), F32)
    for _ in range(TOPK_GROUPS):
        m = jnp.max(gs, axis=0, keepdims=True)
        i = first_argmax(gs, m, gidx, G)
        hit = gidx == i
        gsel = jnp.where(hit, 1.0, gsel)
        gs = jnp.where(hit, neg, gs)
    cand = [jnp.where(gsel[g:g + 1, :] > 0.0, bg[g], neg) for g in range(G)]
    eidx = [sub + g * per for g in range(G)]
    sel = [jnp.zeros((per, TB), F32) for _ in range(G)]
    top_i, top_w = [], []
    for _ in range(TOP_K):
        m = functools.reduce(jnp.maximum, [jnp.max(c, axis=0, keepdims=True) for c in cand])
        i = functools.reduce(jnp.minimum,
                             [first_argmax(cand[g], m, eidx[g], E) for g in range(G)])
        w = jnp.zeros((1, TB), F32)
        for g in range(G):
            hit = eidx[g] == i
            w = w + jnp.sum(jnp.where(hit, sg[g], 0.0), axis=0, keepdims=True)
            sel[g] = jnp.where(hit, 1.0, sel[g])
            cand[g] = jnp.where(hit, neg, cand[g])
        top_i.append(i)
        top_w.append(w)
    wsum = functools.reduce(lambda a, b: a + b, top_w)
    wts_ref[...] = jnp.concatenate([w / wsum * ROUTED_SCALE for w in top_w], axis=0)

    mask = jnp.concatenate(sel, axis=0)
    t_row = lax.broadcasted_iota(I32, (TB, TB), 0)
    t_col = lax.broadcasted_iota(I32, (TB, TB), 1)
    before = jnp.where(t_row < t_col, 1.0, 0.0).astype(BF16)
    rank = jnp.dot(mask.astype(BF16), before, preferred_element_type=F32)
    cnt = jnp.sum(mask, axis=1, keepdims=True)
    pc_rep = jnp.broadcast_to(cnt, (E, LANES))
    e_row = lax.broadcasted_iota(I32, (E, E), 0)
    e_col = lax.broadcasted_iota(I32, (E, E), 1)
    lower = jnp.where(e_col < e_row, 1.0, 0.0)
    off_rep = jnp.dot(lower, pc_rep, precision=lax.Precision.HIGHEST,
                      preferred_element_type=F32)
    off_ref[...] = off_rep.astype(I32)
    pc_ref[...] = pc_rep.astype(I32)
    posfull = off_rep[:, :1] + rank
    pos = []
    for kk in range(TOP_K):
        p = jnp.zeros((1, TB), F32)
        for g in range(G):
            p = p + jnp.sum(jnp.where(eidx[g] == top_i[kk], posfull[g * per:(g + 1) * per, :], 0.0),
                            axis=0, keepdims=True)
        pos.append(p.astype(I32))
    pp_ref[...] = jnp.concatenate(pos, axis=0) * (D // LANES)


def _router(x1, mod3, g_norm2, w_router, b_router, S):
    T, D = x1.shape
    TB = min(MOE_TB, T)
    nb = T // TB
    per_b = S // TB
    E = N_EXPERTS
    return pl.pallas_call(
        _router_kernel,
        grid=(nb,),
        in_specs=[pl.BlockSpec((TB, D), lambda i: (i, 0)),
                  _mod_spec(D, per_b, MOD_SC2), _mod_spec(D, per_b, MOD_SH2),
                  pl.BlockSpec((1, D), lambda i: (0, 0)),
                  pl.BlockSpec((E, D), lambda i: (0, 0)),
                  pl.BlockSpec((E, 1), lambda i: (0, 0))],
        out_specs=[pl.BlockSpec((TB * D // LANES, LANES), lambda i: (i, 0)),
                   pl.BlockSpec((TOP_K, TB), lambda i: (0, i)),
                   pl.BlockSpec((TOP_K, TB), lambda i: (0, i)),
                   pl.BlockSpec((E, LANES), lambda i: (i, 0)),
                   pl.BlockSpec((E, LANES), lambda i: (i, 0))],
        out_shape=[jax.ShapeDtypeStruct((T * D // LANES, LANES), F32),
                   jax.ShapeDtypeStruct((TOP_K, T), F32),
                   jax.ShapeDtypeStruct((TOP_K, T), I32),
                   jax.ShapeDtypeStruct((nb * E, LANES), I32),
                   jax.ShapeDtypeStruct((nb * E, LANES), I32)],
        compiler_params=_cparams(("arbitrary",)),
        name="moe_router",
    )(x1, mod3, mod3, g_norm2.reshape(1, D), w_router.T, b_router.reshape(E, 1))


def _moe_kernel(pp_sm, w_sm, off_sm, cnt_sm, h2_ref, wg_ref, wu_ref, wd_ref, *rest, TB):
    out_ref, xs_scr = rest[-2:]
    j = pl.program_id(0)
    step = pl.program_id(1)
    E = N_EXPERTS
    D = wg_ref.shape[1]
    RT = D // LANES
    CH = MOE_CH

    def tile(ref, first):
        return ref.at[pl.ds(pl.multiple_of(first, RT), RT), :]

    def positions(t):
        return [pp_sm[(j * TB + t) * TOP_K + k] for k in range(TOP_K)]

    @pl.when(step == 0)
    def _dispatch():
        xs_scr[TOP_K * TB * RT:, :] = jnp.zeros((2 * CH * RT, LANES), F32)

        def scatter(t, carry):
            row = tile(h2_ref, t * RT)[...]
            for p in positions(t):
                tile(xs_scr, p)[...] = row
            return carry

        lax.fori_loop(0, TB, scatter, 0, unroll=8)

    def run_rows(ee, r0, n_left, nrows):
        blk = xs_scr.at[pl.ds(pl.multiple_of(r0 * RT, RT), nrows * RT), :]
        xin = _load_row_tiles(blk, nrows)
        xb = xin.astype(BF16)
        g = jnp.dot(xb, wg_ref[ee], preferred_element_type=F32)
        u = jnp.dot(xb, wu_ref[ee], preferred_element_type=F32)
        y = jnp.dot((_silu(g) * u).astype(BF16), wd_ref[ee], preferred_element_type=F32)
        rows = lax.broadcasted_iota(I32, (nrows, D), 0)
        _store_row_tiles(blk, jnp.where(rows < n_left, y, xin))

    for ee in range(MOE_EPS):
        e = step * MOE_EPS + ee
        st = off_sm[j * E + e]
        n = cnt_sm[j * E + e]
        nbig = n // (2 * CH)

        def big_body(i, carry, ee=ee, st=st, n=n):
            run_rows(ee, st + i * 2 * CH, n - i * 2 * CH, 2 * CH)
            return carry

        lax.fori_loop(0, nbig, big_body, 0)
        r1 = st + nbig * 2 * CH
        rem = n - nbig * 2 * CH

        @pl.when(rem > CH)
        def _(ee=ee, r1=r1, rem=rem):
            run_rows(ee, r1, rem, 2 * CH)

        @pl.when((rem > 0) & (rem <= CH))
        def _(ee=ee, r1=r1, rem=rem):
            run_rows(ee, r1, rem, CH)

    @pl.when(step == pl.num_programs(1) - 1)
    def _combine():
        def gather(t, carry):
            acc = None
            for k, p in enumerate(positions(t)):
                term = w_sm[(j * TB + t) * TOP_K + k] * tile(xs_scr, p)[...]
                acc = term if acc is None else acc + term
            tile(out_ref, t * RT)[...] = acc
            return carry

        lax.fori_loop(0, TB, gather, 0, unroll=16)


def _moe(h2, wts, pp, off, pc, wg, wu, wd, n_split):
    D = wg.shape[1]
    RT = D // LANES
    T = h2.shape[0] // RT
    E = N_EXPERTS
    EPS = MOE_EPS
    Ts = T // n_split
    TB = min(MOE_TB, Ts)
    nb = Ts // TB
    rows = TOP_K * TB + 2 * MOE_CH
    routed = None
    for s in range(n_split):
        tok = slice(s * Ts, (s + 1) * Ts)
        blk = slice(s * nb * E, (s + 1) * nb * E)
        in_specs = [pl.BlockSpec((TB * RT, LANES), lambda j, e, *_, s=s: (s * nb + j, 0),
                                 pipeline_mode=pl.Buffered(1)),
                    pl.BlockSpec((EPS, D, D_EXPERT), lambda j, e, *_: (e, 0, 0)),
                    pl.BlockSpec((EPS, D, D_EXPERT), lambda j, e, *_: (e, 0, 0)),
                    pl.BlockSpec((EPS, D_EXPERT, D), lambda j, e, *_: (e, 0, 0))]
        args = [pp[:, tok].T.reshape(-1), wts[:, tok].T.reshape(-1), off[blk], pc[blk],
                h2, wg, wu, wd]
        aliases = {}
        if routed is not None:
            in_specs.append(pl.BlockSpec(memory_space=pl.ANY))
            args.append(routed)
            aliases = {len(args) - 1: 0}
        grid_spec = pltpu.PrefetchScalarGridSpec(
            num_scalar_prefetch=4,
            grid=(nb, E // EPS),
            in_specs=in_specs,
            out_specs=pl.BlockSpec((TB * RT, LANES), lambda j, e, *_, s=s: (s * nb + j, 0),
                                   pipeline_mode=pl.Buffered(1)),
            scratch_shapes=[pltpu.VMEM((rows * RT, LANES), F32)],
        )
        routed = pl.pallas_call(
            functools.partial(_moe_kernel, TB=TB),
            grid_spec=grid_spec,
            out_shape=jax.ShapeDtypeStruct((T * RT, LANES), F32),
            input_output_aliases=aliases,
            compiler_params=_cparams(("arbitrary", "arbitrary"), MOE_VMEM_LIMIT),
            name="moe_experts",
        )(*args)
    return routed


def _final_kernel(x1_ref, routed_ref, h2_ref, gt_ref, wsg_ref, wsu_ref, wsd_ref, gf_ref, o_ref):
    tm = x1_ref.shape[0]
    hb = _load_row_tiles(h2_ref, tm).astype(BF16)
    g = jnp.dot(hb, wsg_ref[...], preferred_element_type=F32)
    u = jnp.dot(hb, wsu_ref[...], preferred_element_type=F32)
    shared = jnp.dot((_silu(g) * u).astype(BF16), wsd_ref[...], preferred_element_type=F32)
    xo = x1_ref[...] + gt_ref[...] * (_load_row_tiles(routed_ref, tm) + shared)
    o_ref[...] = _rms(xo) * gf_ref[...]


def _final(x1, routed, h2, mod3, wsg, wsu, wsd, g_final, S):
    T, D = x1.shape
    tm = FINAL_TILE
    per_b = S // tm
    full = lambda shape: pl.BlockSpec(shape, lambda i: (0, 0))
    return pl.pallas_call(
        _final_kernel,
        grid=(T // tm,),
        in_specs=[pl.BlockSpec((tm, D), lambda i: (i, 0)),
                  pl.BlockSpec((tm * D // LANES, LANES), lambda i: (i, 0)),
                  pl.BlockSpec((tm * D // LANES, LANES), lambda i: (i, 0)),
                  _mod_spec(D, per_b, MOD_GT2),
                  full(wsg.shape), full(wsu.shape), full(wsd.shape), full((1, D))],
        out_specs=pl.BlockSpec((tm, D), lambda i: (i, 0)),
        out_shape=jax.ShapeDtypeStruct((T, D), F32),
        compiler_params=_cparams(("arbitrary",)),
        name="final_out",
    )(x1, routed, h2, mod3, wsg, wsu, wsd, g_final.reshape(1, D))


def _pack_w_in_kernel(w_ref, o_ref):
    splits = [MLA_Q_RANK, MLA_KV_RANK, MLA_ROPE, RET_HEADS * RET_QK, RET_HEADS * RET_QK,
              RET_HEADS * RET_V, RET_HEADS * RET_V]
    gate_w = (w_ref.shape[1] - sum(splits)) // 2
    splits += [gate_w, gate_w]
    edges = [0] + [int(v) for v in np.cumsum(splits)]
    wcq, wckv, wkr, wrq, wrk, wrv, wrg, wga, wgb = [
        w_ref[:, edges[i]:edges[i + 1]] for i in range(len(splits))]
    rows = w_ref.shape[0]
    hm = MLA_ROPE // 2
    zl = jnp.zeros((rows, MLA_NOPE), F32)
    zr = jnp.zeros((rows, LANES - MLA_NOPE - MLA_ROPE), F32)
    pad = jnp.zeros((rows, Z_COLS - (Z_KR2 + LANES)), F32)
    w = jnp.concatenate([wrv, wrg, wga, wgb, wrq, wrk, wcq, wckv,
                         zl, wkr, zr, zl, wkr[:, hm:], wkr[:, :hm], zr, pad], axis=1)
    o_ref[...] = w.astype(BF16)


def _pack_w_in(w_in):
    D, n_in = w_in.shape
    tr = WPACK_ROWS
    return pl.pallas_call(
        _pack_w_in_kernel,
        grid=(D // tr,),
        in_specs=[pl.BlockSpec((tr, n_in), lambda i: (i, 0))],
        out_specs=pl.BlockSpec((tr, Z_COLS), lambda i: (i, 0)),
        out_shape=jax.ShapeDtypeStruct((D, Z_COLS), BF16),
        compiler_params=_cparams(("arbitrary",)),
        name="pack_w_in",
    )(w_in)


def _pack_mla_weights(w_uq, w_ukv):
    H = MLA_HEADS
    hm = MLA_ROPE // 2
    wq = w_uq.reshape(MLA_Q_RANK, H, MLA_NOPE + MLA_ROPE)
    nope, pe = wq[..., :MLA_NOPE], wq[..., MLA_NOPE:]
    zpad = jnp.zeros((MLA_Q_RANK, H, LANES - MLA_NOPE - MLA_ROPE), w_uq.dtype)
    wq1 = jnp.concatenate([nope, pe, zpad], axis=-1).reshape(MLA_Q_RANK, H * LANES)
    wq2 = jnp.concatenate([jnp.zeros_like(nope), pe[..., hm:], pe[..., :hm], zpad],
                          axis=-1).reshape(MLA_Q_RANK, H * LANES)
    wkv = w_ukv.reshape(MLA_KV_RANK, H, MLA_NOPE + MLA_V)
    kn, vv = wkv[..., :MLA_NOPE], wkv[..., MLA_NOPE:]
    wk = jnp.concatenate([kn, jnp.zeros((MLA_KV_RANK, H, LANES - MLA_NOPE), w_ukv.dtype)],
                         axis=-1).reshape(MLA_KV_RANK, H * LANES)
    wv = vv.reshape(MLA_KV_RANK, H * MLA_V).T
    return wq1.T.astype(BF16), wq2.T.astype(BF16), wk.astype(BF16), wv.astype(BF16)


def kernel(x, c, positions, w_ada, b_ada, g_norm1, w_in, g_cq, w_uq, g_ckv, w_ukv, g_ret,
           w_o_mla, w_o_ret, w_out, g_norm2, w_router, b_router, w_exp_gate, w_exp_up,
           w_exp_down, w_sh_gate, w_sh_up, w_sh_down, g_final):
    B, S, D = x.shape
    T = B * S
    x2 = x.reshape(T, D)

    mod = _ada(c, w_ada, b_ada)
    mod3 = mod.reshape(B * N_MOD, 1, D)
    cr, sr, cm, sm = _rope_tables(positions)

    z = _inproj(x2, mod3, g_norm1, _pack_w_in(w_in), S)
    wq1, wq2, wk, wv = _pack_mla_weights(w_uq, w_ukv)
    q, k, v = _mla_up(z, cm, sm, g_cq, g_ckv, wq1, wq2, wk, wv)
    o_mla = _attention(q, k, v, B, S)
    o_ret = _retention(z, cr, sr, g_ret, B, S)
    x1 = _merge(x2, o_mla, o_ret, z, mod3, w_o_mla.astype(BF16), w_o_ret.astype(BF16),
                w_out.astype(BF16), S)

    h2, wts, pp, off_rep, pc_rep = _router(x1, mod3, g_norm2, w_router, b_router, S)
    routed = _moe(h2, wts, pp, off_rep[:, 0], pc_rep[:, 0],
                  w_exp_gate.astype(BF16), w_exp_up.astype(BF16), w_exp_down.astype(BF16), n_split=B)
    out = _final(x1, routed, h2, mod3, w_sh_gate.astype(BF16), w_sh_up.astype(BF16),
                 w_sh_down.astype(BF16), g_final, S)
    return out.reshape(B, S, D)
```

```python
import functools

import numpy as np
import jax
import jax.numpy as jnp
from jax import lax
from jax.experimental import pallas as pl
from jax.experimental.pallas import tpu as pltpu

F32 = jnp.float32
BF16 = jnp.bfloat16
I32 = jnp.int32

MLA_HEADS = 8
MLA_Q_RANK = 384
MLA_KV_RANK = 256
MLA_NOPE = 64
MLA_ROPE = 32
MLA_V = 64
RET_HEADS = 4
RET_QK = 128
RET_V = 256
RET_BLOCK = 256
ROPE_THETA = 10000.0
N_EXPERTS = 64
TOP_K = 8
N_GROUPS = 8
TOPK_GROUPS = 4
D_EXPERT = 256
ROUTED_SCALE = 2.5
RMS_EPS = 1e-6
GN_EPS = 1e-5

LANES = 128
SUBLANES = 8
VMEM_LIMIT = 56 * 1024 * 1024

Z_RV, Z_RG, Z_GA, Z_GB = 0, 1024, 2048, 3072
Z_RQ, Z_RK = 4096, 4608
Z_CQKV = 5120
Z_KR1, Z_KR2 = 5760, 5888
Z_COLS = 6144

LOG2E = 1.4426950408889634

MOD_SH1, MOD_SC1, MOD_GT1, MOD_SH2, MOD_SC2, MOD_GT2 = range(6)
N_MOD = 6

ROW_TILE = 512
FINAL_TILE = 1024
ROPE_TILE = 1024
INPROJ_TM, INPROJ_TN = 1024, 2048
WPACK_ROWS = 256

ATTN_BQ = 512
ATTN_BK = 512
ATTN_HEADS = 4
ATTN_LROWS = 16

MOE_TB = 1024
MOE_CH = 144
MOE_EPS = 4
MOE_VMEM_LIMIT = 58 * 1024 * 1024


def _cparams(sem, vmem_limit=VMEM_LIMIT):
    return pltpu.CompilerParams(dimension_semantics=sem, vmem_limit_bytes=vmem_limit)


def _mod_spec(D, per_b, which):
    return pl.BlockSpec((None, 1, D), lambda i, *_: ((i // per_b) * N_MOD + which, 0, 0))


def _rms(x):
    return x * lax.rsqrt(jnp.mean(x * x, axis=-1, keepdims=True) + RMS_EPS)


def _silu(x):
    return x * jax.nn.sigmoid(x)


def _load_row_tiles(ref, nrows):
    nchunk = ref.shape[0] // nrows
    return jnp.concatenate([ref[pl.ds(c, nrows, stride=nchunk), :] for c in range(nchunk)], axis=1)


def _store_row_tiles(ref, val):
    nrows, d = val.shape
    nchunk = d // LANES
    for c in range(nchunk):
        ref[pl.ds(c, nrows, stride=nchunk), :] = val[:, c * LANES:(c + 1) * LANES]


def _ada_kernel(c_ref, w_ref, b_ref, o_ref):
    c = c_ref[...]
    o_ref[...] = jnp.dot(_silu(c).astype(BF16), w_ref[...].astype(BF16),
                         preferred_element_type=F32) + b_ref[...]


def _ada(c, w_ada, b_ada):
    B, D = c.shape
    n_out = w_ada.shape[1]
    cp = jnp.zeros((SUBLANES, D), F32).at[:B].set(c)
    tn = D
    out = pl.pallas_call(
        _ada_kernel,
        grid=(n_out // tn,),
        in_specs=[pl.BlockSpec((SUBLANES, D), lambda j: (0, 0)),
                  pl.BlockSpec((D, tn), lambda j: (0, j)),
                  pl.BlockSpec((1, tn), lambda j: (0, j))],
        out_specs=pl.BlockSpec((SUBLANES, tn), lambda j: (0, j)),
        out_shape=jax.ShapeDtypeStruct((SUBLANES, n_out), F32),
        compiler_params=_cparams(("arbitrary",)),
        name="ada_mod",
    )(cp, w_ada, b_ada.reshape(1, n_out))
    return out[:B]


def _rope_kernel(pos_ref, inv_ref, cr_ref, sr_ref, cm_ref, sm_ref):
    ang = pos_ref[...].astype(F32) * inv_ref[...]
    c = jnp.cos(ang)
    s = jnp.sin(ang)
    lane = lax.broadcasted_iota(I32, c.shape, 1)
    half = RET_QK // 2
    cr_ref[...] = jnp.where(lane < half, c, pltpu.roll(c, half, 1))
    sr_ref[...] = jnp.where(lane < half, -s, pltpu.roll(s, half, 1))
    hm = MLA_ROPE // 2
    cm_ref[...] = jnp.where(lane < MLA_NOPE, 1.0,
                            jnp.where(lane < MLA_NOPE + hm, c,
                                      jnp.where(lane < MLA_NOPE + 2 * hm, pltpu.roll(c, hm, 1), 0.0)))
    sm_ref[...] = jnp.where(lane < MLA_NOPE, 0.0,
                            jnp.where(lane < MLA_NOPE + hm, -s,
                                      jnp.where(lane < MLA_NOPE + 2 * hm, pltpu.roll(s, hm, 1), 0.0)))


def _rope_tables(positions):
    T = positions.size
    tm = min(T, ROPE_TILE)
    inv_r = 1.0 / (ROPE_THETA ** (jnp.arange(0, RET_QK, 2, dtype=F32) / RET_QK))
    inv_m = 1.0 / (ROPE_THETA ** (jnp.arange(0, MLA_ROPE, 2, dtype=F32) / MLA_ROPE))
    inv = jnp.zeros((1, LANES), F32).at[0, :RET_QK // 2].set(inv_r)
    inv = inv.at[0, MLA_NOPE:MLA_NOPE + MLA_ROPE // 2].set(inv_m)
    tab = jax.ShapeDtypeStruct((T, LANES), F32)
    spec = pl.BlockSpec((tm, LANES), lambda i: (i, 0))
    return pl.pallas_call(
        _rope_kernel,
        grid=(T // tm,),
        in_specs=[pl.BlockSpec((tm, 1), lambda i: (i, 0)),
                  pl.BlockSpec((1, LANES), lambda i: (0, 0))],
        out_specs=[spec, spec, spec, spec],
        out_shape=[tab, tab, tab, tab],
        compiler_params=_cparams(("arbitrary",)),
        name="rope_tables",
    )(positions.reshape(T, 1), inv)


def _inproj_kernel(x_ref, sc_ref, sh_ref, g_ref, w_ref, z_ref, h_scr):
    @pl.when(pl.program_id(1) == 0)
    def _():
        h = _rms(x_ref[...]) * g_ref[...] * (1.0 + sc_ref[...]) + sh_ref[...]
        h_scr[...] = h.astype(BF16)

    z_ref[...] = jnp.dot(h_scr[...], w_ref[...], preferred_element_type=F32).astype(z_ref.dtype)


def _inproj(x2, mod3, g_norm1, w_pack, S):
    T, D = x2.shape
    N = w_pack.shape[1]
    tm, tn = INPROJ_TM, INPROJ_TN
    per_b = S // tm
    return pl.pallas_call(
        _inproj_kernel,
        grid=(T // tm, N // tn),
        in_specs=[pl.BlockSpec((tm, D), lambda i, j: (i, 0)),
                  _mod_spec(D, per_b, MOD_SC1), _mod_spec(D, per_b, MOD_SH1),
                  pl.BlockSpec((1, D), lambda i, j: (0, 0)),
                  pl.BlockSpec((D, tn), lambda i, j: (0, j))],
        out_specs=pl.BlockSpec((tm, tn), lambda i, j: (i, j)),
        out_shape=jax.ShapeDtypeStruct((T, N), BF16),
        scratch_shapes=[pltpu.VMEM((tm, D), BF16)],
        compiler_params=_cparams(("arbitrary", "arbitrary")),
        name="in_proj",
    )(x2, mod3, mod3, g_norm1.reshape(1, D), w_pack)


def _mla_up_kernel(zc_ref, kr1_ref, kr2_ref, cm_ref, sm_ref, gq_ref, gkv_ref,
                   wq1_ref, wq2_ref, wk_ref, wv_ref, q_ref, k_ref, v_ref):
    zc = zc_ref[...].astype(F32)
    cqn = (_rms(zc[:, :MLA_Q_RANK]) * gq_ref[...]).astype(BF16)
    ckvn = (_rms(zc[:, MLA_Q_RANK:]) * gkv_ref[...]).astype(BF16)
    cm = cm_ref[...]
    sm = sm_ref[...]
    nt = (((1,), (1,)), ((), ()))
    q1t = lax.dot_general(wq1_ref[...], cqn, nt, preferred_element_type=F32)
    q2t = lax.dot_general(wq2_ref[...], cqn, nt, preferred_element_type=F32)
    cmt, smt = cm.T, sm.T
    kn = jnp.dot(ckvn, wk_ref[...], preferred_element_type=F32)
    kpe = kr1_ref[...].astype(F32) * cm + kr2_ref[...].astype(F32) * sm
    qscale = (MLA_NOPE + MLA_ROPE) ** -0.5 * LOG2E
    for h in range(MLA_HEADS):
        sl = slice(h * LANES, (h + 1) * LANES)
        q_ref[sl, :] = ((q1t[sl, :] * cmt + q2t[sl, :] * smt) * qscale).astype(BF16)
        k_ref[:, sl] = (kn[:, sl] + kpe).astype(BF16)
    vt = lax.dot_general(wv_ref[...], ckvn, (((1,), (1,)), ((), ())),
                         preferred_element_type=F32).astype(BF16)
    vrows = MLA_V + ATTN_LROWS
    for h in range(MLA_HEADS):
        v_ref[h * vrows:h * vrows + MLA_V, :] = vt[h * MLA_V:(h + 1) * MLA_V, :]
        v_ref[h * vrows + MLA_V:(h + 1) * vrows, :] = jnp.ones((ATTN_LROWS, vt.shape[1]), BF16)


def _mla_up(z, cm, sm, g_cq, g_ckv, wq1, wq2, wk, wv):
    T = z.shape[0]
    tm = ROW_TILE
    HW = MLA_HEADS * LANES
    wc = MLA_Q_RANK + MLA_KV_RANK
    full = lambda shape: pl.BlockSpec(shape, lambda i: (0, 0))
    return pl.pallas_call(
        _mla_up_kernel,
        grid=(T // tm,),
        in_specs=[pl.BlockSpec((tm, wc), lambda i: (i, Z_CQKV // wc)),
                  pl.BlockSpec((tm, LANES), lambda i: (i, Z_KR1 // LANES)),
                  pl.BlockSpec((tm, LANES), lambda i: (i, Z_KR2 // LANES)),
                  pl.BlockSpec((tm, LANES), lambda i: (i, 0)),
                  pl.BlockSpec((tm, LANES), lambda i: (i, 0)),
                  full((1, MLA_Q_RANK)), full((1, MLA_KV_RANK)),
                  full(wq1.shape), full(wq2.shape), full(wk.shape), full(wv.shape)],
        out_specs=[pl.BlockSpec((HW, tm), lambda i: (0, i)),
                   pl.BlockSpec((tm, HW), lambda i: (i, 0)),
                   pl.BlockSpec((MLA_HEADS * (MLA_V + ATTN_LROWS), tm), lambda i: (0, i))],
        out_shape=[jax.ShapeDtypeStruct((HW, T), BF16),
                   jax.ShapeDtypeStruct((T, HW), BF16),
                   jax.ShapeDtypeStruct((MLA_HEADS * (MLA_V + ATTN_LROWS), T), BF16)],
        compiler_params=_cparams(("arbitrary",)),
        name="mla_up",
    )(z, z, z, cm, sm, g_cq.reshape(1, -1), g_ckv.reshape(1, -1), wq1, wq2, wk, wv)


def _attn_kernel(q_ref, k_ref, vt_ref, o_ref, *scr, bq, bk):
    nh = ATTN_HEADS
    slots = (scr[:nh], scr[nh:2 * nh])
    p_scrs, acc_scr, m_scr = scr[2 * nh:3 * nh], scr[3 * nh], scr[3 * nh + 1]
    mx_scrs = scr[3 * nh + 2:3 * nh + 4]
    qi = pl.program_id(2)
    vrows = MLA_V + ATTN_LROWS
    qts = [q_ref[h * LANES:(h + 1) * LANES, :] for h in range(nh)]
    acc_scr[...] = jnp.zeros(acc_scr.shape, F32)
    m_scr[...] = jnp.full(m_scr.shape, -jnp.inf, F32)
    sub8 = lax.broadcasted_iota(I32, (SUBLANES, bq), 0)
    lane8 = lax.broadcasted_iota(I32, (SUBLANES, bq), 1)
    pack = 2 * SUBLANES

    def scores(kb, slot):
        k0 = pl.multiple_of(kb * bk, bk)
        for h in range(nh):
            s = jnp.dot(k_ref[pl.ds(k0, bk), h * LANES:(h + 1) * LANES], qts[h],
                        preferred_element_type=F32)
            slots[slot][h][...] = s
            mx_scrs[slot][h * SUBLANES:(h + 1) * SUBLANES, :] = jnp.max(
                s.reshape(bk // SUBLANES, SUBLANES, bq), axis=0)

    def update(kb, slot, masked):
        k0 = pl.multiple_of(kb * bk, bk)
        for h in range(nh):
            st, pr = slots[slot][h], p_scrs[h]
            if masked:
                parts = [None, None]
                for r in range(bk // SUBLANES):
                    rs = slice(r * SUBLANES, (r + 1) * SUBLANES)
                    x = jnp.where(lane8 >= sub8 + r * SUBLANES, st[rs, :], -jnp.inf)
                    st[rs, :] = x
                    parts[r % 2] = x if parts[r % 2] is None else jnp.maximum(parts[r % 2], x)
                part = jnp.maximum(parts[0], parts[1])
            else:
                part = mx_scrs[slot][h * SUBLANES:(h + 1) * SUBLANES, :]
            m_cur = jnp.max(part, axis=0, keepdims=True)
            m_prev = m_scr[h:h + 1, :]
            m_new = jnp.maximum(m_prev, m_cur)
            m_scr[h:h + 1, :] = m_new
            alpha = jnp.exp2(m_prev - m_new)
            for r in range(bk // pack):
                rs = slice(r * pack, (r + 1) * pack)
                pr[rs, :] = jnp.exp2(st[rs, :] - m_new).astype(BF16)
            rows = slice(h * vrows, (h + 1) * vrows)
            acc_scr[rows, :] = alpha * acc_scr[rows, :] + jnp.dot(
                vt_ref[rows, pl.ds(k0, bk)], pr[...], preferred_element_type=F32)

    scores(0, 0)
    npairs = qi // 2

    def pair(i, carry):
        kb = 2 * i
        scores(kb + 1, 1)
        update(kb, 0, False)
        scores(kb + 2, 0)
        update(kb + 1, 1, False)
        return carry

    lax.fori_loop(0, npairs, pair, 0)

    @pl.when(qi % 2 == 0)
    def _():
        update(qi, 0, True)

    @pl.when(qi % 2 == 1)
    def _():
        scores(qi, 1)
        update(qi - 1, 0, False)
        update(qi, 1, True)

    ot = jnp.concatenate(
        [acc_scr[h * vrows:h * vrows + MLA_V, :]
         / acc_scr[h * vrows + MLA_V:h * vrows + MLA_V + 1, :] for h in range(nh)], axis=0)
    o_ref[...] = ot.T.astype(o_ref.dtype)


def _attention(qt, k, vt, B, S):
    T = k.shape[0]
    bq = min(ATTN_BQ, S)
    bk = min(ATTN_BK, S)
    nq = S // bq
    nh = ATTN_HEADS
    vrows = MLA_V + ATTN_LROWS
    assert bq == bk and nh <= SUBLANES
    kern = functools.partial(_attn_kernel, bq=bq, bk=bk)
    return pl.pallas_call(
        kern,
        grid=(B, MLA_HEADS // nh, nq),
        in_specs=[pl.BlockSpec((nh * LANES, bq), lambda b, p, i: (p, b * nq + i)),
                  pl.BlockSpec((S, nh * LANES), lambda b, p, i: (b, p)),
                  pl.BlockSpec((nh * vrows, S), lambda b, p, i: (p, b))],
        out_specs=pl.BlockSpec((bq, nh * MLA_V), lambda b, p, i: (b * nq + i, p)),
        out_shape=jax.ShapeDtypeStruct((T, MLA_HEADS * MLA_V), BF16),
        scratch_shapes=([pltpu.VMEM((bk, bq), F32)] * (2 * nh) + [pltpu.VMEM((bk, bq), BF16)] * nh
                        + [pltpu.VMEM((nh * vrows, bq), F32), pltpu.VMEM((SUBLANES, bq), F32)]
                        + [pltpu.VMEM((nh * SUBLANES, bq), F32)] * 2),
        compiler_params=_cparams(("arbitrary", "arbitrary", "arbitrary")),
        name="mla_attention",
    )(qt, k, vt)


def _ret_kernel(rq_ref, rk_ref, rv_ref, rg_ref, cr_ref, sr_ref, dec_ref, xi_ref, zeta_ref,
                g_ref, o_ref, state_scr):
    @pl.when(pl.program_id(1) == 0)
    def _():
        state_scr[...] = jnp.zeros(state_scr.shape, F32)

    C = rq_ref.shape[0]
    cr = cr_ref[...]
    sr = sr_ref[...]
    half = RET_QK // 2
    for h in range(RET_HEADS):
        qk = slice(h * RET_QK, (h + 1) * RET_QK)
        vv = slice(h * RET_V, (h + 1) * RET_V)
        rq = rq_ref[:, qk].astype(F32)
        rk = rk_ref[:, qk].astype(F32)
        q = rq * cr + pltpu.roll(rq, half, 1) * sr
        k = (rk * cr + pltpu.roll(rk, half, 1) * sr) * (RET_QK ** -0.5)
        v = rv_ref[:, vv]
        xi = xi_ref[h]
        state = state_scr[h]
        qb = q.astype(BF16)
        s = lax.dot_general(qb, k.astype(BF16), (((1,), (1,)), ((), ())),
                            preferred_element_type=F32) * dec_ref[h]
        inner = jnp.dot(s.astype(BF16), v, preferred_element_type=F32)
        cross = jnp.dot(qb, state.astype(BF16), preferred_element_type=F32) * xi
        kz = (k * zeta_ref[h]).astype(BF16)
        kv = lax.dot_general(kz, v, (((0,), (0,)), ((), ())), preferred_element_type=F32)
        state_scr[h] = xi[C - 1:C, :] * state + kv
        o = inner + cross
        mu = jnp.mean(o, axis=-1, keepdims=True)
        d = o - mu
        var = jnp.mean(d * d, axis=-1, keepdims=True)
        on = d * lax.rsqrt(var + GN_EPS) * g_ref[:, vv]
        o_ref[:, vv] = (_silu(rg_ref[:, vv].astype(F32)) * on).astype(o_ref.dtype)


def _retention_tables():
    C = RET_BLOCK
    h = np.arange(RET_HEADS, dtype=np.float64)
    log_g = np.log(1.0 - np.exp2(-5.0 - h))
    idx = np.arange(C, dtype=np.float64)
    diff = idx[:, None] - idx[None, :]
    decay = np.where(diff[None] >= 0, np.exp(np.maximum(diff, 0.0)[None] * log_g[:, None, None]), 0.0)
    zeta = np.exp((C - 1 - idx)[None, :] * log_g[:, None])
    xi = np.exp((idx + 1.0)[None, :] * log_g[:, None])
    zeta_rep = np.broadcast_to(zeta[:, :, None], (RET_HEADS, C, RET_QK))
    xi_rep = np.broadcast_to(xi[:, :, None], (RET_HEADS, C, RET_V))
    return (jnp.asarray(decay, F32), jnp.asarray(xi_rep, F32), jnp.asarray(zeta_rep, F32))


def _retention(z, cr, sr, g_ret, B, S):
    T = z.shape[0]
    C = RET_BLOCK
    N = S // C
    H = RET_HEADS
    WQ, WV = H * RET_QK, H * RET_V
    dec, xi, zeta = _retention_tables()
    row = lambda b, n: b * N + n
    const = lambda shape: pl.BlockSpec(shape, lambda b, n: (0,) * len(shape))
    return pl.pallas_call(
        _ret_kernel,
        grid=(B, N),
        in_specs=[pl.BlockSpec((C, WQ), lambda b, n: (row(b, n), Z_RQ // WQ)),
                  pl.BlockSpec((C, WQ), lambda b, n: (row(b, n), Z_RK // WQ)),
                  pl.BlockSpec((C, WV), lambda b, n: (row(b, n), Z_RV // WV)),
                  pl.BlockSpec((C, WV), lambda b, n: (row(b, n), Z_RG // WV)),
                  pl.BlockSpec((C, LANES), lambda b, n: (row(b, n), 0)),
                  pl.BlockSpec((C, LANES), lambda b, n: (row(b, n), 0)),
                  const((H, C, C)), const((H, C, RET_V)), const((H, C, RET_QK)), const((1, WV))],
        out_specs=pl.BlockSpec((C, WV), lambda b, n: (row(b, n), 0)),
        out_shape=jax.ShapeDtypeStruct((T, WV), BF16),
        scratch_shapes=[pltpu.VMEM((H, RET_QK, RET_V), F32)],
        compiler_params=_cparams(("arbitrary", "arbitrary")),
        name="retention",
    )(z, z, z, z, cr, sr, dec, xi, zeta, g_ret.reshape(1, -1))


def _merge_kernel(x_ref, oa_ref, ob_ref, ga_ref, gb_ref, gt_ref, wa_ref, wb_ref, wo_ref, x1_ref):
    a = jnp.dot(oa_ref[...], wa_ref[...], preferred_element_type=F32)
    b = jnp.dot(ob_ref[...], wb_ref[...], preferred_element_type=F32)
    merged = (jax.nn.sigmoid(ga_ref[...].astype(F32)) * a
              + jax.nn.sigmoid(gb_ref[...].astype(F32)) * b)
    y = jnp.dot(merged.astype(BF16), wo_ref[...], preferred_element_type=F32)
    x1_ref[...] = x_ref[...] + gt_ref[...] * y


def _merge(x2, o_mla, o_ret, z, mod3, wa, wb, wo, S):
    T, D = x2.shape
    tm = ROW_TILE
    per_b = S // tm
    full = lambda shape: pl.BlockSpec(shape, lambda i: (0, 0))
    return pl.pallas_call(
        _merge_kernel,
        grid=(T // tm,),
        in_specs=[pl.BlockSpec((tm, D), lambda i: (i, 0)),
                  pl.BlockSpec((tm, o_mla.shape[1]), lambda i: (i, 0)),
                  pl.BlockSpec((tm, D), lambda i: (i, 0)),
                  pl.BlockSpec((tm, D), lambda i: (i, Z_GA // D)),
                  pl.BlockSpec((tm, D), lambda i: (i, Z_GB // D)),
                  _mod_spec(D, per_b, MOD_GT1),
                  full(wa.shape), full(wb.shape), full(wo.shape)],
        out_specs=pl.BlockSpec((tm, D), lambda i: (i, 0)),
        out_shape=jax.ShapeDtypeStruct((T, D), F32),
        compiler_params=_cparams(("arbitrary",)),
        name="merge_out",
    )(x2, o_mla, o_ret, z, z, mod3, wa, wb, wo)


def _router_kernel(x1_ref, sc_ref, sh_ref, g_ref, wr_ref, br_ref,
                   h2_ref, wts_ref, pp_ref, off_ref, pc_ref):
    TB = x1_ref.shape[0]
    D = x1_ref.shape[1]
    E, G = N_EXPERTS, N_GROUPS
    per = E // G
    h2 = _rms(x1_ref[...]) * g_ref[...] * (1.0 + sc_ref[...]) + sh_ref[...]
    _store_row_tiles(h2_ref, h2)
    logits = lax.dot_general(wr_ref[...], h2, (((1,), (1,)), ((), ())),
                             precision=lax.Precision.HIGHEST,
                             preferred_element_type=F32)
    s = jax.nn.sigmoid(logits)
    biased = s + br_ref[...]
    sub = lax.broadcasted_iota(I32, (per, TB), 0)
    neg = -jnp.inf

    def first_argmax(vals, m, idx, sentinel):
        return jnp.min(jnp.where(vals == m, idx, sentinel), axis=0, keepdims=True)

    bg = [biased[g * per:(g + 1) * per, :] for g in range(G)]
    sg = [s[g * per:(g + 1) * per, :] for g in range(G)]
    gscore = []
    for g in range(G):
        m1 = jnp.max(bg[g], axis=0, keepdims=True)
        i1 = first_argmax(bg[g], m1, sub, per)
        m2 = jnp.max(jnp.where(sub == i1, neg, bg[g]), axis=0, keepdims=True)
        gscore.append(m1 + m2)
    gs = jnp.concatenate(gscore, axis=0)
    gidx = lax.broadcasted_iota(I32, (G, TB), 0)
    gsel = jnp.zeros((G, TB), F32)
    for _ in range(TOPK_GROUPS):
        m = jnp.max(gs, axis=0, keepdims=True)
        i = first_argmax(gs, m, gidx, G)
        hit = gidx == i
        gsel = jnp.where(hit, 1.0, gsel)
        gs = jnp.where(hit, neg, gs)
    cand = [jnp.where(gsel[g:g + 1, :] > 0.0, bg[g], neg) for g in range(G)]
    eidx = [sub + g * per for g in range(G)]
    sel = [jnp.zeros((per, TB), F32) for _ in range(G)]
    top_i, top_w = [], []
    for _ in range(TOP_K):
        m = functools.reduce(jnp.maximum, [jnp.max(c, axis=0, keepdims=True) for c in cand])
        i = functools.reduce(jnp.minimum,
                             [first_argmax(cand[g], m, eidx[g], E) for g in range(G)])
        w = jnp.zeros((1, TB), F32)
        for g in range(G):
            hit = eidx[g] == i
            w = w + jnp.sum(jnp.where(hit, sg[g], 0.0), axis=0, keepdims=True)
            sel[g] = jnp.where(hit, 1.0, sel[g])
            cand[g] = jnp.where(hit, neg, cand[g])
        top_i.append(i)
        top_w.append(w)
    wsum = functools.reduce(lambda a, b: a + b, top_w)
    wts_ref[...] = jnp.concatenate([w / wsum * ROUTED_SCALE for w in top_w], axis=0)

    mask = jnp.concatenate(sel, axis=0)
    t_row = lax.broadcasted_iota(I32, (TB, TB), 0)
    t_col = lax.broadcasted_iota(I32, (TB, TB), 1)
    before = jnp.where(t_row < t_col, 1.0, 0.0).astype(BF16)
    rank = jnp.dot(mask.astype(BF16), before, preferred_element_type=F32)
    cnt = jnp.sum(mask, axis=1, keepdims=True)
    pc_rep = jnp.broadcast_to(cnt, (E, LANES))
    e_row = lax.broadcasted_iota(I32, (E, E), 0)
    e_col = lax.broadcasted_iota(I32, (E, E), 1)
    lower = jnp.where(e_col < e_row, 1.0, 0.0)
    off_rep = jnp.dot(lower, pc_rep, precision=lax.Precision.HIGHEST,
                      preferred_element_type=F32)
    off_ref[...] = off_rep.astype(I32)
    pc_ref[...] = pc_rep.astype(I32)
    posfull = off_rep[:, :1] + rank
    pos = []
    for kk in range(TOP_K):
        p = jnp.zeros((1, TB), F32)
        for g in range(G):
            p = p + jnp.sum(jnp.where(eidx[g] == top_i[kk], posfull[g * per:(g + 1) * per, :], 0.0),
                            axis=0, keepdims=True)
        pos.append(p.astype(I32))
    pp_ref[...] = jnp.concatenate(pos, axis=0) * (D // LANES)


def _router(x1, mod3, g_norm2, w_router, b_router, S):
    T, D = x1.shape
    TB = min(MOE_TB, T)
    nb = T // TB
    per_b = S // TB
    E = N_EXPERTS
    return pl.pallas_call(
        _router_kernel,
        grid=(nb,),
        in_specs=[pl.BlockSpec((TB, D), lambda i: (i, 0)),
                  _mod_spec(D, per_b, MOD_SC2), _mod_spec(D, per_b, MOD_SH2),
                  pl.BlockSpec((1, D), lambda i: (0, 0)),
                  pl.BlockSpec((E, D), lambda i: (0, 0)),
                  pl.BlockSpec((E, 1), lambda i: (0, 0))],
        out_specs=[pl.BlockSpec((TB * D // LANES, LANES), lambda i: (i, 0)),
                   pl.BlockSpec((TOP_K, TB), lambda i: (0, i)),
                   pl.BlockSpec((TOP_K, TB), lambda i: (0, i)),
                   pl.BlockSpec((E, LANES), lambda i: (i, 0)),
                   pl.BlockSpec((E, LANES), lambda i: (i, 0))],
        out_shape=[jax.ShapeDtypeStruct((T * D // LANES, LANES), F32),
                   jax.ShapeDtypeStruct((TOP_K, T), F32),
                   jax.ShapeDtypeStruct((TOP_K, T), I32),
                   jax.ShapeDtypeStruct((nb * E, LANES), I32),
                   jax.ShapeDtypeStruct((nb * E, LANES), I32)],
        compiler_params=_cparams(("arbitrary",)),
        name="moe_router",
    )(x1, mod3, mod3, g_norm2.reshape(1, D), w_router.T, b_router.reshape(E, 1))


def _moe_kernel(pp_sm, w_sm, off_sm, cnt_sm, h2_ref, wg_ref, wu_ref, wd_ref, *rest, TB):
    out_ref, xs_scr = rest[-2:]
    j = pl.program_id(0)
    step = pl.program_id(1)
    E = N_EXPERTS
    D = wg_ref.shape[1]
    RT = D // LANES
    CH = MOE_CH

    def tile(ref, first):
        return ref.at[pl.ds(pl.multiple_of(first, RT), RT), :]

    def positions(t):
        return [pp_sm[(j * TB + t) * TOP_K + k] for k in range(TOP_K)]

    @pl.when(step == 0)
    def _dispatch():
        xs_scr[TOP_K * TB * RT:, :] = jnp.zeros((2 * CH * RT, LANES), F32)

        def scatter(t, carry):
            row = tile(h2_ref, t * RT)[...]
            for p in positions(t):
                tile(xs_scr, p)[...] = row
            return carry

        lax.fori_loop(0, TB, scatter, 0, unroll=8)

    def run_rows(ee, r0, n_left, nrows):
        blk = xs_scr.at[pl.ds(pl.multiple_of(r0 * RT, RT), nrows * RT), :]
        xin = _load_row_tiles(blk, nrows)
        xb = xin.astype(BF16)
        g = jnp.dot(xb, wg_ref[ee], preferred_element_type=F32)
        u = jnp.dot(xb, wu_ref[ee], preferred_element_type=F32)
        y = jnp.dot((_silu(g) * u).astype(BF16), wd_ref[ee], preferred_element_type=F32)
        rows = lax.broadcasted_iota(I32, (nrows, D), 0)
        _store_row_tiles(blk, jnp.where(rows < n_left, y, xin))

    for ee in range(MOE_EPS):
        e = step * MOE_EPS + ee
        st = off_sm[j * E + e]
        n = cnt_sm[j * E + e]
        nbig = n // (2 * CH)

        def big_body(i, carry, ee=ee, st=st, n=n):
            run_rows(ee, st + i * 2 * CH, n - i * 2 * CH, 2 * CH)
            return carry

        lax.fori_loop(0, nbig, big_body, 0)
        r1 = st + nbig * 2 * CH
        rem = n - nbig * 2 * CH

        @pl.when(rem > CH)
        def _(ee=ee, r1=r1, rem=rem):
            run_rows(ee, r1, rem, 2 * CH)

        @pl.when((rem > 0) & (rem <= CH))
        def _(ee=ee, r1=r1, rem=rem):
            run_rows(ee, r1, rem, CH)

    @pl.when(step == pl.num_programs(1) - 1)
    def _combine():
        def gather(t, carry):
            acc = None
            for k, p in enumerate(positions(t)):
                term = w_sm[(j * TB + t) * TOP_K + k] * tile(xs_scr, p)[...]
                acc = term if acc is None else acc + term
            tile(out_ref, t * RT)[...] = acc
            return carry

        lax.fori_loop(0, TB, gather, 0, unroll=16)


def _moe(h2, wts, pp, off, pc, wg, wu, wd, n_split):
    D = wg.shape[1]
    RT = D // LANES
    T = h2.shape[0] // RT
    E = N_EXPERTS
    EPS = MOE_EPS
    Ts = T // n_split
    TB = min(MOE_TB, Ts)
    nb = Ts // TB
    rows = TOP_K * TB + 2 * MOE_CH
    routed = None
    for s in range(n_split):
        tok = slice(s * Ts, (s + 1) * Ts)
        blk = slice(s * nb * E, (s + 1) * nb * E)
        in_specs = [pl.BlockSpec((TB * RT, LANES), lambda j, e, *_, s=s: (s * nb + j, 0),
                                 pipeline_mode=pl.Buffered(1)),
                    pl.BlockSpec((EPS, D, D_EXPERT), lambda j, e, *_: (e, 0, 0)),
                    pl.BlockSpec((EPS, D, D_EXPERT), lambda j, e, *_: (e, 0, 0)),
                    pl.BlockSpec((EPS, D_EXPERT, D), lambda j, e, *_: (e, 0, 0))]
        args = [pp[:, tok].T.reshape(-1), wts[:, tok].T.reshape(-1), off[blk], pc[blk],
                h2, wg, wu, wd]
        aliases = {}
        if routed is not None:
            in_specs.append(pl.BlockSpec(memory_space=pl.ANY))
            args.append(routed)
            aliases = {len(args) - 1: 0}
        grid_spec = pltpu.PrefetchScalarGridSpec(
            num_scalar_prefetch=4,
            grid=(nb, E // EPS),
            in_specs=in_specs,
            out_specs=pl.BlockSpec((TB * RT, LANES), lambda j, e, *_, s=s: (s * nb + j, 0),
                                   pipeline_mode=pl.Buffered(1)),
            scratch_shapes=[pltpu.VMEM((rows * RT, LANES), F32)],
        )
        routed = pl.pallas_call(
            functools.partial(_moe_kernel, TB=TB),
            grid_spec=grid_spec,
            out_shape=jax.ShapeDtypeStruct((T * RT, LANES), F32),
            input_output_aliases=aliases,
            compiler_params=_cparams(("arbitrary", "arbitrary"), MOE_VMEM_LIMIT),
            name="moe_experts",
        )(*args)
    return routed


def _final_kernel(x1_ref, routed_ref, h2_ref, gt_ref, wsg_ref, wsu_ref, wsd_ref, gf_ref, o_ref):
    tm = x1_ref.shape[0]
    hb = _load_row_tiles(h2_ref, tm).astype(BF16)
    g = jnp.dot(hb, wsg_ref[...], preferred_element_type=F32)
    u = jnp.dot(hb, wsu_ref[...], preferred_element_type=F32)
    shared = jnp.dot((_silu(g) * u).astype(BF16), wsd_ref[...], preferred_element_type=F32)
    xo = x1_ref[...] + gt_ref[...] * (_load_row_tiles(routed_ref, tm) + shared)
    o_ref[...] = _rms(xo) * gf_ref[...]


def _final(x1, routed, h2, mod3, wsg, wsu, wsd, g_final, S):
    T, D = x1.shape
    tm = FINAL_TILE
    per_b = S // tm
    full = lambda shape: pl.BlockSpec(shape, lambda i: (0, 0))
    return pl.pallas_call(
        _final_kernel,
        grid=(T // tm,),
        in_specs=[pl.BlockSpec((tm, D), lambda i: (i, 0)),
                  pl.BlockSpec((tm * D // LANES, LANES), lambda i: (i, 0)),
                  pl.BlockSpec((tm * D // LANES, LANES), lambda i: (i, 0)),
                  _mod_spec(D, per_b, MOD_GT2),
                  full(wsg.shape), full(wsu.shape), full(wsd.shape), full((1, D))],
        out_specs=pl.BlockSpec((tm, D), lambda i: (i, 0)),
        out_shape=jax.ShapeDtypeStruct((T, D), F32),
        compiler_params=_cparams(("arbitrary",)),
        name="final_out",
    )(x1, routed, h2, mod3, wsg, wsu, wsd, g_final.reshape(1, D))


def _pack_w_in_kernel(w_ref, o_ref):
    splits = [MLA_Q_RANK, MLA_KV_RANK, MLA_ROPE, RET_HEADS * RET_QK, RET_HEADS * RET_QK,
              RET_HEADS * RET_V, RET_HEADS * RET_V]
    gate_w = (w_ref.shape[1] - sum(splits)) // 2
    splits += [gate_w, gate_w]
    edges = [0] + [int(v) for v in np.cumsum(splits)]
    wcq, wckv, wkr, wrq, wrk, wrv, wrg, wga, wgb = [
        w_ref[:, edges[i]:edges[i + 1]] for i in range(len(splits))]
    rows = w_ref.shape[0]
    hm = MLA_ROPE // 2
    zl = jnp.zeros((rows, MLA_NOPE), F32)
    zr = jnp.zeros((rows, LANES - MLA_NOPE - MLA_ROPE), F32)
    pad = jnp.zeros((rows, Z_COLS - (Z_KR2 + LANES)), F32)
    w = jnp.concatenate([wrv, wrg, wga, wgb, wrq, wrk, wcq, wckv,
                         zl, wkr, zr, zl, wkr[:, hm:], wkr[:, :hm], zr, pad], axis=1)
    o_ref[...] = w.astype(BF16)


def _pack_w_in(w_in):
    D, n_in = w_in.shape
    tr = WPACK_ROWS
    return pl.pallas_call(
        _pack_w_in_kernel,
        grid=(D // tr,),
        in_specs=[pl.BlockSpec((tr, n_in), lambda i: (i, 0))],
        out_specs=pl.BlockSpec((tr, Z_COLS), lambda i: (i, 0)),
        out_shape=jax.ShapeDtypeStruct((D, Z_COLS), BF16),
        compiler_params=_cparams(("arbitrary",)),
        name="pack_w_in",
    )(w_in)


def _pack_mla_weights(w_uq, w_ukv):
    H = MLA_HEADS
    hm = MLA_ROPE // 2
    wq = w_uq.reshape(MLA_Q_RANK, H, MLA_NOPE + MLA_ROPE)
    nope, pe = wq[..., :MLA_NOPE], wq[..., MLA_NOPE:]
    zpad = jnp.zeros((MLA_Q_RANK, H, LANES - MLA_NOPE - MLA_ROPE), w_uq.dtype)
    wq1 = jnp.concatenate([nope, pe, zpad], axis=-1).reshape(MLA_Q_RANK, H * LANES)
    wq2 = jnp.concatenate([jnp.zeros_like(nope), pe[..., hm:], pe[..., :hm], zpad],
                          axis=-1).reshape(MLA_Q_RANK, H * LANES)
    wkv = w_ukv.reshape(MLA_KV_RANK, H, MLA_NOPE + MLA_V)
    kn, vv = wkv[..., :MLA_NOPE], wkv[..., MLA_NOPE:]
    wk = jnp.concatenate([kn, jnp.zeros((MLA_KV_RANK, H, LANES - MLA_NOPE), w_ukv.dtype)],
                         axis=-1).reshape(MLA_KV_RANK, H * LANES)
    wv = vv.reshape(MLA_KV_RANK, H * MLA_V).T
    return wq1.T.astype(BF16), wq2.T.astype(BF16), wk.astype(BF16), wv.astype(BF16)


def kernel(x, c, positions, w_ada, b_ada, g_norm1, w_in, g_cq, w_uq, g_ckv, w_ukv, g_ret,
           w_o_mla, w_o_ret, w_out, g_norm2, w_router, b_router, w_exp_gate, w_exp_up,
           w_exp_down, w_sh_gate, w_sh_up, w_sh_down, g_final):
    B, S, D = x.shape
    T = B * S
    x2 = x.reshape(T, D)

    mod = _ada(c, w_ada, b_ada)
    mod3 = mod.reshape(B * N_MOD, 1, D)
    cr, sr, cm, sm = _rope_tables(positions)

    z = _inproj(x2, mod3, g_norm1, _pack_w_in(w_in), S)
    wq1, wq2, wk, wv = _pack_mla_weights(w_uq, w_ukv)
    q, k, v = _mla_up(z, cm, sm, g_cq, g_ckv, wq1, wq2, wk, wv)
    o_mla = _attention(q, k, v, B, S)
    o_ret = _retention(z, cr, sr, g_ret, B, S)
    x1 = _merge(x2, o_mla, o_ret, z, mod3, w_o_mla.astype(BF16), w_o_ret.astype(BF16),
                w_out.astype(BF16), S)

    h2, wts, pp, off_rep, pc_rep = _router(x1, mod3, g_norm2, w_router, b_router, S)
    routed = _moe(h2, wts, pp, off_rep[:, 0], pc_rep[:, 0],
                  w_exp_gate.astype(BF16), w_exp_up.astype(BF16), w_exp_down.astype(BF16), n_split=B)
    out = _final(x1, routed, h2, mod3, w_sh_gate.astype(BF16), w_sh_up.astype(BF16),
                 w_sh_down.astype(BF16), g_final, S)
    return out.reshape(B, S, D)
```

```python
import functools

import numpy as np
import jax
import jax.numpy as jnp
from jax import lax
from jax.experimental import pallas as pl
from jax.experimental.pallas import tpu as pltpu

F32 = jnp.float32
BF16 = jnp.bfloat16
I32 = jnp.int32

MLA_HEADS = 8
MLA_Q_RANK = 384
MLA_KV_RANK = 256
MLA_NOPE = 64
MLA_ROPE = 32
MLA_V = 64
RET_HEADS = 4
RET_QK = 128
RET_V = 256
RET_BLOCK = 256
ROPE_THETA = 10000.0
N_EXPERTS = 64
TOP_K = 8
N_GROUPS = 8
TOPK_GROUPS = 4
D_EXPERT = 256
ROUTED_SCALE = 2.5
RMS_EPS = 1e-6
GN_EPS = 1e-5

LANES = 128
SUBLANES = 8
VMEM_LIMIT = 56 * 1024 * 1024

Z_RV, Z_RG, Z_GA, Z_GB = 0, 1024, 2048, 3072
Z_RQ, Z_RK = 4096, 4608
Z_CQKV = 5120
Z_KR1, Z_KR2 = 5760, 5888
Z_COLS = 6144

LOG2E = 1.4426950408889634

MOD_SH1, MOD_SC1, MOD_GT1, MOD_SH2, MOD_SC2, MOD_GT2 = range(6)
N_MOD = 6

ROW_TILE = 512
FINAL_TILE = 1024
ROPE_TILE = 1024
INPROJ_TM, INPROJ_TN = 1024, 2048
WPACK_ROWS = 256

ATTN_BQ = 512
ATTN_BK = 512
ATTN_HEADS = 8
ATTN_LROWS = 16

MOE_TB = 1024
MOE_CH = 144
MOE_EPS = 4
MOE_VMEM_LIMIT = 58 * 1024 * 1024


def _cparams(sem, vmem_limit=VMEM_LIMIT):
    return pltpu.CompilerParams(dimension_semantics=sem, vmem_limit_bytes=vmem_limit)


def _mod_spec(D, per_b, which):
    return pl.BlockSpec((None, 1, D), lambda i, *_: ((i // per_b) * N_MOD + which, 0, 0))


def _rms(x):
    return x * lax.rsqrt(jnp.mean(x * x, axis=-1, keepdims=True) + RMS_EPS)


def _silu(x):
    return x * jax.nn.sigmoid(x)


def _load_row_tiles(ref, nrows):
    nchunk = ref.shape[0] // nrows
    return jnp.concatenate([ref[pl.ds(c, nrows, stride=nchunk), :] for c in range(nchunk)], axis=1)


def _store_row_tiles(ref, val):
    nrows, d = val.shape
    nchunk = d // LANES
    for c in range(nchunk):
        ref[pl.ds(c, nrows, stride=nchunk), :] = val[:, c * LANES:(c + 1) * LANES]


def _ada_kernel(c_ref, w_ref, b_ref, o_ref):
    c = c_ref[...]
    o_ref[...] = jnp.dot(_silu(c).astype(BF16), w_ref[...].astype(BF16),
                         preferred_element_type=F32) + b_ref[...]


def _ada(c, w_ada, b_ada):
    B, D = c.shape
    n_out = w_ada.shape[1]
    cp = jnp.zeros((SUBLANES, D), F32).at[:B].set(c)
    tn = D
    out = pl.pallas_call(
        _ada_kernel,
        grid=(n_out // tn,),
        in_specs=[pl.BlockSpec((SUBLANES, D), lambda j: (0, 0)),
                  pl.BlockSpec((D, tn), lambda j: (0, j)),
                  pl.BlockSpec((1, tn), lambda j: (0, j))],
        out_specs=pl.BlockSpec((SUBLANES, tn), lambda j: (0, j)),
        out_shape=jax.ShapeDtypeStruct((SUBLANES, n_out), F32),
        compiler_params=_cparams(("arbitrary",)),
        name="ada_mod",
    )(cp, w_ada, b_ada.reshape(1, n_out))
    return out[:B]


def _rope_kernel(pos_ref, inv_ref, cr_ref, sr_ref, cm_ref, sm_ref):
    ang = pos_ref[...].astype(F32) * inv_ref[...]
    c = jnp.cos(ang)
    s = jnp.sin(ang)
    lane = lax.broadcasted_iota(I32, c.shape, 1)
    half = RET_QK // 2
    cr_ref[...] = jnp.where(lane < half, c, pltpu.roll(c, half, 1))
    sr_ref[...] = jnp.where(lane < half, -s, pltpu.roll(s, half, 1))
    hm = MLA_ROPE // 2
    cm_ref[...] = jnp.where(lane < MLA_NOPE, 1.0,
                            jnp.where(lane < MLA_NOPE + hm, c,
                                      jnp.where(lane < MLA_NOPE + 2 * hm, pltpu.roll(c, hm, 1), 0.0)))
    sm_ref[...] = jnp.where(lane < MLA_NOPE, 0.0,
                            jnp.where(lane < MLA_NOPE + hm, -s,
                                      jnp.where(lane < MLA_NOPE + 2 * hm, pltpu.roll(s, hm, 1), 0.0)))


def _rope_tables(positions):
    T = positions.size
    tm = min(T, ROPE_TILE)
    inv_r = 1.0 / (ROPE_THETA ** (jnp.arange(0, RET_QK, 2, dtype=F32) / RET_QK))
    inv_m = 1.0 / (ROPE_THETA ** (jnp.arange(0, MLA_ROPE, 2, dtype=F32) / MLA_ROPE))
    inv = jnp.zeros((1, LANES), F32).at[0, :RET_QK // 2].set(inv_r)
    inv = inv.at[0, MLA_NOPE:MLA_NOPE + MLA_ROPE // 2].set(inv_m)
    tab = jax.ShapeDtypeStruct((T, LANES), F32)
    spec = pl.BlockSpec((tm, LANES), lambda i: (i, 0))
    return pl.pallas_call(
        _rope_kernel,
        grid=(T // tm,),
        in_specs=[pl.BlockSpec((tm, 1), lambda i: (i, 0)),
                  pl.BlockSpec((1, LANES), lambda i: (0, 0))],
        out_specs=[spec, spec, spec, spec],
        out_shape=[tab, tab, tab, tab],
        compiler_params=_cparams(("arbitrary",)),
        name="rope_tables",
    )(positions.reshape(T, 1), inv)


def _inproj_kernel(x_ref, sc_ref, sh_ref, g_ref, w_ref, z_ref, h_scr):
    @pl.when(pl.program_id(1) == 0)
    def _():
        h = _rms(x_ref[...]) * g_ref[...] * (1.0 + sc_ref[...]) + sh_ref[...]
        h_scr[...] = h.astype(BF16)

    z_ref[...] = jnp.dot(h_scr[...], w_ref[...], preferred_element_type=F32).astype(z_ref.dtype)


def _inproj(x2, mod3, g_norm1, w_pack, S):
    T, D = x2.shape
    N = w_pack.shape[1]
    tm, tn = INPROJ_TM, INPROJ_TN
    per_b = S // tm
    return pl.pallas_call(
        _inproj_kernel,
        grid=(T // tm, N // tn),
        in_specs=[pl.BlockSpec((tm, D), lambda i, j: (i, 0)),
                  _mod_spec(D, per_b, MOD_SC1), _mod_spec(D, per_b, MOD_SH1),
                  pl.BlockSpec((1, D), lambda i, j: (0, 0)),
                  pl.BlockSpec((D, tn), lambda i, j: (0, j))],
        out_specs=pl.BlockSpec((tm, tn), lambda i, j: (i, j)),
        out_shape=jax.ShapeDtypeStruct((T, N), BF16),
        scratch_shapes=[pltpu.VMEM((tm, D), BF16)],
        compiler_params=_cparams(("arbitrary", "arbitrary")),
        name="in_proj",
    )(x2, mod3, mod3, g_norm1.reshape(1, D), w_pack)


def _mla_up_kernel(zc_ref, kr1_ref, kr2_ref, cm_ref, sm_ref, gq_ref, gkv_ref,
                   wq1_ref, wq2_ref, wk_ref, wv_ref, q_ref, k_ref, v_ref):
    zc = zc_ref[...].astype(F32)
    cqn = (_rms(zc[:, :MLA_Q_RANK]) * gq_ref[...]).astype(BF16)
    ckvn = (_rms(zc[:, MLA_Q_RANK:]) * gkv_ref[...]).astype(BF16)
    cm = cm_ref[...]
    sm = sm_ref[...]
    nt = (((1,), (1,)), ((), ()))
    q1t = lax.dot_general(wq1_ref[...], cqn, nt, preferred_element_type=F32)
    q2t = lax.dot_general(wq2_ref[...], cqn, nt, preferred_element_type=F32)
    cmt, smt = cm.T, sm.T
    kn = jnp.dot(ckvn, wk_ref[...], preferred_element_type=F32)
    kpe = kr1_ref[...].astype(F32) * cm + kr2_ref[...].astype(F32) * sm
    qscale = (MLA_NOPE + MLA_ROPE) ** -0.5 * LOG2E
    for h in range(MLA_HEADS):
        sl = slice(h * LANES, (h + 1) * LANES)
        q_ref[sl, :] = ((q1t[sl, :] * cmt + q2t[sl, :] * smt) * qscale).astype(BF16)
        k_ref[:, sl] = (kn[:, sl] + kpe).astype(BF16)
    vt = lax.dot_general(wv_ref[...], ckvn, (((1,), (1,)), ((), ())),
                         preferred_element_type=F32).astype(BF16)
    vrows = MLA_V + ATTN_LROWS
    for h in range(MLA_HEADS):
        v_ref[h * vrows:h * vrows + MLA_V, :] = vt[h * MLA_V:(h + 1) * MLA_V, :]
        v_ref[h * vrows + MLA_V:(h + 1) * vrows, :] = jnp.ones((ATTN_LROWS, vt.shape[1]), BF16)


def _mla_up(z, cm, sm, g_cq, g_ckv, wq1, wq2, wk, wv):
    T = z.shape[0]
    tm = ROW_TILE
    HW = MLA_HEADS * LANES
    wc = MLA_Q_RANK + MLA_KV_RANK
    full = lambda shape: pl.BlockSpec(shape, lambda i: (0, 0))
    return pl.pallas_call(
        _mla_up_kernel,
        grid=(T // tm,),
        in_specs=[pl.BlockSpec((tm, wc), lambda i: (i, Z_CQKV // wc)),
                  pl.BlockSpec((tm, LANES), lambda i: (i, Z_KR1 // LANES)),
                  pl.BlockSpec((tm, LANES), lambda i: (i, Z_KR2 // LANES)),
                  pl.BlockSpec((tm, LANES), lambda i: (i, 0)),
                  pl.BlockSpec((tm, LANES), lambda i: (i, 0)),
                  full((1, MLA_Q_RANK)), full((1, MLA_KV_RANK)),
                  full(wq1.shape), full(wq2.shape), full(wk.shape), full(wv.shape)],
        out_specs=[pl.BlockSpec((HW, tm), lambda i: (0, i)),
                   pl.BlockSpec((tm, HW), lambda i: (i, 0)),
                   pl.BlockSpec((MLA_HEADS * (MLA_V + ATTN_LROWS), tm), lambda i: (0, i))],
        out_shape=[jax.ShapeDtypeStruct((HW, T), BF16),
                   jax.ShapeDtypeStruct((T, HW), BF16),
                   jax.ShapeDtypeStruct((MLA_HEADS * (MLA_V + ATTN_LROWS), T), BF16)],
        compiler_params=_cparams(("arbitrary",)),
        name="mla_up",
    )(z, z, z, cm, sm, g_cq.reshape(1, -1), g_ckv.reshape(1, -1), wq1, wq2, wk, wv)


def _attn_kernel(q_ref, k_ref, vt_ref, o_ref, *scr, bq, bk):
    nh = ATTN_HEADS
    slots = (scr[:nh], scr[nh:2 * nh])
    p_scrs, acc_scr, m_scr = scr[2 * nh:3 * nh], scr[3 * nh], scr[3 * nh + 1]
    mx_scrs = scr[3 * nh + 2:3 * nh + 4]
    qi = pl.program_id(2)
    vrows = MLA_V + ATTN_LROWS
    qts = [q_ref[h * LANES:(h + 1) * LANES, :] for h in range(nh)]
    acc_scr[...] = jnp.zeros(acc_scr.shape, F32)
    m_scr[...] = jnp.full(m_scr.shape, -jnp.inf, F32)
    sub8 = lax.broadcasted_iota(I32, (SUBLANES, bq), 0)
    lane8 = lax.broadcasted_iota(I32, (SUBLANES, bq), 1)
    pack = 2 * SUBLANES

    def scores(kb, slot):
        k0 = pl.multiple_of(kb * bk, bk)
        for h in range(nh):
            s = jnp.dot(k_ref[pl.ds(k0, bk), h * LANES:(h + 1) * LANES], qts[h],
                        preferred_element_type=F32)
            slots[slot][h][...] = s
            mx_scrs[slot][h * SUBLANES:(h + 1) * SUBLANES, :] = jnp.max(
                s.reshape(bk // SUBLANES, SUBLANES, bq), axis=0)

    def update(kb, slot, masked):
        k0 = pl.multiple_of(kb * bk, bk)
        for h in range(nh):
            st, pr = slots[slot][h], p_scrs[h]
            if masked:
                parts = [None, None]
                for r in range(bk // SUBLANES):
                    rs = slice(r * SUBLANES, (r + 1) * SUBLANES)
                    x = jnp.where(lane8 >= sub8 + r * SUBLANES, st[rs, :], -jnp.inf)
                    st[rs, :] = x
                    parts[r % 2] = x if parts[r % 2] is None else jnp.maximum(parts[r % 2], x)
                part = jnp.maximum(parts[0], parts[1])
            else:
                part = mx_scrs[slot][h * SUBLANES:(h + 1) * SUBLANES, :]
            m_cur = jnp.max(part, axis=0, keepdims=True)
            m_prev = m_scr[h:h + 1, :]
            m_new = jnp.maximum(m_prev, m_cur)
            m_scr[h:h + 1, :] = m_new
            alpha = jnp.exp2(m_prev - m_new)
            for r in range(bk // pack):
                rs = slice(r * pack, (r + 1) * pack)
                pr[rs, :] = jnp.exp2(st[rs, :] - m_new).astype(BF16)
            rows = slice(h * vrows, (h + 1) * vrows)
            acc_scr[rows, :] = alpha * acc_scr[rows, :] + jnp.dot(
                vt_ref[rows, pl.ds(k0, bk)], pr[...], preferred_element_type=F32)

    scores(0, 0)
    npairs = qi // 2

    def pair(i, carry):
        kb = 2 * i
        scores(kb + 1, 1)
        update(kb, 0, False)
        scores(kb + 2, 0)
        update(kb + 1, 1, False)
        return carry

    lax.fori_loop(0, npairs, pair, 0)

    @pl.when(qi % 2 == 0)
    def _():
        update(qi, 0, True)

    @pl.when(qi % 2 == 1)
    def _():
        scores(qi, 1)
        update(qi - 1, 0, False)
        update(qi, 1, True)

    ot = jnp.concatenate(
        [acc_scr[h * vrows:h * vrows + MLA_V, :]
         / acc_scr[h * vrows + MLA_V:h * vrows + MLA_V + 1, :] for h in range(nh)], axis=0)
    o_ref[...] = ot.T.astype(o_ref.dtype)


def _attention(qt, k, vt, B, S):
    T = k.shape[0]
    bq = min(ATTN_BQ, S)
    bk = min(ATTN_BK, S)
    nq = S // bq
    nh = ATTN_HEADS
    vrows = MLA_V + ATTN_LROWS
    assert bq == bk and nh <= SUBLANES
    kern = functools.partial(_attn_kernel, bq=bq, bk=bk)
    return pl.pallas_call(
        kern,
        grid=(B, MLA_HEADS // nh, nq),
        in_specs=[pl.BlockSpec((nh * LANES, bq), lambda b, p, i: (p, b * nq + i)),
                  pl.BlockSpec((S, nh * LANES), lambda b, p, i: (b, p), pipeline_mode=pl.Buffered(1)),
                  pl.BlockSpec((nh * vrows, S), lambda b, p, i: (p, b), pipeline_mode=pl.Buffered(1))],
        out_specs=pl.BlockSpec((bq, nh * MLA_V), lambda b, p, i: (b * nq + i, p)),
        out_shape=jax.ShapeDtypeStruct((T, MLA_HEADS * MLA_V), BF16),
        scratch_shapes=([pltpu.VMEM((bk, bq), F32)] * (2 * nh) + [pltpu.VMEM((bk, bq), BF16)] * nh
                        + [pltpu.VMEM((nh * vrows, bq), F32), pltpu.VMEM((SUBLANES, bq), F32)]
                        + [pltpu.VMEM((nh * SUBLANES, bq), F32)] * 2),
        compiler_params=_cparams(("arbitrary", "arbitrary", "arbitrary")),
        name="mla_attention",
    )(qt, k, vt)


def _ret_kernel(rq_ref, rk_ref, rv_ref, rg_ref, cr_ref, sr_ref, dec_ref, xi_ref, zeta_ref,
                g_ref, o_ref, state_scr):
    @pl.when(pl.program_id(1) == 0)
    def _():
        state_scr[...] = jnp.zeros(state_scr.shape, F32)

    C = rq_ref.shape[0]
    cr = cr_ref[...]
    sr = sr_ref[...]
    half = RET_QK // 2
    for h in range(RET_HEADS):
        qk = slice(h * RET_QK, (h + 1) * RET_QK)
        vv = slice(h * RET_V, (h + 1) * RET_V)
        rq = rq_ref[:, qk].astype(F32)
        rk = rk_ref[:, qk].astype(F32)
        q = rq * cr + pltpu.roll(rq, half, 1) * sr
        k = (rk * cr + pltpu.roll(rk, half, 1) * sr) * (RET_QK ** -0.5)
        v = rv_ref[:, vv]
        xi = xi_ref[h]
        state = state_scr[h]
        qb = q.astype(BF16)
        s = lax.dot_general(qb, k.astype(BF16), (((1,), (1,)), ((), ())),
                            preferred_element_type=F32) * dec_ref[h]
        inner = jnp.dot(s.astype(BF16), v, preferred_element_type=F32)
        cross = jnp.dot(qb, state.astype(BF16), preferred_element_type=F32) * xi
        kz = (k * zeta_ref[h]).astype(BF16)
        kv = lax.dot_general(kz, v, (((0,), (0,)), ((), ())), preferred_element_type=F32)
        state_scr[h] = xi[C - 1:C, :] * state + kv
        o = inner + cross
        mu = jnp.mean(o, axis=-1, keepdims=True)
        d = o - mu
        var = jnp.mean(d * d, axis=-1, keepdims=True)
        on = d * lax.rsqrt(var + GN_EPS) * g_ref[:, vv]
        o_ref[:, vv] = (_silu(rg_ref[:, vv].astype(F32)) * on).astype(o_ref.dtype)


def _retention_tables():
    C = RET_BLOCK
    h = np.arange(RET_HEADS, dtype=np.float64)
    log_g = np.log(1.0 - np.exp2(-5.0 - h))
    idx = np.arange(C, dtype=np.float64)
    diff = idx[:, None] - idx[None, :]
    decay = np.where(diff[None] >= 0, np.exp(np.maximum(diff, 0.0)[None] * log_g[:, None, None]), 0.0)
    zeta = np.exp((C - 1 - idx)[None, :] * log_g[:, None])
    xi = np.exp((idx + 1.0)[None, :] * log_g[:, None])
    zeta_rep = np.broadcast_to(zeta[:, :, None], (RET_HEADS, C, RET_QK))
    xi_rep = np.broadcast_to(xi[:, :, None], (RET_HEADS, C, RET_V))
    return (jnp.asarray(decay, F32), jnp.asarray(xi_rep, F32), jnp.asarray(zeta_rep, F32))


def _retention(z, cr, sr, g_ret, B, S):
    T = z.shape[0]
    C = RET_BLOCK
    N = S // C
    H = RET_HEADS
    WQ, WV = H * RET_QK, H * RET_V
    dec, xi, zeta = _retention_tables()
    row = lambda b, n: b * N + n
    const = lambda shape: pl.BlockSpec(shape, lambda b, n: (0,) * len(shape))
    return pl.pallas_call(
        _ret_kernel,
        grid=(B, N),
        in_specs=[pl.BlockSpec((C, WQ), lambda b, n: (row(b, n), Z_RQ // WQ)),
                  pl.BlockSpec((C, WQ), lambda b, n: (row(b, n), Z_RK // WQ)),
                  pl.BlockSpec((C, WV), lambda b, n: (row(b, n), Z_RV // WV)),
                  pl.BlockSpec((C, WV), lambda b, n: (row(b, n), Z_RG // WV)),
                  pl.BlockSpec((C, LANES), lambda b, n: (row(b, n), 0)),
                  pl.BlockSpec((C, LANES), lambda b, n: (row(b, n), 0)),
                  const((H, C, C)), const((H, C, RET_V)), const((H, C, RET_QK)), const((1, WV))],
        out_specs=pl.BlockSpec((C, WV), lambda b, n: (row(b, n), 0)),
        out_shape=jax.ShapeDtypeStruct((T, WV), BF16),
        scratch_shapes=[pltpu.VMEM((H, RET_QK, RET_V), F32)],
        compiler_params=_cparams(("arbitrary", "arbitrary")),
        name="retention",
    )(z, z, z, z, cr, sr, dec, xi, zeta, g_ret.reshape(1, -1))


def _merge_kernel(x_ref, oa_ref, ob_ref, ga_ref, gb_ref, gt_ref, wa_ref, wb_ref, wo_ref, x1_ref):
    a = jnp.dot(oa_ref[...], wa_ref[...], preferred_element_type=F32)
    b = jnp.dot(ob_ref[...], wb_ref[...], preferred_element_type=F32)
    merged = (jax.nn.sigmoid(ga_ref[...].astype(F32)) * a
              + jax.nn.sigmoid(gb_ref[...].astype(F32)) * b)
    y = jnp.dot(merged.astype(BF16), wo_ref[...], preferred_element_type=F32)
    x1_ref[...] = x_ref[...] + gt_ref[...] * y


def _merge(x2, o_mla, o_ret, z, mod3, wa, wb, wo, S):
    T, D = x2.shape
    tm = ROW_TILE
    per_b = S // tm
    full = lambda shape: pl.BlockSpec(shape, lambda i: (0, 0))
    return pl.pallas_call(
        _merge_kernel,
        grid=(T // tm,),
        in_specs=[pl.BlockSpec((tm, D), lambda i: (i, 0)),
                  pl.BlockSpec((tm, o_mla.shape[1]), lambda i: (i, 0)),
                  pl.BlockSpec((tm, D), lambda i: (i, 0)),
                  pl.BlockSpec((tm, D), lambda i: (i, Z_GA // D)),
                  pl.BlockSpec((tm, D), lambda i: (i, Z_GB // D)),
                  _mod_spec(D, per_b, MOD_GT1),
                  full(wa.shape), full(wb.shape), full(wo.shape)],
        out_specs=pl.BlockSpec((tm, D), lambda i: (i, 0)),
        out_shape=jax.ShapeDtypeStruct((T, D), F32),
        compiler_params=_cparams(("arbitrary",)),
        name="merge_out",
    )(x2, o_mla, o_ret, z, z, mod3, wa, wb, wo)


def _router_kernel(x1_ref, sc_ref, sh_ref, g_ref, wr_ref, br_ref,
                   h2_ref, wts_ref, pp_ref, off_ref, pc_ref):
    TB = x1_ref.shape[0]
    D = x1_ref.shape[1]
    E, G = N_EXPERTS, N_GROUPS
    per = E // G
    h2 = _rms(x1_ref[...]) * g_ref[...] * (1.0 + sc_ref[...]) + sh_ref[...]
    _store_row_tiles(h2_ref, h2)
    logits = lax.dot_general(wr_ref[...], h2, (((1,), (1,)), ((), ())),
                             precision=lax.Precision.HIGHEST,
                             preferred_element_type=F32)
    s = jax.nn.sigmoid(logits)
    biased = s + br_ref[...]
    sub = lax.broadcasted_iota(I32, (per, TB), 0)
    neg = -jnp.inf

    def first_argmax(vals, m, idx, sentinel):
        return jnp.min(jnp.where(vals == m, idx, sentinel), axis=0, keepdims=True)

    bg = [biased[g * per:(g + 1) * per, :] for g in range(G)]
    sg = [s[g * per:(g + 1) * per, :] for g in range(G)]
    gscore = []
    for g in range(G):
        m1 = jnp.max(bg[g], axis=0, keepdims=True)
        i1 = first_argmax(bg[g], m1, sub, per)
        m2 = jnp.max(jnp.where(sub == i1, neg, bg[g]), axis=0, keepdims=True)
        gscore.append(m1 + m2)
    gs = jnp.concatenate(gscore, axis=0)
    gidx = lax.broadcasted_iota(I32, (G, TB), 0)
    gsel = jnp.zeros((G, TB), F32)
    for _ in range(TOPK_GROUPS):
        m = jnp.max(gs, axis=0, keepdims=True)
        i = first_argmax(gs, m, gidx, G)
        hit = gidx == i
        gsel = jnp.where(hit, 1.0, gsel)
        gs = jnp.where(hit, neg, gs)
    cand = [jnp.where(gsel[g:g + 1, :] > 0.0, bg[g], neg) for g in range(G)]
    eidx = [sub + g * per for g in range(G)]
    sel = [jnp.zeros((per, TB), F32) for _ in range(G)]
    top_i, top_w = [], []
    for _ in range(TOP_K):
        m = functools.reduce(jnp.maximum, [jnp.max(c, axis=0, keepdims=True) for c in cand])
        i = functools.reduce(jnp.minimum,
                             [first_argmax(cand[g], m, eidx[g], E) for g in range(G)])
        w = jnp.zeros((1, TB), F32)
        for g in range(G):
            hit = eidx[g] == i
            w = w + jnp.sum(jnp.where(hit, sg[g], 0.0), axis=0, keepdims=True)
            sel[g] = jnp.where(hit, 1.0, sel[g])
            cand[g] = jnp.where(hit, neg, cand[g])
        top_i.append(i)
        top_w.append(w)
    wsum = functools.reduce(lambda a, b: a + b, top_w)
    wts_ref[...] = jnp.concatenate([w / wsum * ROUTED_SCALE for w in top_w], axis=0)

    mask = jnp.concatenate(sel, axis=0)
    t_row = lax.broadcasted_iota(I32, (TB, TB), 0)
    t_col = lax.broadcasted_iota(I32, (TB, TB), 1)
    before = jnp.where(t_row < t_col, 1.0, 0.0).astype(BF16)
    rank = jnp.dot(mask.astype(BF16), before, preferred_element_type=F32)
    cnt = jnp.sum(mask, axis=1, keepdims=True)
    pc_rep = jnp.broadcast_to(cnt, (E, LANES))
    e_row = lax.broadcasted_iota(I32, (E, E), 0)
    e_col = lax.broadcasted_iota(I32, (E, E), 1)
    lower = jnp.where(e_col < e_row, 1.0, 0.0)
    off_rep = jnp.dot(lower, pc_rep, precision=lax.Precision.HIGHEST,
                      preferred_element_type=F32)
    off_ref[...] = off_rep.astype(I32)
    pc_ref[...] = pc_rep.astype(I32)
    posfull = off_rep[:, :1] + rank
    pos = []
    for kk in range(TOP_K):
        p = jnp.zeros((1, TB), F32)
        for g in range(G):
            p = p + jnp.sum(jnp.where(eidx[g] == top_i[kk], posfull[g * per:(g + 1) * per, :], 0.0),
                            axis=0, keepdims=True)
        pos.append(p.astype(I32))
    pp_ref[...] = jnp.concatenate(pos, axis=0) * (D // LANES)


def _router(x1, mod3, g_norm2, w_router, b_router, S):
    T, D = x1.shape
    TB = min(MOE_TB, T)
    nb = T // TB
    per_b = S // TB
    E = N_EXPERTS
    return pl.pallas_call(
        _router_kernel,
        grid=(nb,),
        in_specs=[pl.BlockSpec((TB, D), lambda i: (i, 0)),
                  _mod_spec(D, per_b, MOD_SC2), _mod_spec(D, per_b, MOD_SH2),
                  pl.BlockSpec((1, D), lambda i: (0, 0)),
                  pl.BlockSpec((E, D), lambda i: (0, 0)),
                  pl.BlockSpec((E, 1), lambda i: (0, 0))],
        out_specs=[pl.BlockSpec((TB * D // LANES, LANES), lambda i: (i, 0)),
                   pl.BlockSpec((TOP_K, TB), lambda i: (0, i)),
                   pl.BlockSpec((TOP_K, TB), lambda i: (0, i)),
                   pl.BlockSpec((E, LANES), lambda i: (i, 0)),
                   pl.BlockSpec((E, LANES), lambda i: (i, 0))],
        out_shape=[jax.ShapeDtypeStruct((T * D // LANES, LANES), F32),
                   jax.ShapeDtypeStruct((TOP_K, T), F32),
                   jax.ShapeDtypeStruct((TOP_K, T), I32),
                   jax.ShapeDtypeStruct((nb * E, LANES), I32),
                   jax.ShapeDtypeStruct((nb * E, LANES), I32)],
        compiler_params=_cparams(("arbitrary",)),
        name="moe_router",
    )(x1, mod3, mod3, g_norm2.reshape(1, D), w_router.T, b_router.reshape(E, 1))


def _moe_kernel(pp_sm, w_sm, off_sm, cnt_sm, h2_ref, wg_ref, wu_ref, wd_ref, *rest, TB):
    out_ref, xs_scr = rest[-2:]
    j = pl.program_id(0)
    step = pl.program_id(1)
    E = N_EXPERTS
    D = wg_ref.shape[1]
    RT = D // LANES
    CH = MOE_CH

    def tile(ref, first):
        return ref.at[pl.ds(pl.multiple_of(first, RT), RT), :]

    def positions(t):
        return [pp_sm[(j * TB + t) * TOP_K + k] for k in range(TOP_K)]

    @pl.when(step == 0)
    def _dispatch():
        xs_scr[TOP_K * TB * RT:, :] = jnp.zeros((2 * CH * RT, LANES), F32)

        def scatter(t, carry):
            row = tile(h2_ref, t * RT)[...]
            for p in positions(t):
                tile(xs_scr, p)[...] = row
            return carry

        lax.fori_loop(0, TB, scatter, 0, unroll=8)

    def run_rows(ee, r0, n_left, nrows):
        blk = xs_scr.at[pl.ds(pl.multiple_of(r0 * RT, RT), nrows * RT), :]
        xin = _load_row_tiles(blk, nrows)
        xb = xin.astype(BF16)
        g = jnp.dot(xb, wg_ref[ee], preferred_element_type=F32)
        u = jnp.dot(xb, wu_ref[ee], preferred_element_type=F32)
        y = jnp.dot((_silu(g) * u).astype(BF16), wd_ref[ee], preferred_element_type=F32)
        rows = lax.broadcasted_iota(I32, (nrows, D), 0)
        _store_row_tiles(blk, jnp.where(rows < n_left, y, xin))

    for ee in range(MOE_EPS):
        e = step * MOE_EPS + ee
        st = off_sm[j * E + e]
        n = cnt_sm[j * E + e]
        nbig = n // (2 * CH)

        def big_body(i, carry, ee=ee, st=st, n=n):
            run_rows(ee, st + i * 2 * CH, n - i * 2 * CH, 2 * CH)
            return carry

        lax.fori_loop(0, nbig, big_body, 0)
        r1 = st + nbig * 2 * CH
        rem = n - nbig * 2 * CH

        @pl.when(rem > CH)
        def _(ee=ee, r1=r1, rem=rem):
            run_rows(ee, r1, rem, 2 * CH)

        @pl.when((rem > 0) & (rem <= CH))
        def _(ee=ee, r1=r1, rem=rem):
            run_rows(ee, r1, rem, CH)

    @pl.when(step == pl.num_programs(1) - 1)
    def _combine():
        def gather(t, carry):
            acc = None
            for k, p in enumerate(positions(t)):
                term = w_sm[(j * TB + t) * TOP_K + k] * tile(xs_scr, p)[...]
                acc = term if acc is None else acc + term
            tile(out_ref, t * RT)[...] = acc
            return carry

        lax.fori_loop(0, TB, gather, 0, unroll=16)


def _moe(h2, wts, pp, off, pc, wg, wu, wd, n_split):
    D = wg.shape[1]
    RT = D // LANES
    T = h2.shape[0] // RT
    E = N_EXPERTS
    EPS = MOE_EPS
    Ts = T // n_split
    TB = min(MOE_TB, Ts)
    nb = Ts // TB
    rows = TOP_K * TB + 2 * MOE_CH
    routed = None
    for s in range(n_split):
        tok = slice(s * Ts, (s + 1) * Ts)
        blk = slice(s * nb * E, (s + 1) * nb * E)
        in_specs = [pl.BlockSpec((TB * RT, LANES), lambda j, e, *_, s=s: (s * nb + j, 0),
                                 pipeline_mode=pl.Buffered(1)),
                    pl.BlockSpec((EPS, D, D_EXPERT), lambda j, e, *_: (e, 0, 0)),
                    pl.BlockSpec((EPS, D, D_EXPERT), lambda j, e, *_: (e, 0, 0)),
                    pl.BlockSpec((EPS, D_EXPERT, D), lambda j, e, *_: (e, 0, 0))]
        args = [pp[:, tok].T.reshape(-1), wts[:, tok].T.reshape(-1), off[blk], pc[blk],
                h2, wg, wu, wd]
        aliases = {}
        if routed is not None:
            in_specs.append(pl.BlockSpec(memory_space=pl.ANY))
            args.append(routed)
            aliases = {len(args) - 1: 0}
        grid_spec = pltpu.PrefetchScalarGridSpec(
            num_scalar_prefetch=4,
            grid=(nb, E // EPS),
            in_specs=in_specs,
            out_specs=pl.BlockSpec((TB * RT, LANES), lambda j, e, *_, s=s: (s * nb + j, 0),
                                   pipeline_mode=pl.Buffered(1)),
            scratch_shapes=[pltpu.VMEM((rows * RT, LANES), F32)],
        )
        routed = pl.pallas_call(
            functools.partial(_moe_kernel, TB=TB),
            grid_spec=grid_spec,
            out_shape=jax.ShapeDtypeStruct((T * RT, LANES), F32),
            input_output_aliases=aliases,
            compiler_params=_cparams(("arbitrary", "arbitrary"), MOE_VMEM_LIMIT),
            name="moe_experts",
        )(*args)
    return routed


def _final_kernel(x1_ref, routed_ref, h2_ref, gt_ref, wsg_ref, wsu_ref, wsd_ref, gf_ref, o_ref):
    tm = x1_ref.shape[0]
    hb = _load_row_tiles(h2_ref, tm).astype(BF16)
    g = jnp.dot(hb, wsg_ref[...], preferred_element_type=F32)
    u = jnp.dot(hb, wsu_ref[...], preferred_element_type=F32)
    shared = jnp.dot((_silu(g) * u).astype(BF16), wsd_ref[...], preferred_element_type=F32)
    xo = x1_ref[...] + gt_ref[...] * (_load_row_tiles(routed_ref, tm) + shared)
    o_ref[...] = _rms(xo) * gf_ref[...]


def _final(x1, routed, h2, mod3, wsg, wsu, wsd, g_final, S):
    T, D = x1.shape
    tm = FINAL_TILE
    per_b = S // tm
    full = lambda shape: pl.BlockSpec(shape, lambda i: (0, 0))
    return pl.pallas_call(
        _final_kernel,
        grid=(T // tm,),
        in_specs=[pl.BlockSpec((tm, D), lambda i: (i, 0)),
                  pl.BlockSpec((tm * D // LANES, LANES), lambda i: (i, 0)),
                  pl.BlockSpec((tm * D // LANES, LANES), lambda i: (i, 0)),
                  _mod_spec(D, per_b, MOD_GT2),
                  full(wsg.shape), full(wsu.shape), full(wsd.shape), full((1, D))],
        out_specs=pl.BlockSpec((tm, D), lambda i: (i, 0)),
        out_shape=jax.ShapeDtypeStruct((T, D), F32),
        compiler_params=_cparams(("arbitrary",)),
        name="final_out",
    )(x1, routed, h2, mod3, wsg, wsu, wsd, g_final.reshape(1, D))


def _pack_w_in_kernel(w_ref, o_ref):
    splits = [MLA_Q_RANK, MLA_KV_RANK, MLA_ROPE, RET_HEADS * RET_QK, RET_HEADS * RET_QK,
              RET_HEADS * RET_V, RET_HEADS * RET_V]
    gate_w = (w_ref.shape[1] - sum(splits)) // 2
    splits += [gate_w, gate_w]
    edges = [0] + [int(v) for v in np.cumsum(splits)]
    wcq, wckv, wkr, wrq, wrk, wrv, wrg, wga, wgb = [
        w_ref[:, edges[i]:edges[i + 1]] for i in range(len(splits))]
    rows = w_ref.shape[0]
    hm = MLA_ROPE // 2
    zl = jnp.zeros((rows, MLA_NOPE), F32)
    zr = jnp.zeros((rows, LANES - MLA_NOPE - MLA_ROPE), F32)
    pad = jnp.zeros((rows, Z_COLS - (Z_KR2 + LANES)), F32)
    w = jnp.concatenate([wrv, wrg, wga, wgb, wrq, wrk, wcq, wckv,
                         zl, wkr, zr, zl, wkr[:, hm:], wkr[:, :hm], zr, pad], axis=1)
    o_ref[...] = w.astype(BF16)


def _pack_w_in(w_in):
    D, n_in = w_in.shape
    tr = WPACK_ROWS
    return pl.pallas_call(
        _pack_w_in_kernel,
        grid=(D // tr,),
        in_specs=[pl.BlockSpec((tr, n_in), lambda i: (i, 0))],
        out_specs=pl.BlockSpec((tr, Z_COLS), lambda i: (i, 0)),
        out_shape=jax.ShapeDtypeStruct((D, Z_COLS), BF16),
        compiler_params=_cparams(("arbitrary",)),
        name="pack_w_in",
    )(w_in)


def _pack_mla_weights(w_uq, w_ukv):
    H = MLA_HEADS
    hm = MLA_ROPE // 2
    wq = w_uq.reshape(MLA_Q_RANK, H, MLA_NOPE + MLA_ROPE)
    nope, pe = wq[..., :MLA_NOPE], wq[..., MLA_NOPE:]
    zpad = jnp.zeros((MLA_Q_RANK, H, LANES - MLA_NOPE - MLA_ROPE), w_uq.dtype)
    wq1 = jnp.concatenate([nope, pe, zpad], axis=-1).reshape(MLA_Q_RANK, H * LANES)
    wq2 = jnp.concatenate([jnp.zeros_like(nope), pe[..., hm:], pe[..., :hm], zpad],
                          axis=-1).reshape(MLA_Q_RANK, H * LANES)
    wkv = w_ukv.reshape(MLA_KV_RANK, H, MLA_NOPE + MLA_V)
    kn, vv = wkv[..., :MLA_NOPE], wkv[..., MLA_NOPE:]
    wk = jnp.concatenate([kn, jnp.zeros((MLA_KV_RANK, H, LANES - MLA_NOPE), w_ukv.dtype)],
                         axis=-1).reshape(MLA_KV_RANK, H * LANES)
    wv = vv.reshape(MLA_KV_RANK, H * MLA_V).T
    return wq1.T.astype(BF16), wq2.T.astype(BF16), wk.astype(BF16), wv.astype(BF16)


def kernel(x, c, positions, w_ada, b_ada, g_norm1, w_in, g_cq, w_uq, g_ckv, w_ukv, g_ret,
           w_o_mla, w_o_ret, w_out, g_norm2, w_router, b_router, w_exp_gate, w_exp_up,
           w_exp_down, w_sh_gate, w_sh_up, w_sh_down, g_final):
    B, S, D = x.shape
    T = B * S
    x2 = x.reshape(T, D)

    mod = _ada(c, w_ada, b_ada)
    mod3 = mod.reshape(B * N_MOD, 1, D)
    cr, sr, cm, sm = _rope_tables(positions)

    z = _inproj(x2, mod3, g_norm1, _pack_w_in(w_in), S)
    wq1, wq2, wk, wv = _pack_mla_weights(w_uq, w_ukv)
    q, k, v = _mla_up(z, cm, sm, g_cq, g_ckv, wq1, wq2, wk, wv)
    o_mla = _attention(q, k, v, B, S)
    o_ret = _retention(z, cr, sr, g_ret, B, S)
    x1 = _merge(x2, o_mla, o_ret, z, mod3, w_o_mla.astype(BF16), w_o_ret.astype(BF16),
                w_out.astype(BF16), S)

    h2, wts, pp, off_rep, pc_rep = _router(x1, mod3, g_norm2, w_router, b_router, S)
    routed = _moe(h2, wts, pp, off_rep[:, 0], pc_rep[:, 0],
                  w_exp_gate.astype(BF16), w_exp_up.astype(BF16), w_exp_down.astype(BF16), n_split=B)
    out = _final(x1, routed, h2, mod3, w_sh_gate.astype(BF16), w_sh_up.astype(BF16),
                 w_sh_down.astype(BF16), g_final, S)
    return out.reshape(B, S, D)
```

```python
import functools

import numpy as np
import jax
import jax.numpy as jnp
from jax import lax
from jax.experimental import pallas as pl
from jax.experimental.pallas import tpu as pltpu

F32 = jnp.float32
BF16 = jnp.bfloat16
I32 = jnp.int32

MLA_HEADS = 8
MLA_Q_RANK = 384
MLA_KV_RANK = 256
MLA_NOPE = 64
MLA_ROPE = 32
MLA_V = 64
RET_HEADS = 4
RET_QK = 128
RET_V = 256
RET_BLOCK = 512
ROPE_THETA = 10000.0
N_EXPERTS = 64
TOP_K = 8
N_GROUPS = 8
TOPK_GROUPS = 4
D_EXPERT = 256
ROUTED_SCALE = 2.5
RMS_EPS = 1e-6
GN_EPS = 1e-5

LANES = 128
SUBLANES = 8
VMEM_LIMIT = 56 * 1024 * 1024

Z_RV, Z_RG, Z_GA, Z_GB = 0, 1024, 2048, 3072
Z_RQ, Z_RK = 4096, 4608
Z_CQKV = 5120
Z_KR1, Z_KR2 = 5760, 5888
Z_COLS = 6144

LOG2E = 1.4426950408889634

MOD_SH1, MOD_SC1, MOD_GT1, MOD_SH2, MOD_SC2, MOD_GT2 = range(6)
N_MOD = 6

ROW_TILE = 512
FINAL_TILE = 1024
MLA_UP_TILE = 1024
ROPE_TILE = 2048
INPROJ_TM, INPROJ_TN = 1024, 2048
WPACK_ROWS = 256

ATTN_BQ = 512
ATTN_BK = 512
ATTN_HEADS = 8
ATTN_LROWS = 16

MOE_TB = 1024
MOE_CH = 144
MOE_EPS = 4
MOE_VMEM_LIMIT = 58 * 1024 * 1024


def _cparams(sem, vmem_limit=VMEM_LIMIT):
    return pltpu.CompilerParams(dimension_semantics=sem, vmem_limit_bytes=vmem_limit)


def _mod_spec(D, per_b, which):
    return pl.BlockSpec((None, 1, D), lambda i, *_: ((i // per_b) * N_MOD + which, 0, 0))


def _rms(x):
    return x * lax.rsqrt(jnp.mean(x * x, axis=-1, keepdims=True) + RMS_EPS)


def _silu(x):
    return x * jax.nn.sigmoid(x)


def _load_row_tiles(ref, nrows):
    nchunk = ref.shape[0] // nrows
    return jnp.concatenate([ref[pl.ds(c, nrows, stride=nchunk), :] for c in range(nchunk)], axis=1)


def _store_row_tiles(ref, val):
    nrows, d = val.shape
    nchunk = d // LANES
    for c in range(nchunk):
        ref[pl.ds(c, nrows, stride=nchunk), :] = val[:, c * LANES:(c + 1) * LANES]


def _ada_kernel(c_ref, w_ref, b_ref, o_ref):
    c = c_ref[...]
    o_ref[...] = jnp.dot(_silu(c).astype(BF16), w_ref[...].astype(BF16),
                         preferred_element_type=F32) + b_ref[...]


def _ada(c, w_ada, b_ada):
    B, D = c.shape
    n_out = w_ada.shape[1]
    cp = jnp.zeros((SUBLANES, D), F32).at[:B].set(c)
    tn = D
    out = pl.pallas_call(
        _ada_kernel,
        grid=(n_out // tn,),
        in_specs=[pl.BlockSpec((SUBLANES, D), lambda j: (0, 0)),
                  pl.BlockSpec((D, tn), lambda j: (0, j)),
                  pl.BlockSpec((1, tn), lambda j: (0, j))],
        out_specs=pl.BlockSpec((SUBLANES, tn), lambda j: (0, j)),
        out_shape=jax.ShapeDtypeStruct((SUBLANES, n_out), F32),
        compiler_params=_cparams(("arbitrary",)),
        name="ada_mod",
    )(cp, w_ada, b_ada.reshape(1, n_out))
    return out[:B]


def _rope_kernel(pos_ref, inv_ref, cr_ref, sr_ref, cm_ref, sm_ref):
    ang = pos_ref[...].astype(F32) * inv_ref[...]
    c = jnp.cos(ang)
    s = jnp.sin(ang)
    lane = lax.broadcasted_iota(I32, c.shape, 1)
    half = RET_QK // 2
    cr_ref[...] = jnp.where(lane < half, c, pltpu.roll(c, half, 1))
    sr_ref[...] = jnp.where(lane < half, -s, pltpu.roll(s, half, 1))
    hm = MLA_ROPE // 2
    cm_ref[...] = jnp.where(lane < MLA_NOPE, 1.0,
                            jnp.where(lane < MLA_NOPE + hm, c,
                                      jnp.where(lane < MLA_NOPE + 2 * hm, pltpu.roll(c, hm, 1), 0.0)))
    sm_ref[...] = jnp.where(lane < MLA_NOPE, 0.0,
                            jnp.where(lane < MLA_NOPE + hm, -s,
                                      jnp.where(lane < MLA_NOPE + 2 * hm, pltpu.roll(s, hm, 1), 0.0)))


def _rope_tables(positions):
    T = positions.size
    tm = min(T, ROPE_TILE)
    inv_r = 1.0 / (ROPE_THETA ** (jnp.arange(0, RET_QK, 2, dtype=F32) / RET_QK))
    inv_m = 1.0 / (ROPE_THETA ** (jnp.arange(0, MLA_ROPE, 2, dtype=F32) / MLA_ROPE))
    inv = jnp.zeros((1, LANES), F32).at[0, :RET_QK // 2].set(inv_r)
    inv = inv.at[0, MLA_NOPE:MLA_NOPE + MLA_ROPE // 2].set(inv_m)
    tab = jax.ShapeDtypeStruct((T, LANES), F32)
    spec = pl.BlockSpec((tm, LANES), lambda i: (i, 0))
    return pl.pallas_call(
        _rope_kernel,
        grid=(T // tm,),
        in_specs=[pl.BlockSpec((tm, 1), lambda i: (i, 0)),
                  pl.BlockSpec((1, LANES), lambda i: (0, 0))],
        out_specs=[spec, spec, spec, spec],
        out_shape=[tab, tab, tab, tab],
        compiler_params=_cparams(("arbitrary",)),
        name="rope_tables",
    )(positions.reshape(T, 1), inv)


def _inproj_kernel(x_ref, sc_ref, sh_ref, g_ref, w_ref, z_ref, h_scr):
    @pl.when(pl.program_id(1) == 0)
    def _():
        h = _rms(x_ref[...]) * g_ref[...] * (1.0 + sc_ref[...]) + sh_ref[...]
        h_scr[...] = h.astype(BF16)

    z_ref[...] = jnp.dot(h_scr[...], w_ref[...], preferred_element_type=F32).astype(z_ref.dtype)


def _inproj(x2, mod3, g_norm1, w_pack, S):
    T, D = x2.shape
    N = w_pack.shape[1]
    tm, tn = INPROJ_TM, INPROJ_TN
    per_b = S // tm
    return pl.pallas_call(
        _inproj_kernel,
        grid=(T // tm, N // tn),
        in_specs=[pl.BlockSpec((tm, D), lambda i, j: (i, 0)),
                  _mod_spec(D, per_b, MOD_SC1), _mod_spec(D, per_b, MOD_SH1),
                  pl.BlockSpec((1, D), lambda i, j: (0, 0)),
                  pl.BlockSpec((D, tn), lambda i, j: (0, j))],
        out_specs=pl.BlockSpec((tm, tn), lambda i, j: (i, j)),
        out_shape=jax.ShapeDtypeStruct((T, N), BF16),
        scratch_shapes=[pltpu.VMEM((tm, D), BF16)],
        compiler_params=_cparams(("arbitrary", "arbitrary")),
        name="in_proj",
    )(x2, mod3, mod3, g_norm1.reshape(1, D), w_pack)


def _mla_up_kernel(zc_ref, kr1_ref, kr2_ref, cm_ref, sm_ref, gq_ref, gkv_ref,
                   wq1_ref, wq2_ref, wk_ref, wv_ref, q_ref, k_ref, v_ref):
    zc = zc_ref[...].astype(F32)
    cqn = (_rms(zc[:, :MLA_Q_RANK]) * gq_ref[...]).astype(BF16)
    ckvn = (_rms(zc[:, MLA_Q_RANK:]) * gkv_ref[...]).astype(BF16)
    cm = cm_ref[...]
    sm = sm_ref[...]
    nt = (((1,), (1,)), ((), ()))
    q1t = lax.dot_general(wq1_ref[...], cqn, nt, preferred_element_type=F32)
    q2t = lax.dot_general(wq2_ref[...], cqn, nt, preferred_element_type=F32)
    cmt, smt = cm.T, sm.T
    kn = jnp.dot(ckvn, wk_ref[...], preferred_element_type=F32)
    kpe = kr1_ref[...].astype(F32) * cm + kr2_ref[...].astype(F32) * sm
    qscale = (MLA_NOPE + MLA_ROPE) ** -0.5 * LOG2E
    for h in range(MLA_HEADS):
        sl = slice(h * LANES, (h + 1) * LANES)
        q_ref[sl, :] = ((q1t[sl, :] * cmt + q2t[sl, :] * smt) * qscale).astype(BF16)
        k_ref[:, sl] = (kn[:, sl] + kpe).astype(BF16)
    vt = lax.dot_general(wv_ref[...], ckvn, (((1,), (1,)), ((), ())),
                         preferred_element_type=F32).astype(BF16)
    vrows = MLA_V + ATTN_LROWS
    for h in range(MLA_HEADS):
        v_ref[h * vrows:h * vrows + MLA_V, :] = vt[h * MLA_V:(h + 1) * MLA_V, :]
        v_ref[h * vrows + MLA_V:(h + 1) * vrows, :] = jnp.ones((ATTN_LROWS, vt.shape[1]), BF16)


def _mla_up(z, cm, sm, g_cq, g_ckv, wq1, wq2, wk, wv):
    T = z.shape[0]
    tm = MLA_UP_TILE
    HW = MLA_HEADS * LANES
    wc = MLA_Q_RANK + MLA_KV_RANK
    full = lambda shape: pl.BlockSpec(shape, lambda i: (0, 0))
    return pl.pallas_call(
        _mla_up_kernel,
        grid=(T // tm,),
        in_specs=[pl.BlockSpec((tm, wc), lambda i: (i, Z_CQKV // wc)),
                  pl.BlockSpec((tm, LANES), lambda i: (i, Z_KR1 // LANES)),
                  pl.BlockSpec((tm, LANES), lambda i: (i, Z_KR2 // LANES)),
                  pl.BlockSpec((tm, LANES), lambda i: (i, 0)),
                  pl.BlockSpec((tm, LANES), lambda i: (i, 0)),
                  full((1, MLA_Q_RANK)), full((1, MLA_KV_RANK)),
                  full(wq1.shape), full(wq2.shape), full(wk.shape), full(wv.shape)],
        out_specs=[pl.BlockSpec((HW, tm), lambda i: (0, i)),
                   pl.BlockSpec((tm, HW), lambda i: (i, 0)),
                   pl.BlockSpec((MLA_HEADS * (MLA_V + ATTN_LROWS), tm), lambda i: (0, i))],
        out_shape=[jax.ShapeDtypeStruct((HW, T), BF16),
                   jax.ShapeDtypeStruct((T, HW), BF16),
                   jax.ShapeDtypeStruct((MLA_HEADS * (MLA_V + ATTN_LROWS), T), BF16)],
        compiler_params=_cparams(("arbitrary",)),
        name="mla_up",
    )(z, z, z, cm, sm, g_cq.reshape(1, -1), g_ckv.reshape(1, -1), wq1, wq2, wk, wv)


def _attn_kernel(q_ref, k_ref, vt_ref, o_ref, *scr, bq, bk):
    nh = ATTN_HEADS
    slots = (scr[:nh], scr[nh:2 * nh])
    p_scrs, acc_scr, m_scr = scr[2 * nh:3 * nh], scr[3 * nh], scr[3 * nh + 1]
    mx_scrs = scr[3 * nh + 2:3 * nh + 4]
    qi = pl.program_id(2)
    vrows = MLA_V + ATTN_LROWS
    qts = [q_ref[h * LANES:(h + 1) * LANES, :] for h in range(nh)]
    acc_scr[...] = jnp.zeros(acc_scr.shape, F32)
    m_scr[...] = jnp.full(m_scr.shape, -jnp.inf, F32)
    sub8 = lax.broadcasted_iota(I32, (SUBLANES, bq), 0)
    lane8 = lax.broadcasted_iota(I32, (SUBLANES, bq), 1)
    pack = 2 * SUBLANES

    def scores(kb, slot):
        k0 = pl.multiple_of(kb * bk, bk)
        for h in range(nh):
            s = jnp.dot(k_ref[pl.ds(k0, bk), h * LANES:(h + 1) * LANES], qts[h],
                        preferred_element_type=F32)
            slots[slot][h][...] = s
            mx_scrs[slot][h * SUBLANES:(h + 1) * SUBLANES, :] = jnp.max(
                s.reshape(bk // SUBLANES, SUBLANES, bq), axis=0)

    def update(kb, slot, masked):
        k0 = pl.multiple_of(kb * bk, bk)
        for h in range(nh):
            st, pr = slots[slot][h], p_scrs[h]
            if masked:
                parts = [None, None]
                for r in range(bk // SUBLANES):
                    rs = slice(r * SUBLANES, (r + 1) * SUBLANES)
                    x = jnp.where(lane8 >= sub8 + r * SUBLANES, st[rs, :], -jnp.inf)
                    st[rs, :] = x
                    parts[r % 2] = x if parts[r % 2] is None else jnp.maximum(parts[r % 2], x)
                part = jnp.maximum(parts[0], parts[1])
            else:
                part = mx_scrs[slot][h * SUBLANES:(h + 1) * SUBLANES, :]
            m_cur = jnp.max(part, axis=0, keepdims=True)
            m_prev = m_scr[h:h + 1, :]
            m_new = jnp.maximum(m_prev, m_cur)
            m_scr[h:h + 1, :] = m_new
            alpha = jnp.exp2(m_prev - m_new)
            for r in range(bk // pack):
                rs = slice(r * pack, (r + 1) * pack)
                pr[rs, :] = jnp.exp2(st[rs, :] - m_new).astype(BF16)
            rows = slice(h * vrows, (h + 1) * vrows)
            acc_scr[rows, :] = alpha * acc_scr[rows, :] + jnp.dot(
                vt_ref[rows, pl.ds(k0, bk)], pr[...], preferred_element_type=F32)

    scores(0, 0)
    npairs = qi // 2

    def pair(i, carry):
        kb = 2 * i
        scores(kb + 1, 1)
        update(kb, 0, False)
        scores(kb + 2, 0)
        update(kb + 1, 1, False)
        return carry

    lax.fori_loop(0, npairs, pair, 0)

    @pl.when(qi % 2 == 0)
    def _():
        update(qi, 0, True)

    @pl.when(qi % 2 == 1)
    def _():
        scores(qi, 1)
        update(qi - 1, 0, False)
        update(qi, 1, True)

    ot = jnp.concatenate(
        [acc_scr[h * vrows:h * vrows + MLA_V, :]
         / acc_scr[h * vrows + MLA_V:h * vrows + MLA_V + 1, :] for h in range(nh)], axis=0)
    o_ref[...] = ot.T.astype(o_ref.dtype)


def _attention(qt, k, vt, B, S):
    T = k.shape[0]
    bq = min(ATTN_BQ, S)
    bk = min(ATTN_BK, S)
    nq = S // bq
    nh = ATTN_HEADS
    vrows = MLA_V + ATTN_LROWS
    assert bq == bk and nh <= SUBLANES
    kern = functools.partial(_attn_kernel, bq=bq, bk=bk)
    return pl.pallas_call(
        kern,
        grid=(B, MLA_HEADS // nh, nq),
        in_specs=[pl.BlockSpec((nh * LANES, bq), lambda b, p, i: (p, b * nq + i)),
                  pl.BlockSpec((S, nh * LANES), lambda b, p, i: (b, p), pipeline_mode=pl.Buffered(1)),
                  pl.BlockSpec((nh * vrows, S), lambda b, p, i: (p, b), pipeline_mode=pl.Buffered(1))],
        out_specs=pl.BlockSpec((bq, nh * MLA_V), lambda b, p, i: (b * nq + i, p)),
        out_shape=jax.ShapeDtypeStruct((T, MLA_HEADS * MLA_V), BF16),
        scratch_shapes=([pltpu.VMEM((bk, bq), F32)] * (2 * nh) + [pltpu.VMEM((bk, bq), BF16)] * nh
                        + [pltpu.VMEM((nh * vrows, bq), F32), pltpu.VMEM((SUBLANES, bq), F32)]
                        + [pltpu.VMEM((nh * SUBLANES, bq), F32)] * 2),
        compiler_params=_cparams(("arbitrary", "arbitrary", "arbitrary")),
        name="mla_attention",
    )(qt, k, vt)


def _ret_kernel(rq_ref, rk_ref, rv_ref, rg_ref, cr_ref, sr_ref, dec_ref, xi_ref, zeta_ref,
                g_ref, o_ref, state_scr):
    @pl.when(pl.program_id(1) == 0)
    def _():
        state_scr[...] = jnp.zeros(state_scr.shape, F32)

    C = rq_ref.shape[0]
    cr = cr_ref[...]
    sr = sr_ref[...]
    half = RET_QK // 2
    for h in range(RET_HEADS):
        qk = slice(h * RET_QK, (h + 1) * RET_QK)
        vv = slice(h * RET_V, (h + 1) * RET_V)
        rq = rq_ref[:, qk].astype(F32)
        rk = rk_ref[:, qk].astype(F32)
        q = rq * cr + pltpu.roll(rq, half, 1) * sr
        k = (rk * cr + pltpu.roll(rk, half, 1) * sr) * (RET_QK ** -0.5)
        v = rv_ref[:, vv]
        xi = xi_ref[h]
        state = state_scr[h]
        qb = q.astype(BF16)
        s = lax.dot_general(qb, k.astype(BF16), (((1,), (1,)), ((), ())),
                            preferred_element_type=F32) * dec_ref[h]
        inner = jnp.dot(s.astype(BF16), v, preferred_element_type=F32)
        cross = jnp.dot(qb, state.astype(BF16), preferred_element_type=F32) * xi
        kz = (k * zeta_ref[h]).astype(BF16)
        kv = lax.dot_general(kz, v, (((0,), (0,)), ((), ())), preferred_element_type=F32)
        state_scr[h] = xi[C - 1:C, :] * state + kv
        o = inner + cross
        mu = jnp.mean(o, axis=-1, keepdims=True)
        d = o - mu
        var = jnp.mean(d * d, axis=-1, keepdims=True)
        on = d * lax.rsqrt(var + GN_EPS) * g_ref[:, vv]
        o_ref[:, vv] = (_silu(rg_ref[:, vv].astype(F32)) * on).astype(o_ref.dtype)


def _retention_tables():
    C = RET_BLOCK
    h = np.arange(RET_HEADS, dtype=np.float64)
    log_g = np.log(1.0 - np.exp2(-5.0 - h))
    idx = np.arange(C, dtype=np.float64)
    diff = idx[:, None] - idx[None, :]
    decay = np.where(diff[None] >= 0, np.exp(np.maximum(diff, 0.0)[None] * log_g[:, None, None]), 0.0)
    zeta = np.exp((C - 1 - idx)[None, :] * log_g[:, None])
    xi = np.exp((idx + 1.0)[None, :] * log_g[:, None])
    zeta_rep = np.broadcast_to(zeta[:, :, None], (RET_HEADS, C, RET_QK))
    xi_rep = np.broadcast_to(xi[:, :, None], (RET_HEADS, C, RET_V))
    return (jnp.asarray(decay, F32), jnp.asarray(xi_rep, F32), jnp.asarray(zeta_rep, F32))


def _retention(z, cr, sr, g_ret, B, S):
    T = z.shape[0]
    C = RET_BLOCK
    N = S // C
    H = RET_HEADS
    WQ, WV = H * RET_QK, H * RET_V
    dec, xi, zeta = _retention_tables()
    row = lambda b, n: b * N + n
    const = lambda shape: pl.BlockSpec(shape, lambda b, n: (0,) * len(shape))
    return pl.pallas_call(
        _ret_kernel,
        grid=(B, N),
        in_specs=[pl.BlockSpec((C, WQ), lambda b, n: (row(b, n), Z_RQ // WQ)),
                  pl.BlockSpec((C, WQ), lambda b, n: (row(b, n), Z_RK // WQ)),
                  pl.BlockSpec((C, WV), lambda b, n: (row(b, n), Z_RV // WV)),
                  pl.BlockSpec((C, WV), lambda b, n: (row(b, n), Z_RG // WV)),
                  pl.BlockSpec((C, LANES), lambda b, n: (row(b, n), 0)),
                  pl.BlockSpec((C, LANES), lambda b, n: (row(b, n), 0)),
                  const((H, C, C)), const((H, C, RET_V)), const((H, C, RET_QK)), const((1, WV))],
        out_specs=pl.BlockSpec((C, WV), lambda b, n: (row(b, n), 0)),
        out_shape=jax.ShapeDtypeStruct((T, WV), BF16),
        scratch_shapes=[pltpu.VMEM((H, RET_QK, RET_V), F32)],
        compiler_params=_cparams(("arbitrary", "arbitrary")),
        name="retention",
    )(z, z, z, z, cr, sr, dec, xi, zeta, g_ret.reshape(1, -1))


def _merge_kernel(x_ref, oa_ref, ob_ref, ga_ref, gb_ref, gt_ref, wa_ref, wb_ref, wo_ref, x1_ref):
    a = jnp.dot(oa_ref[...], wa_ref[...], preferred_element_type=F32)
    b = jnp.dot(ob_ref[...], wb_ref[...], preferred_element_type=F32)
    merged = (jax.nn.sigmoid(ga_ref[...].astype(F32)) * a
              + jax.nn.sigmoid(gb_ref[...].astype(F32)) * b)
    y = jnp.dot(merged.astype(BF16), wo_ref[...], preferred_element_type=F32)
    x1_ref[...] = x_ref[...] + gt_ref[...] * y


def _merge(x2, o_mla, o_ret, z, mod3, wa, wb, wo, S):
    T, D = x2.shape
    tm = ROW_TILE
    per_b = S // tm
    full = lambda shape: pl.BlockSpec(shape, lambda i: (0, 0))
    return pl.pallas_call(
        _merge_kernel,
        grid=(T // tm,),
        in_specs=[pl.BlockSpec((tm, D), lambda i: (i, 0)),
                  pl.BlockSpec((tm, o_mla.shape[1]), lambda i: (i, 0)),
                  pl.BlockSpec((tm, D), lambda i: (i, 0)),
                  pl.BlockSpec((tm, D), lambda i: (i, Z_GA // D)),
                  pl.BlockSpec((tm, D), lambda i: (i, Z_GB // D)),
                  _mod_spec(D, per_b, MOD_GT1),
                  full(wa.shape), full(wb.shape), full(wo.shape)],
        out_specs=pl.BlockSpec((tm, D), lambda i: (i, 0)),
        out_shape=jax.ShapeDtypeStruct((T, D), F32),
        compiler_params=_cparams(("arbitrary",)),
        name="merge_out",
    )(x2, o_mla, o_ret, z, z, mod3, wa, wb, wo)


def _router_kernel(x1_ref, sc_ref, sh_ref, g_ref, wr_ref, br_ref,
                   h2_ref, wts_ref, pp_ref, off_ref, pc_ref):
    TB = x1_ref.shape[0]
    D = x1_ref.shape[1]
    E, G = N_EXPERTS, N_GROUPS
    per = E // G
    h2 = _rms(x1_ref[...]) * g_ref[...] * (1.0 + sc_ref[...]) + sh_ref[...]
    _store_row_tiles(h2_ref, h2)
    logits = lax.dot_general(wr_ref[...], h2, (((1,), (1,)), ((), ())),
                             precision=lax.Precision.HIGHEST,
                             preferred_element_type=F32)
    s = jax.nn.sigmoid(logits)
    biased = s + br_ref[...]
    sub = lax.broadcasted_iota(I32, (per, TB), 0)
    neg = -jnp.inf

    def first_argmax(vals, m, idx, sentinel):
        return jnp.min(jnp.where(vals == m, idx, sentinel), axis=0, keepdims=True)

    bg = [biased[g * per:(g + 1) * per, :] for g in range(G)]
    sg = [s[g * per:(g + 1) * per, :] for g in range(G)]
    gscore = []
    for g in range(G):
        m1 = jnp.max(bg[g], axis=0, keepdims=True)
        i1 = first_argmax(bg[g], m1, sub, per)
        m2 = jnp.max(jnp.where(sub == i1, neg, bg[g]), axis=0, keepdims=True)
        gscore.append(m1 + m2)
    gs = jnp.concatenate(gscore, axis=0)
    gidx = lax.broadcasted_iota(I32, (G, TB), 0)
    gsel = jnp.zeros((G, TB), F32)
    for _ in range(TOPK_GROUPS):
        m = jnp.max(gs, axis=0, keepdims=True)
        i = first_argmax(gs, m, gidx, G)
        hit = gidx == i
        gsel = jnp.where(hit, 1.0, gsel)
        gs = jnp.where(hit, neg, gs)
    cand = [jnp.where(gsel[g:g + 1, :] > 0.0, bg[g], neg) for g in range(G)]
    eidx = [sub + g * per for g in range(G)]
    sel = [jnp.zeros((per, TB), F32) for _ in range(G)]
    top_i, top_w = [], []
    for _ in range(TOP_K):
        m = functools.reduce(jnp.maximum, [jnp.max(c, axis=0, keepdims=True) for c in cand])
        i = functools.reduce(jnp.minimum,
                             [first_argmax(cand[g], m, eidx[g], E) for g in range(G)])
        w = jnp.zeros((1, TB), F32)
        for g in range(G):
            hit = eidx[g] == i
            w = w + jnp.sum(jnp.where(hit, sg[g], 0.0), axis=0, keepdims=True)
            sel[g] = jnp.where(hit, 1.0, sel[g])
            cand[g] = jnp.where(hit, neg, cand[g])
        top_i.append(i)
        top_w.append(w)
    wsum = functools.reduce(lambda a, b: a + b, top_w)
    wts_ref[...] = jnp.concatenate([w / wsum * ROUTED_SCALE for w in top_w], axis=0)

    mask = jnp.concatenate(sel, axis=0)
    t_row = lax.broadcasted_iota(I32, (TB, TB), 0)
    t_col = lax.broadcasted_iota(I32, (TB, TB), 1)
    before = jnp.where(t_row < t_col, 1.0, 0.0).astype(BF16)
    rank = jnp.dot(mask.astype(BF16), before, preferred_element_type=F32)
    cnt = jnp.sum(mask, axis=1, keepdims=True)
    pc_rep = jnp.broadcast_to(cnt, (E, LANES))
    e_row = lax.broadcasted_iota(I32, (E, E), 0)
    e_col = lax.broadcasted_iota(I32, (E, E), 1)
    lower = jnp.where(e_col < e_row, 1.0, 0.0)
    off_rep = jnp.dot(lower, pc_rep, precision=lax.Precision.HIGHEST,
                      preferred_element_type=F32)
    off_ref[...] = off_rep.astype(I32)
    pc_ref[...] = pc_rep.astype(I32)
    posfull = off_rep[:, :1] + rank
    pos = []
    for kk in range(TOP_K):
        p = jnp.zeros((1, TB), F32)
        for g in range(G):
            p = p + jnp.sum(jnp.where(eidx[g] == top_i[kk], posfull[g * per:(g + 1) * per, :], 0.0),
                            axis=0, keepdims=True)
        pos.append(p.astype(I32))
    pp_ref[...] = jnp.concatenate(pos, axis=0) * (D // LANES)


def _router(x1, mod3, g_norm2, w_router, b_router, S):
    T, D = x1.shape
    TB = min(MOE_TB, T)
    nb = T // TB
    per_b = S // TB
    E = N_EXPERTS
    return pl.pallas_call(
        _router_kernel,
        grid=(nb,),
        in_specs=[pl.BlockSpec((TB, D), lambda i: (i, 0)),
                  _mod_spec(D, per_b, MOD_SC2), _mod_spec(D, per_b, MOD_SH2),
                  pl.BlockSpec((1, D), lambda i: (0, 0)),
                  pl.BlockSpec((E, D), lambda i: (0, 0)),
                  pl.BlockSpec((E, 1), lambda i: (0, 0))],
        out_specs=[pl.BlockSpec((TB * D // LANES, LANES), lambda i: (i, 0)),
                   pl.BlockSpec((TOP_K, TB), lambda i: (0, i)),
                   pl.BlockSpec((TOP_K, TB), lambda i: (0, i)),
                   pl.BlockSpec((E, LANES), lambda i: (i, 0)),
                   pl.BlockSpec((E, LANES), lambda i: (i, 0))],
        out_shape=[jax.ShapeDtypeStruct((T * D // LANES, LANES), F32),
                   jax.ShapeDtypeStruct((TOP_K, T), F32),
                   jax.ShapeDtypeStruct((TOP_K, T), I32),
                   jax.ShapeDtypeStruct((nb * E, LANES), I32),
                   jax.ShapeDtypeStruct((nb * E, LANES), I32)],
        compiler_params=_cparams(("arbitrary",)),
        name="moe_router",
    )(x1, mod3, mod3, g_norm2.reshape(1, D), w_router.T, b_router.reshape(E, 1))


def _moe_kernel(pp_sm, w_sm, off_sm, cnt_sm, h2_ref, wg_ref, wu_ref, wd_ref, *rest, TB):
    out_ref, xs_scr = rest[-2:]
    j = pl.program_id(0)
    step = pl.program_id(1)
    E = N_EXPERTS
    D = wg_ref.shape[1]
    RT = D // LANES
    CH = MOE_CH

    def tile(ref, first):
        return ref.at[pl.ds(pl.multiple_of(first, RT), RT), :]

    def positions(t):
        return [pp_sm[(j * TB + t) * TOP_K + k] for k in range(TOP_K)]

    @pl.when(step == 0)
    def _dispatch():
        xs_scr[TOP_K * TB * RT:, :] = jnp.zeros((2 * CH * RT, LANES), F32)

        def scatter(t, carry):
            row = tile(h2_ref, t * RT)[...]
            for p in positions(t):
                tile(xs_scr, p)[...] = row
            return carry

        lax.fori_loop(0, TB, scatter, 0, unroll=8)

    def run_rows(ee, r0, n_left, nrows):
        blk = xs_scr.at[pl.ds(pl.multiple_of(r0 * RT, RT), nrows * RT), :]
        xin = _load_row_tiles(blk, nrows)
        xb = xin.astype(BF16)
        g = jnp.dot(xb, wg_ref[ee], preferred_element_type=F32)
        u = jnp.dot(xb, wu_ref[ee], preferred_element_type=F32)
        y = jnp.dot((_silu(g) * u).astype(BF16), wd_ref[ee], preferred_element_type=F32)
        rows = lax.broadcasted_iota(I32, (nrows, D), 0)
        _store_row_tiles(blk, jnp.where(rows < n_left, y, xin))

    for ee in range(MOE_EPS):
        e = step * MOE_EPS + ee
        st = off_sm[j * E + e]
        n = cnt_sm[j * E + e]
        nbig = n // (2 * CH)

        def big_body(i, carry, ee=ee, st=st, n=n):
            run_rows(ee, st + i * 2 * CH, n - i * 2 * CH, 2 * CH)
            return carry

        lax.fori_loop(0, nbig, big_body, 0)
        r1 = st + nbig * 2 * CH
        rem = n - nbig * 2 * CH

        @pl.when(rem > CH)
        def _(ee=ee, r1=r1, rem=rem):
            run_rows(ee, r1, rem, 2 * CH)

        @pl.when((rem > 0) & (rem <= CH))
        def _(ee=ee, r1=r1, rem=rem):
            run_rows(ee, r1, rem, CH)

    @pl.when(step == pl.num_programs(1) - 1)
    def _combine():
        def gather(t, carry):
            acc = None
            for k, p in enumerate(positions(t)):
                term = w_sm[(j * TB + t) * TOP_K + k] * tile(xs_scr, p)[...]
                acc = term if acc is None else acc + term
            tile(out_ref, t * RT)[...] = acc
            return carry

        lax.fori_loop(0, TB, gather, 0, unroll=16)


def _moe(h2, wts, pp, off, pc, wg, wu, wd, n_split):
    D = wg.shape[1]
    RT = D // LANES
    T = h2.shape[0] // RT
    E = N_EXPERTS
    EPS = MOE_EPS
    Ts = T // n_split
    TB = min(MOE_TB, Ts)
    nb = Ts // TB
    rows = TOP_K * TB + 2 * MOE_CH
    routed = None
    for s in range(n_split):
        tok = slice(s * Ts, (s + 1) * Ts)
        blk = slice(s * nb * E, (s + 1) * nb * E)
        in_specs = [pl.BlockSpec((TB * RT, LANES), lambda j, e, *_, s=s: (s * nb + j, 0),
                                 pipeline_mode=pl.Buffered(1)),
                    pl.BlockSpec((EPS, D, D_EXPERT), lambda j, e, *_: (e, 0, 0)),
                    pl.BlockSpec((EPS, D, D_EXPERT), lambda j, e, *_: (e, 0, 0)),
                    pl.BlockSpec((EPS, D_EXPERT, D), lambda j, e, *_: (e, 0, 0))]
        args = [pp[:, tok].T.reshape(-1), wts[:, tok].T.reshape(-1), off[blk], pc[blk],
                h2, wg, wu, wd]
        aliases = {}
        if routed is not None:
            in_specs.append(pl.BlockSpec(memory_space=pl.ANY))
            args.append(routed)
            aliases = {len(args) - 1: 0}
        grid_spec = pltpu.PrefetchScalarGridSpec(
            num_scalar_prefetch=4,
            grid=(nb, E // EPS),
            in_specs=in_specs,
            out_specs=pl.BlockSpec((TB * RT, LANES), lambda j, e, *_, s=s: (s * nb + j, 0),
                                   pipeline_mode=pl.Buffered(1)),
            scratch_shapes=[pltpu.VMEM((rows * RT, LANES), F32)],
        )
        routed = pl.pallas_call(
            functools.partial(_moe_kernel, TB=TB),
            grid_spec=grid_spec,
            out_shape=jax.ShapeDtypeStruct((T * RT, LANES), F32),
            input_output_aliases=aliases,
            compiler_params=_cparams(("arbitrary", "arbitrary"), MOE_VMEM_LIMIT),
            name="moe_experts",
        )(*args)
    return routed


def _final_kernel(x1_ref, routed_ref, h2_ref, gt_ref, wsg_ref, wsu_ref, wsd_ref, gf_ref, o_ref):
    tm = x1_ref.shape[0]
    hb = _load_row_tiles(h2_ref, tm).astype(BF16)
    g = jnp.dot(hb, wsg_ref[...], preferred_element_type=F32)
    u = jnp.dot(hb, wsu_ref[...], preferred_element_type=F32)
    shared = jnp.dot((_silu(g) * u).astype(BF16), wsd_ref[...], preferred_element_type=F32)
    xo = x1_ref[...] + gt_ref[...] * (_load_row_tiles(routed_ref, tm) + shared)
    o_ref[...] = _rms(xo) * gf_ref[...]


def _final(x1, routed, h2, mod3, wsg, wsu, wsd, g_final, S):
    T, D = x1.shape
    tm = FINAL_TILE
    per_b = S // tm
    full = lambda shape: pl.BlockSpec(shape, lambda i: (0, 0))
    return pl.pallas_call(
        _final_kernel,
        grid=(T // tm,),
        in_specs=[pl.BlockSpec((tm, D), lambda i: (i, 0)),
                  pl.BlockSpec((tm * D // LANES, LANES), lambda i: (i, 0)),
                  pl.BlockSpec((tm * D // LANES, LANES), lambda i: (i, 0)),
                  _mod_spec(D, per_b, MOD_GT2),
                  full(wsg.shape), full(wsu.shape), full(wsd.shape), full((1, D))],
        out_specs=pl.BlockSpec((tm, D), lambda i: (i, 0)),
        out_shape=jax.ShapeDtypeStruct((T, D), F32),
        compiler_params=_cparams(("arbitrary",)),
        name="final_out",
    )(x1, routed, h2, mod3, wsg, wsu, wsd, g_final.reshape(1, D))


def _pack_w_in_kernel(w_ref, o_ref):
    splits = [MLA_Q_RANK, MLA_KV_RANK, MLA_ROPE, RET_HEADS * RET_QK, RET_HEADS * RET_QK,
              RET_HEADS * RET_V, RET_HEADS * RET_V]
    gate_w = (w_ref.shape[1] - sum(splits)) // 2
    splits += [gate_w, gate_w]
    edges = [0] + [int(v) for v in np.cumsum(splits)]
    wcq, wckv, wkr, wrq, wrk, wrv, wrg, wga, wgb = [
        w_ref[:, edges[i]:edges[i + 1]] for i in range(len(splits))]
    rows = w_ref.shape[0]
    hm = MLA_ROPE // 2
    zl = jnp.zeros((rows, MLA_NOPE), F32)
    zr = jnp.zeros((rows, LANES - MLA_NOPE - MLA_ROPE), F32)
    pad = jnp.zeros((rows, Z_COLS - (Z_KR2 + LANES)), F32)
    w = jnp.concatenate([wrv, wrg, wga, wgb, wrq, wrk, wcq, wckv,
                         zl, wkr, zr, zl, wkr[:, hm:], wkr[:, :hm], zr, pad], axis=1)
    o_ref[...] = w.astype(BF16)


def _pack_w_in(w_in):
    D, n_in = w_in.shape
    tr = WPACK_ROWS
    return pl.pallas_call(
        _pack_w_in_kernel,
        grid=(D // tr,),
        in_specs=[pl.BlockSpec((tr, n_in), lambda i: (i, 0))],
        out_specs=pl.BlockSpec((tr, Z_COLS), lambda i: (i, 0)),
        out_shape=jax.ShapeDtypeStruct((D, Z_COLS), BF16),
        compiler_params=_cparams(("arbitrary",)),
        name="pack_w_in",
    )(w_in)


def _pack_mla_weights(w_uq, w_ukv):
    H = MLA_HEADS
    hm = MLA_ROPE // 2
    wq = w_uq.reshape(MLA_Q_RANK, H, MLA_NOPE + MLA_ROPE)
    nope, pe = wq[..., :MLA_NOPE], wq[..., MLA_NOPE:]
    zpad = jnp.zeros((MLA_Q_RANK, H, LANES - MLA_NOPE - MLA_ROPE), w_uq.dtype)
    wq1 = jnp.concatenate([nope, pe, zpad], axis=-1).reshape(MLA_Q_RANK, H * LANES)
    wq2 = jnp.concatenate([jnp.zeros_like(nope), pe[..., hm:], pe[..., :hm], zpad],
                          axis=-1).reshape(MLA_Q_RANK, H * LANES)
    wkv = w_ukv.reshape(MLA_KV_RANK, H, MLA_NOPE + MLA_V)
    kn, vv = wkv[..., :MLA_NOPE], wkv[..., MLA_NOPE:]
    wk = jnp.concatenate([kn, jnp.zeros((MLA_KV_RANK, H, LANES - MLA_NOPE), w_ukv.dtype)],
                         axis=-1).reshape(MLA_KV_RANK, H * LANES)
    wv = vv.reshape(MLA_KV_RANK, H * MLA_V).T
    return wq1.T.astype(BF16), wq2.T.astype(BF16), wk.astype(BF16), wv.astype(BF16)


def kernel(x, c, positions, w_ada, b_ada, g_norm1, w_in, g_cq, w_uq, g_ckv, w_ukv, g_ret,
           w_o_mla, w_o_ret, w_out, g_norm2, w_router, b_router, w_exp_gate, w_exp_up,
           w_exp_down, w_sh_gate, w_sh_up, w_sh_down, g_final):
    B, S, D = x.shape
    T = B * S
    x2 = x.reshape(T, D)

    mod = _ada(c, w_ada, b_ada)
    mod3 = mod.reshape(B * N_MOD, 1, D)
    cr, sr, cm, sm = _rope_tables(positions)

    z = _inproj(x2, mod3, g_norm1, _pack_w_in(w_in), S)
    wq1, wq2, wk, wv = _pack_mla_weights(w_uq, w_ukv)
    q, k, v = _mla_up(z, cm, sm, g_cq, g_ckv, wq1, wq2, wk, wv)
    o_mla = _attention(q, k, v, B, S)
    o_ret = _retention(z, cr, sr, g_ret, B, S)
    x1 = _merge(x2, o_mla, o_ret, z, mod3, w_o_mla.astype(BF16), w_o_ret.astype(BF16),
                w_out.astype(BF16), S)

    h2, wts, pp, off_rep, pc_rep = _router(x1, mod3, g_norm2, w_router, b_router, S)
    routed = _moe(h2, wts, pp, off_rep[:, 0], pc_rep[:, 0],
                  w_exp_gate.astype(BF16), w_exp_up.astype(BF16), w_exp_down.astype(BF16), n_split=B)
    out = _final(x1, routed, h2, mod3, w_sh_gate.astype(BF16), w_sh_up.astype(BF16),
                 w_sh_down.astype(BF16), g_final, S)
    return out.reshape(B, S, D)
```
